```python
import jax, jax.numpy as jnp
from jax import lax
import numpy as np

D_MODEL = 1024
BATCH = 2
SEQ = 8192
DEPTH = 1

MLA_HEADS = 8
Q_LORA_RANK = 256
KV_LORA_RANK = 128
QK_NOPE_DIM = 64
QK_ROPE_DIM = 32
V_HEAD_DIM = 64
MLA_SCALE = (QK_NOPE_DIM + QK_ROPE_DIM) ** -0.5
Q_BLOCK = 128

RET_HEADS = 8
RET_HEAD_DIM = 64
RET_CHUNK = 128

D_MIX = MLA_HEADS * V_HEAD_DIM + RET_HEADS * RET_HEAD_DIM
D_FF = 2816
ROPE_THETA = 10000.0
EPS = 1e-6

IN_SIZES = (Q_LORA_RANK, KV_LORA_RANK, QK_ROPE_DIM,
            RET_HEADS * RET_HEAD_DIM, RET_HEADS * RET_HEAD_DIM,
            RET_HEADS * RET_HEAD_DIM, RET_HEADS * RET_HEAD_DIM)
D_IN = sum(IN_SIZES)
SPLIT_POINTS = tuple(int(s) for s in np.cumsum(IN_SIZES)[:-1])

kernel_name = "hybrid_mla_retention_macaron_encoder"


def rms_norm(x, g):
    xf = x.astype(jnp.float32)
    y = xf * lax.rsqrt(jnp.mean(xf * xf, axis=-1, keepdims=True) + EPS)
    return (y * g.astype(jnp.float32)).astype(x.dtype)


def swiglu(x, w_gate, w_up, w_down):
    return (jax.nn.silu(x @ w_gate) * (x @ w_up)) @ w_down


def rope(x, pos_f):
    half = x.shape[-1] // 2
    inv = ROPE_THETA ** (-jnp.arange(half, dtype=jnp.float32) / half)
    ang = pos_f[..., None] * inv
    cos = jnp.cos(ang)[:, :, None, :]
    sin = jnp.sin(ang)[:, :, None, :]
    x1 = x[..., :half].astype(jnp.float32)
    x2 = x[..., half:].astype(jnp.float32)
    return jnp.concatenate([x1 * cos - x2 * sin, x1 * sin + x2 * cos], axis=-1).astype(x.dtype)


def mla_group(c_q, c_kv, k_pe_raw, pos_f, q_norm, w_uq, kv_norm, w_ukv):
    B, S, _ = c_q.shape
    H = MLA_HEADS
    q = (rms_norm(c_q, q_norm) @ w_uq).reshape(B, S, H, QK_NOPE_DIM + QK_ROPE_DIM)
    q_nope = q[..., :QK_NOPE_DIM]
    q_pe = rope(q[..., QK_NOPE_DIM:], pos_f)
    kv = (rms_norm(c_kv, kv_norm) @ w_ukv).reshape(B, S, H, QK_NOPE_DIM + V_HEAD_DIM)
    k_nope = kv[..., :QK_NOPE_DIM]
    v = kv[..., QK_NOPE_DIM:]
    k_pe = rope(k_pe_raw[:, :, None, :], pos_f)[:, :, 0]
    nb = S // Q_BLOCK

    def block(args):
        qn, qp = args
        s = (jnp.einsum('bqhd,bkhd->bhqk', qn, k_nope).astype(jnp.float32)
             + jnp.einsum('bqhr,bkr->bhqk', qp, k_pe).astype(jnp.float32))
        p = jax.nn.softmax(s * MLA_SCALE, axis=-1)
        return jnp.einsum('bhqk,bkhd->bqhd', p.astype(v.dtype), v)

    qn_b = q_nope.reshape(B, nb, Q_BLOCK, H, QK_NOPE_DIM).swapaxes(0, 1)
    qp_b = q_pe.reshape(B, nb, Q_BLOCK, H, QK_ROPE_DIM).swapaxes(0, 1)
    out = lax.map(block, (qn_b, qp_b))
    return out.swapaxes(0, 1).reshape(B, S, H * V_HEAD_DIM)


def retention_scan(q, k, v, log_gamma, inclusive):
    B, S, H, d = q.shape
    C = RET_CHUNK
    nc = S // C
    idx = jnp.arange(C, dtype=jnp.float32)
    rel = idx[:, None] - idx[None, :]
    mask = (rel >= 0) if inclusive else (rel > 0)
    d_inner = jnp.where(mask[None], jnp.exp(log_gamma[:, None, None] * jnp.maximum(rel, 0.0)[None]), 0.0)
    q_dec = jnp.exp(log_gamma[None, :] * (idx[:, None] + 1.0))
    k_dec = jnp.exp(log_gamma[None, :] * (C - 1.0 - idx)[:, None])
    c_dec = jnp.exp(log_gamma * C)

    def to_chunks(t):
        return t.reshape(B, nc, C, H, d).swapaxes(0, 1)

    def step(state, inp):
        qc, kc, vc = inp
        s = jnp.einsum('bihd,bjhd->bhij', qc, kc) * d_inner
        inner = jnp.einsum('bhij,bjhe->bihe', s, vc)
        cross = jnp.einsum('bihd,bhde->bihe', qc * q_dec[None, :, :, None], state)
        state = (state * c_dec[None, :, None, None]
                 + jnp.einsum('bjhd,bjhe->bhde', kc * k_dec[None, :, :, None], vc))
        return state, inner + cross

    state0 = jnp.zeros((B, H, d, d), jnp.float32)
    _, out = lax.scan(step, state0, (to_chunks(q), to_chunks(k), to_chunks(v)))
    return out.swapaxes(0, 1).reshape(B, S, H, d)


def retention_group(r_q, r_k, r_v, r_g, pos_f, decay_fwd, decay_bwd):
    B, S, _ = r_q.shape
    shp = (B, S, RET_HEADS, RET_HEAD_DIM)
    q = rope(r_q.reshape(shp), pos_f).astype(jnp.float32)
    k = rope(r_k.reshape(shp), pos_f).astype(jnp.float32) * (RET_HEAD_DIM ** -0.5)
    v = r_v.reshape(shp).astype(jnp.float32)
    lg_f = jax.nn.log_sigmoid(decay_fwd.astype(jnp.float32))
    lg_b = jax.nn.log_sigmoid(decay_bwd.astype(jnp.float32))
    fwd = retention_scan(q, k, v, lg_f, True)
    bwd = jnp.flip(retention_scan(jnp.flip(q, 1), jnp.flip(k, 1), jnp.flip(v, 1), lg_b, False), 1)
    o = fwd + bwd
    mu = jnp.mean(o, axis=-1, keepdims=True)
    var = jnp.mean(jnp.square(o - mu), axis=-1, keepdims=True)
    o = ((o - mu) * lax.rsqrt(var + EPS)).reshape(B, S, RET_HEADS * RET_HEAD_DIM)
    return (jax.nn.silu(r_g.astype(jnp.float32)) * o).astype(r_q.dtype)


def setup_inputs(seed: int = 0) -> dict:
    key = jax.random.key(seed)
    ks = jax.random.split(key, 24)
    f32 = jnp.float32

    def w(k, shape, fan_in):
        return jax.random.normal(k, shape, f32) * (fan_in ** -0.5)

    def gain(k, shape):
        return 1.0 + 0.02 * jax.random.normal(k, shape, f32)

    L = DEPTH
    offset = jax.random.randint(ks[1], (BATCH, 1), 0, 4096, dtype=jnp.int32)
    positions = offset + jnp.arange(SEQ, dtype=jnp.int32)[None, :]
    decay_init = jnp.asarray(np.log(2.0 ** (5.0 + np.arange(RET_HEADS)) - 1.0), f32)
    return {
        "x": jax.random.normal(ks[0], (BATCH, SEQ, D_MODEL), f32),
        "positions": positions,
        "ffn1_norm": gain(ks[2], (L, D_MODEL)),
        "ffn1_w_gate": w(ks[3], (L, D_MODEL, D_FF), D_MODEL),
        "ffn1_w_up": w(ks[4], (L, D_MODEL, D_FF), D_MODEL),
        "ffn1_w_down": w(ks[5], (L, D_FF, D_MODEL), D_FF),
        "mix_norm": gain(ks[6], (L, D_MODEL)),
        "w_in": w(ks[7], (L, D_MODEL, D_IN), D_MODEL),
        "q_norm": gain(ks[8], (L, Q_LORA_RANK)),
        "w_uq": w(ks[9], (L, Q_LORA_RANK, MLA_HEADS * (QK_NOPE_DIM + QK_ROPE_DIM)), Q_LORA_RANK),
        "kv_norm": gain(ks[10], (L, KV_LORA_RANK)),
        "w_ukv": w(ks[11], (L, KV_LORA_RANK, MLA_HEADS * (QK_NOPE_DIM + V_HEAD_DIM)), KV_LORA_RANK),
        "ret_decay_fwd": decay_init[None, :] + 0.05 * jax.random.normal(ks[12], (L, RET_HEADS), f32),
        "ret_decay_bwd": decay_init[None, :] + 0.05 * jax.random.normal(ks[13], (L, RET_HEADS), f32),
        "w_o": w(ks[14], (L, D_MIX, D_MODEL), D_MIX),
        "ffn2_norm": gain(ks[15], (L, D_MODEL)),
        "ffn2_w_gate": w(ks[16], (L, D_MODEL, D_FF), D_MODEL),
        "ffn2_w_up": w(ks[17], (L, D_MODEL, D_FF), D_MODEL),
        "ffn2_w_down": w(ks[18], (L, D_FF, D_MODEL), D_FF),
        "final_norm": gain(ks[19], (D_MODEL,)),
    }


def reference(x, positions, ffn1_norm, ffn1_w_gate, ffn1_w_up, ffn1_w_down, mix_norm, w_in,
              q_norm, w_uq, kv_norm, w_ukv, ret_decay_fwd, ret_decay_bwd, w_o,
              ffn2_norm, ffn2_w_gate, ffn2_w_up, ffn2_w_down, final_norm):
    pos_f = positions.astype(jnp.float32)
    h = x
    for l in range(DEPTH):
        h = h + 0.5 * swiglu(rms_norm(h, ffn1_norm[l]), ffn1_w_gate[l], ffn1_w_up[l], ffn1_w_down[l])
        n = rms_norm(h, mix_norm[l])
        proj = n @ w_in[l]
        c_q, c_kv, k_pe, r_q, r_k, r_v, r_g = jnp.split(proj, SPLIT_POINTS, axis=-1)
        a = mla_group(c_q, c_kv, k_pe, pos_f, q_norm[l], w_uq[l], kv_norm[l], w_ukv[l])
        r = retention_group(r_q, r_k, r_v, r_g, pos_f, ret_decay_fwd[l], ret_decay_bwd[l])
        h = h + jnp.concatenate([a, r], axis=-1) @ w_o[l]
        h = h + 0.5 * swiglu(rms_norm(h, ffn2_norm[l]), ffn2_w_gate[l], ffn2_w_up[l], ffn2_w_down[l])
    return rms_norm(h, final_norm)
```

```python
import functools
import math

import numpy as np
import jax
import jax.numpy as jnp
from jax import lax
from jax.experimental import pallas as pl
from jax.experimental.pallas import tpu as pltpu

F32 = jnp.float32
BF16 = jnp.bfloat16

D_MODEL = 1024
D_FF = 2816
MLA_HEADS = 8
Q_LORA = 256
KV_LORA = 128
NOPE = 64
ROPE = 32
VDIM = 64
RET_HEADS = 8
RET_DIM = 64
D_RET = RET_HEADS * RET_DIM
ROPE_THETA = 10000.0
EPS = 1e-6
MLA_SCALE = (NOPE + ROPE) ** -0.5
LOG2E = math.log2(math.e)

LANES = 128
VMEM_LIMIT = 56 * 1024 * 1024

_OFF_CQ = 0
_OFF_CKV = 256
_OFF_KPE = 384
_OFF_KPE_SW = 512
_OFF_RQ = 640
_OFF_RQ_SW = 1152
_OFF_RK = 1664
_OFF_RK_SW = 2176
_OFF_RV = 2688
_OFF_RG = 3200
D_IN_AUG = 3712


def _params(sem):
    return pltpu.CompilerParams(dimension_semantics=sem, vmem_limit_bytes=VMEM_LIMIT)


def _const_spec(shape):
    nd = len(shape)
    return pl.BlockSpec(shape, lambda *_: (0,) * nd, pipeline_mode=pl.Buffered(1))


def _rope_table_kernel(pos_ref, inv_ref, sgn_ref, one_ref, cos_ref, sin_ref):
    pos = pos_ref[...]
    for t in range(2):
        ang = pos * inv_ref[t]
        cos_ref[t] = jnp.cos(ang) * jnp.abs(sgn_ref[t]) + one_ref[t]
        sin_ref[t] = jnp.sin(ang) * sgn_ref[t]


def _rope_constants():
    lane = np.arange(LANES)
    inv_r = ROPE_THETA ** (-(lane % 32).astype(np.float64) / 32.0)
    sgn_r = np.where(lane % 64 < 32, -1.0, 1.0)
    one_r = np.zeros(LANES)
    in_pe = (lane >= 64) & (lane < 96)
    inv_m = np.where(in_pe, ROPE_THETA ** (-((lane - 64) % 16).astype(np.float64) / 16.0), 0.0)
    sgn_m = np.where(in_pe, np.where(lane < 80, -1.0, 1.0), 0.0)
    one_m = np.where(lane < 64, 1.0, 0.0)
    mk = lambda a, b: jnp.asarray(np.stack([a, b])[:, None, :], F32)
    return mk(inv_r, inv_m), mk(sgn_r, sgn_m), mk(one_r, one_m)


def _rope_tables(pos_b, tm=2048):
    n = pos_b.shape[0]
    inv, sgn, one = _rope_constants()
    cspec = pl.BlockSpec((2, 1, LANES), lambda i: (0, 0, 0))
    return pl.pallas_call(
        _rope_table_kernel,
        grid=(n // tm,),
        in_specs=[pl.BlockSpec((tm, LANES), lambda i: (i, 0)), cspec, cspec, cspec],
        out_specs=[pl.BlockSpec((2, tm, LANES), lambda i: (0, i, 0))] * 2,
        out_shape=[jax.ShapeDtypeStruct((2, n, LANES), F32)] * 2,
        compiler_params=_params(("parallel",)),
        name="rope_tables",
    )(pos_b, inv, sgn, one)


def _rms(x, g):
    return x * lax.rsqrt(jnp.mean(x * x, axis=-1, keepdims=True) + EPS) * g


def _swiglu_acc(n_bf, wg_ref, wu_ref, wd_ref, ck):
    acc = None
    for j in range(D_FF // ck):
        sl = slice(j * ck, (j + 1) * ck)
        g = jnp.dot(n_bf, wg_ref[:, sl], preferred_element_type=F32)
        u = jnp.dot(n_bf, wu_ref[:, sl], preferred_element_type=F32)
        hm = (g * jax.nn.sigmoid(g) * u).astype(BF16)
        d = jnp.dot(hm, wd_ref[sl, :], preferred_element_type=F32)
        acc = d if acc is None else acc + d
    return acc


def _ffn1_kernel(x_ref, g_ref, wg_ref, wu_ref, wd_ref, o_ref, *, ck):
    x = x_ref[...]
    n_bf = _rms(x, g_ref[...]).astype(BF16)
    o_ref[...] = x + 0.5 * _swiglu_acc(n_bf, wg_ref, wu_ref, wd_ref, ck)


def _ffn1(x, g, wg, wu, wd, tm=512, ck=256):
    n = x.shape[0]
    row = pl.BlockSpec((tm, D_MODEL), lambda i: (i, 0))
    return pl.pallas_call(
        functools.partial(_ffn1_kernel, ck=ck),
        grid=(n // tm,),
        in_specs=[row, _const_spec((1, D_MODEL)), _const_spec((D_MODEL, D_FF)),
                  _const_spec((D_MODEL, D_FF)), _const_spec((D_FF, D_MODEL))],
        out_specs=row,
        out_shape=jax.ShapeDtypeStruct((n, D_MODEL), F32),
        compiler_params=_params(("parallel",)),
        name="ffn1",
    )(x, g, wg, wu, wd)


def _out_ffn2_kernel(h_ref, a_ref, r_ref, woa_ref, wor_ref, g_ref, wg_ref, wu_ref, wd_ref,
                     gf_ref, o_ref, *, ck):
    h = (h_ref[...]
         + jnp.dot(a_ref[...], woa_ref[...], preferred_element_type=F32)
         + jnp.dot(r_ref[...], wor_ref[...], preferred_element_type=F32))
    n_bf = _rms(h, g_ref[...]).astype(BF16)
    h = h + 0.5 * _swiglu_acc(n_bf, wg_ref, wu_ref, wd_ref, ck)
    o_ref[...] = _rms(h, gf_ref[...])


def _out_ffn2(h, a, r, woa, wor, g, wg, wu, wd, gf, tm=512, ck=256):
    n = h.shape[0]
    row = pl.BlockSpec((tm, D_MODEL), lambda i: (i, 0))
    half = pl.BlockSpec((tm, D_RET), lambda i: (i, 0))
    return pl.pallas_call(
        functools.partial(_out_ffn2_kernel, ck=ck),
        grid=(n // tm,),
        in_specs=[row, half, half, _const_spec((D_RET, D_MODEL)), _const_spec((D_RET, D_MODEL)),
                  _const_spec((1, D_MODEL)), _const_spec((D_MODEL, D_FF)),
                  _const_spec((D_MODEL, D_FF)), _const_spec((D_FF, D_MODEL)),
                  _const_spec((1, D_MODEL))],
        out_specs=row,
        out_shape=jax.ShapeDtypeStruct((n, D_MODEL), F32),
        compiler_params=_params(("parallel",)),
        name="out_ffn2",
    )(h, a, r, woa, wor, g, wg, wu, wd, gf)


def _proj_kernel(h_ref, cos_ref, sin_ref, gm_ref, win_ref, gq_ref, wq_ref, wqs_ref, gkv_ref,
                 wkv_ref, qm_ref, km_ref, vm_ref, rq_ref, rk_ref, rv_ref, rg_ref):
    n_bf = _rms(h_ref[...], gm_ref[...]).astype(BF16)
    cr, sr = cos_ref[0], sin_ref[0]
    cm, sm = cos_ref[1], sin_ref[1]

    def proj(off, width):
        return jnp.dot(n_bf, win_ref[:, off:off + width], preferred_element_type=F32)

    for off, off_sw, out_ref, scale in ((_OFF_RQ, _OFF_RQ_SW, rq_ref, 1.0),
                                        (_OFF_RK, _OFF_RK_SW, rk_ref, RET_DIM ** -0.5)):
        for b in range(D_RET // LANES):
            x = proj(off + b * LANES, LANES)
            xs = proj(off_sw + b * LANES, LANES)
            out_ref[:, b * LANES:(b + 1) * LANES] = ((x * cr + xs * sr) * scale).astype(BF16)
    rv_ref[...] = proj(_OFF_RV, D_RET).astype(BF16)
    g = proj(_OFF_RG, D_RET)
    rg_ref[...] = g * jax.nn.sigmoid(g)

    cq_bf = _rms(proj(_OFF_CQ, Q_LORA), gq_ref[...]).astype(BF16)
    ckv_bf = _rms(proj(_OFF_CKV, KV_LORA), gkv_ref[...]).astype(BF16)
    kpe = proj(_OFF_KPE, LANES) * cm + proj(_OFF_KPE_SW, LANES) * sm
    for h in range(MLA_HEADS):
        sl = slice(h * LANES, (h + 1) * LANES)
        qa = jnp.dot(cq_bf, wq_ref[:, sl], preferred_element_type=F32)
        qb = jnp.dot(cq_bf, wqs_ref[:, sl], preferred_element_type=F32)
        qm_ref[:, sl] = ((qa * cm + qb * sm) * (MLA_SCALE * LOG2E)).astype(BF16)
        kn = jnp.dot(ckv_bf, wkv_ref[:, sl], preferred_element_type=F32)
        km_ref[:, sl] = (kn + kpe).astype(BF16)
    vm_ref[...] = jnp.dot(ckv_bf, wkv_ref[:, MLA_HEADS * LANES:],
                          preferred_element_type=F32).astype(BF16)


def _proj(h, cos_t, sin_t, gm, win, gq, wq, wqs, gkv, wkv, tm=512):
    n = h.shape[0]
    row = lambda w: pl.BlockSpec((tm, w), lambda i: (i, 0))
    tab = pl.BlockSpec((2, tm, LANES), lambda i: (0, i, 0))
    sds = lambda w, dt: jax.ShapeDtypeStruct((n, w), dt)
    return pl.pallas_call(
        _proj_kernel,
        grid=(n // tm,),
        in_specs=[row(D_MODEL), tab, tab, _const_spec((1, D_MODEL)),
                  _const_spec((D_MODEL, D_IN_AUG)), _const_spec((1, Q_LORA)),
                  _const_spec((Q_LORA, MLA_HEADS * LANES)), _const_spec((Q_LORA, MLA_HEADS * LANES)),
                  _const_spec((1, KV_LORA)), _const_spec((KV_LORA, MLA_HEADS * LANES + D_RET))],
        out_specs=[row(1024), row(1024), row(512), row(512), row(512), row(512), row(512)],
        out_shape=[sds(1024, BF16), sds(1024, BF16), sds(512, BF16), sds(512, BF16),
                   sds(512, BF16), sds(512, BF16), sds(512, F32)],
        compiler_params=_params(("parallel",)),
        name="mixer_proj",
    )(h, cos_t, sin_t, gm, win, gq, wq, wqs, gkv, wkv)


def _mla_kernel(q_ref, k_ref, v_ref, o_ref, *, tq, tk, seq):
    outs = []
    for h in range(2):
        hs = slice(h * LANES, (h + 1) * LANES)
        q = q_ref[0, :, hs]

        def body(c, carry, hs=hs, q=q):
            m, l, acc = carry
            off = pl.multiple_of(c * tk, tk)
            k = k_ref[0, pl.ds(off, tk), hs]
            v = v_ref[0, pl.ds(off, tk), :]
            s = lax.dot_general(q, k, (((1,), (1,)), ((), ())), preferred_element_type=F32)
            m_new = jnp.maximum(m, jnp.max(s, axis=-1, keepdims=True))
            alpha = jnp.exp2(m - m_new)
            p = jnp.exp2(s - m_new)
            l = alpha * l + jnp.sum(p, axis=-1, keepdims=True)
            acc = alpha * acc + jnp.dot(p.astype(BF16), v, preferred_element_type=F32)
            return m_new, l, acc

        init = (jnp.full((tq, 1), -jnp.inf, F32), jnp.zeros((tq, 1), F32),
                jnp.zeros((tq, LANES), F32))
        _, l, acc = lax.fori_loop(0, seq // tk, body, init)
        outs.append(acc / l)
    lane = lax.broadcasted_iota(jnp.int32, (tq, LANES), 1)
    o_ref[0] = jnp.where(lane < VDIM, outs[0], outs[1]).astype(BF16)


def _mla_attention(q, k, v, tq=512, tk=512):
    b, s, _ = q.shape
    npair = MLA_HEADS // 2
    return pl.pallas_call(
        functools.partial(_mla_kernel, tq=tq, tk=tk, seq=s),
        grid=(b, npair, s // tq),
        in_specs=[pl.BlockSpec((1, tq, 2 * LANES), lambda bi, p, i: (bi, i, p)),
                  pl.BlockSpec((1, s, 2 * LANES), lambda bi, p, i: (bi, 0, p)),
                  pl.BlockSpec((1, s, LANES), lambda bi, p, i: (bi, 0, p))],
        out_specs=pl.BlockSpec((1, tq, LANES), lambda bi, p, i: (bi, i, p)),
        out_shape=jax.ShapeDtypeStruct((b, s, MLA_HEADS * VDIM), BF16),
        compiler_params=_params(("parallel", "parallel", "parallel")),
        name="mla_attention",
    )(q, k, v)


def _log_sigmoid(x):
    return jnp.minimum(x, 0.0) - jnp.log1p(jnp.exp(-jnp.abs(x)))


def _retention_kernel(q_ref, k_ref, v_ref, g_ref, df_ref, db_ref, o_ref,
                      acc_ref, state_ref, dmat_ref, *, ch, seq):
    nc = seq // ch
    lane = lax.broadcasted_iota(jnp.int32, (ch, LANES), 1)
    row = lax.broadcasted_iota(jnp.int32, (ch, LANES), 0).astype(F32)
    head0 = lane < RET_DIM
    ri = lax.broadcasted_iota(jnp.int32, (ch, ch), 0)
    ci = lax.broadcasted_iota(jnp.int32, (ch, ch), 1)
    sr = lax.broadcasted_iota(jnp.int32, (LANES, LANES), 0)
    sc = lax.broadcasted_iota(jnp.int32, (LANES, LANES), 1)
    same_head = (sr < RET_DIM) == (sc < RET_DIM)

    def direction(lg, fwd):
        if fwd:
            q_dec = jnp.exp(lg * (row + 1.0))
            k_dec = jnp.exp(lg * (ch - 1.0 - row))
            rel = (ri - ci).astype(F32)
            keep = ri >= ci
        else:
            q_dec = jnp.exp(lg * (ch - row))
            k_dec = jnp.exp(lg * row)
            rel = (ci - ri).astype(F32)
            keep = ci > ri
        c_dec = jnp.exp(lg * float(ch))
        rel = jnp.maximum(rel, 0.0)
        for h in range(2):
            lg_h = lg[:, h * RET_DIM:h * RET_DIM + 1]
            dmat_ref[h] = jnp.where(keep, jnp.exp(lg_h * rel), 0.0)
        state_ref[...] = jnp.zeros_like(state_ref)

        def body(t, carry):
            c = t if fwd else nc - 1 - t
            off = pl.multiple_of(c * ch, ch)
            qc = q_ref[0, pl.ds(off, ch), :]
            kc = k_ref[0, pl.ds(off, ch), :]
            vc = v_ref[0, pl.ds(off, ch), :]
            zero = jnp.zeros_like(qc)
            inner = []
            for h in range(2):
                qh = jnp.where(head0, qc, zero) if h == 0 else jnp.where(head0, zero, qc)
                s = lax.dot_general(qh, kc, (((1,), (1,)), ((), ())), preferred_element_type=F32)
                s = (s * dmat_ref[h]).astype(BF16)
                inner.append(jnp.dot(s, vc, preferred_element_type=F32))
            state = state_ref[...]
            cross = jnp.dot(qc, state.astype(BF16), preferred_element_type=F32) * q_dec
            out = jnp.where(head0, inner[0], inner[1]) + cross
            if fwd:
                acc_ref[pl.ds(off, ch), :] = out
            else:
                acc_ref[pl.ds(off, ch), :] += out
            kd = (kc.astype(F32) * k_dec).astype(BF16)
            upd = lax.dot_general(kd, vc, (((0,), (0,)), ((), ())), preferred_element_type=F32)
            state_ref[...] = state * c_dec + jnp.where(same_head, upd, 0.0)
            return carry

        lax.fori_loop(0, nc, body, 0)

    direction(_log_sigmoid(df_ref[...]), True)
    direction(_log_sigmoid(db_ref[...]), False)

    def finish(t, carry):
        off = pl.multiple_of(t * ch, ch)
        o = acc_ref[pl.ds(off, ch), :]

        def head_mean(x):
            m0 = jnp.sum(jnp.where(head0, x, 0.0), axis=-1, keepdims=True)
            m1 = jnp.sum(jnp.where(head0, 0.0, x), axis=-1, keepdims=True)
            return jnp.where(head0, m0, m1) * (1.0 / RET_DIM)

        d = o - head_mean(o)
        var = head_mean(d * d)
        o_ref[0, pl.ds(off, ch), :] = (d * lax.rsqrt(var + EPS)
                                       * g_ref[0, pl.ds(off, ch), :]).astype(BF16)
        return carry

    lax.fori_loop(0, nc, finish, 0)


def _retention(q, k, v, g, dec_f, dec_b, ch=256):
    b, s, _ = q.shape
    npair = RET_HEADS // 2
    blk = pl.BlockSpec((1, s, LANES), lambda bi, p: (bi, 0, p))
    dec = pl.BlockSpec((1, LANES), lambda bi, p: (0, p))
    return pl.pallas_call(
        functools.partial(_retention_kernel, ch=ch, seq=s),
        grid=(b, npair),
        in_specs=[blk, blk, blk, blk, dec, dec],
        out_specs=blk,
        out_shape=jax.ShapeDtypeStruct((b, s, D_RET), BF16),
        scratch_shapes=[pltpu.VMEM((s, LANES), F32), pltpu.VMEM((LANES, LANES), F32),
                        pltpu.VMEM((2, ch, ch), F32)],
        compiler_params=_params(("parallel", "parallel")),
        name="retention",
    )(q, k, v, g, dec_f, dec_b)


def _swap_halves(w, heads, dim):
    k = w.shape[0]
    w4 = w.reshape(k, heads, 2, dim // 2)
    return w4[:, :, ::-1, :].reshape(k, heads * dim)


def _prep_w_in(w_in):
    c_q, c_kv, k_pe, r_q, r_k, r_v, r_g = jnp.split(
        w_in, np.cumsum([Q_LORA, KV_LORA, ROPE, D_RET, D_RET, D_RET])[:], axis=1)
    k = w_in.shape[0]
    z64 = jnp.zeros((k, 64), w_in.dtype)
    z32 = jnp.zeros((k, 32), w_in.dtype)
    kpe = jnp.concatenate([z64, k_pe, z32], axis=1)
    kpe_sw = jnp.concatenate([z64, k_pe[:, 16:], k_pe[:, :16], z32], axis=1)
    return jnp.concatenate(
        [c_q, c_kv, kpe, kpe_sw, r_q, _swap_halves(r_q, RET_HEADS, RET_DIM),
         r_k, _swap_halves(r_k, RET_HEADS, RET_DIM), r_v, r_g], axis=1).astype(BF16)


def _prep_w_uq(w_uq):
    k = w_uq.shape[0]
    w3 = w_uq.reshape(k, MLA_HEADS, NOPE + ROPE)
    nope, x1, x2 = w3[:, :, :NOPE], w3[:, :, NOPE:NOPE + 16], w3[:, :, NOPE + 16:]
    z32 = jnp.zeros((k, MLA_HEADS, 32), w_uq.dtype)
    wq = jnp.concatenate([nope, x1, x2, z32], axis=2).reshape(k, MLA_HEADS * LANES)
    wqs = jnp.concatenate([jnp.zeros_like(nope), x2, x1, z32], axis=2).reshape(k, MLA_HEADS * LANES)
    return wq.astype(BF16), wqs.astype(BF16)


def _prep_w_ukv(w_ukv):
    k = w_ukv.shape[0]
    w3 = w_ukv.reshape(k, MLA_HEADS, NOPE + VDIM)
    kn = jnp.concatenate([w3[:, :, :NOPE], jnp.zeros((k, MLA_HEADS, 64), w_ukv.dtype)], axis=2)
    v = w3[:, :, NOPE:]
    return jnp.concatenate([kn.reshape(k, MLA_HEADS * LANES), v.reshape(k, MLA_HEADS * VDIM)],
                           axis=1).astype(BF16)


def _lane_expand(dec):
    return jnp.repeat(dec.astype(F32), RET_DIM)[None, :]


def kernel(x, positions, ffn1_norm, ffn1_w_gate, ffn1_w_up, ffn1_w_down, mix_norm, w_in, q_norm, w_uq, kv_norm, w_ukv, ret_decay_fwd, ret_decay_bwd, w_o, ffn2_norm, ffn2_w_gate, ffn2_w_up, ffn2_w_down, final_norm):
    b, s, d = x.shape
    n = b * s
    pos_b = jnp.broadcast_to(positions.astype(F32).reshape(n, 1), (n, LANES))
    cos_t, sin_t = _rope_tables(pos_b)

    h = x.reshape(n, d)
    for l in range(ffn1_norm.shape[0]):
        h = _ffn1(h, ffn1_norm[l][None, :], ffn1_w_gate[l].astype(BF16),
                  ffn1_w_up[l].astype(BF16), ffn1_w_down[l].astype(BF16))
        wq, wqs = _prep_w_uq(w_uq[l])
        qm, km, vm, rq, rk, rv, rg = _proj(
            h, cos_t, sin_t, mix_norm[l][None, :], _prep_w_in(w_in[l]), q_norm[l][None, :],
            wq, wqs, kv_norm[l][None, :], _prep_w_ukv(w_ukv[l]))
        a = _mla_attention(qm.reshape(b, s, -1), km.reshape(b, s, -1), vm.reshape(b, s, -1))
        r = _retention(rq.reshape(b, s, -1), rk.reshape(b, s, -1), rv.reshape(b, s, -1),
                       rg.reshape(b, s, -1), _lane_expand(ret_decay_fwd[l]),
                       _lane_expand(ret_decay_bwd[l]))
        wo = w_o[l].astype(BF16)
        last = l == ffn1_norm.shape[0] - 1
        h = _out_ffn2(h, a.reshape(n, -1), r.reshape(n, -1), wo[:MLA_HEADS * VDIM],
                      wo[MLA_HEADS * VDIM:], ffn2_norm[l][None, :], ffn2_w_gate[l].astype(BF16),
                      ffn2_w_up[l].astype(BF16), ffn2_w_down[l].astype(BF16),
                      final_norm[None, :])
        assert last, "kernel is specialised to DEPTH == 1"
    return h.reshape(b, s, d)
```

```python
import functools
import math

import numpy as np
import jax
import jax.numpy as jnp
from jax import lax
from jax.experimental import pallas as pl
from jax.experimental.pallas import tpu as pltpu

F32 = jnp.float32
BF16 = jnp.bfloat16

D_MODEL = 1024
D_FF = 2816
MLA_HEADS = 8
Q_LORA = 256
KV_LORA = 128
NOPE = 64
ROPE = 32
VDIM = 64
RET_HEADS = 8
RET_DIM = 64
D_RET = RET_HEADS * RET_DIM
ROPE_THETA = 10000.0
EPS = 1e-6
MLA_SCALE = (NOPE + ROPE) ** -0.5
LOG2E = math.log2(math.e)

LANES = 128
VMEM_LIMIT = 56 * 1024 * 1024

_OFF_CQ = 0
_OFF_CKV = 256
_OFF_KPE = 384
_OFF_KPE_SW = 512
_OFF_RQ = 640
_OFF_RQ_SW = 1152
_OFF_RK = 1664
_OFF_RK_SW = 2176
_OFF_RV = 2688
_OFF_RG = 3200
D_IN_AUG = 3712


def _params(sem):
    return pltpu.CompilerParams(dimension_semantics=sem, vmem_limit_bytes=VMEM_LIMIT)


def _const_spec(shape):
    nd = len(shape)
    return pl.BlockSpec(shape, lambda *_: (0,) * nd, pipeline_mode=pl.Buffered(1))


def _rope_table_kernel(pos_ref, inv_ref, sgn_ref, one_ref, cos_ref, sin_ref):
    pos = pos_ref[...]
    for t in range(2):
        ang = pos * inv_ref[t]
        cos_ref[t] = jnp.cos(ang) * jnp.abs(sgn_ref[t]) + one_ref[t]
        sin_ref[t] = jnp.sin(ang) * sgn_ref[t]


def _rope_constants():
    lane = np.arange(LANES)
    inv_r = ROPE_THETA ** (-(lane % 32).astype(np.float64) / 32.0)
    sgn_r = np.where(lane % 64 < 32, -1.0, 1.0)
    one_r = np.zeros(LANES)
    in_pe = (lane >= 64) & (lane < 96)
    inv_m = np.where(in_pe, ROPE_THETA ** (-((lane - 64) % 16).astype(np.float64) / 16.0), 0.0)
    sgn_m = np.where(in_pe, np.where(lane < 80, -1.0, 1.0), 0.0)
    one_m = np.where(lane < 64, 1.0, 0.0)
    mk = lambda a, b: jnp.asarray(np.stack([a, b])[:, None, :], F32)
    return mk(inv_r, inv_m), mk(sgn_r, sgn_m), mk(one_r, one_m)


def _rope_tables(pos_b, tm=2048):
    n = pos_b.shape[0]
    inv, sgn, one = _rope_constants()
    cspec = pl.BlockSpec((2, 1, LANES), lambda i: (0, 0, 0))
    return pl.pallas_call(
        _rope_table_kernel,
        grid=(n // tm,),
        in_specs=[pl.BlockSpec((tm, LANES), lambda i: (i, 0)), cspec, cspec, cspec],
        out_specs=[pl.BlockSpec((2, tm, LANES), lambda i: (0, i, 0))] * 2,
        out_shape=[jax.ShapeDtypeStruct((2, n, LANES), F32)] * 2,
        compiler_params=_params(("parallel",)),
        name="rope_tables",
    )(pos_b, inv, sgn, one)


def _rms(x, g):
    return x * lax.rsqrt(jnp.mean(x * x, axis=-1, keepdims=True) + EPS) * g


def _swiglu_acc(n_bf, wg_ref, wu_ref, wd_ref, ck):
    acc = None
    for j in range(D_FF // ck):
        sl = slice(j * ck, (j + 1) * ck)
        g = jnp.dot(n_bf, wg_ref[:, sl], preferred_element_type=F32)
        u = jnp.dot(n_bf, wu_ref[:, sl], preferred_element_type=F32)
        hm = (g * jax.nn.sigmoid(g) * u).astype(BF16)
        d = jnp.dot(hm, wd_ref[sl, :], preferred_element_type=F32)
        acc = d if acc is None else acc + d
    return acc


def _ffn1_kernel(x_ref, g_ref, wg_ref, wu_ref, wd_ref, o_ref, *, ck):
    x = x_ref[...]
    n_bf = _rms(x, g_ref[...]).astype(BF16)
    o_ref[...] = x + 0.5 * _swiglu_acc(n_bf, wg_ref, wu_ref, wd_ref, ck)


def _ffn1(x, g, wg, wu, wd, tm=512, ck=256):
    n = x.shape[0]
    row = pl.BlockSpec((tm, D_MODEL), lambda i: (i, 0))
    return pl.pallas_call(
        functools.partial(_ffn1_kernel, ck=ck),
        grid=(n // tm,),
        in_specs=[row, _const_spec((1, D_MODEL)), _const_spec((D_MODEL, D_FF)),
                  _const_spec((D_MODEL, D_FF)), _const_spec((D_FF, D_MODEL))],
        out_specs=row,
        out_shape=jax.ShapeDtypeStruct((n, D_MODEL), F32),
        compiler_params=_params(("parallel",)),
        name="ffn1",
    )(x, g, wg, wu, wd)


def _out_ffn2_kernel(h_ref, a_ref, r_ref, woa_ref, wor_ref, g_ref, wg_ref, wu_ref, wd_ref,
                     gf_ref, o_ref, *, ck):
    h = (h_ref[...]
         + jnp.dot(a_ref[...], woa_ref[...], preferred_element_type=F32)
         + jnp.dot(r_ref[...], wor_ref[...], preferred_element_type=F32))
    n_bf = _rms(h, g_ref[...]).astype(BF16)
    h = h + 0.5 * _swiglu_acc(n_bf, wg_ref, wu_ref, wd_ref, ck)
    o_ref[...] = _rms(h, gf_ref[...])


def _out_ffn2(h, a, r, woa, wor, g, wg, wu, wd, gf, tm=512, ck=256):
    n = h.shape[0]
    row = pl.BlockSpec((tm, D_MODEL), lambda i: (i, 0))
    half = pl.BlockSpec((tm, D_RET), lambda i: (i, 0))
    return pl.pallas_call(
        functools.partial(_out_ffn2_kernel, ck=ck),
        grid=(n // tm,),
        in_specs=[row, half, half, _const_spec((D_RET, D_MODEL)), _const_spec((D_RET, D_MODEL)),
                  _const_spec((1, D_MODEL)), _const_spec((D_MODEL, D_FF)),
                  _const_spec((D_MODEL, D_FF)), _const_spec((D_FF, D_MODEL)),
                  _const_spec((1, D_MODEL))],
        out_specs=row,
        out_shape=jax.ShapeDtypeStruct((n, D_MODEL), F32),
        compiler_params=_params(("parallel",)),
        name="out_ffn2",
    )(h, a, r, woa, wor, g, wg, wu, wd, gf)


def _proj_kernel(h_ref, cos_ref, sin_ref, gm_ref, win_ref, gq_ref, wq_ref, wqs_ref, gkv_ref,
                 wkv_ref, qm_ref, km_ref, vm_ref, rq_ref, rk_ref, rv_ref, rg_ref):
    n_bf = _rms(h_ref[...], gm_ref[...]).astype(BF16)
    cr, sr = cos_ref[0], sin_ref[0]
    cm, sm = cos_ref[1], sin_ref[1]

    def proj(off, width):
        return jnp.dot(n_bf, win_ref[:, off:off + width], preferred_element_type=F32)

    for off, off_sw, out_ref, scale in ((_OFF_RQ, _OFF_RQ_SW, rq_ref, 1.0),
                                        (_OFF_RK, _OFF_RK_SW, rk_ref, RET_DIM ** -0.5)):
        for b in range(D_RET // LANES):
            x = proj(off + b * LANES, LANES)
            xs = proj(off_sw + b * LANES, LANES)
            out_ref[:, b * LANES:(b + 1) * LANES] = ((x * cr + xs * sr) * scale).astype(BF16)
    rv_ref[...] = proj(_OFF_RV, D_RET).astype(BF16)
    g = proj(_OFF_RG, D_RET)
    rg_ref[...] = g * jax.nn.sigmoid(g)

    cq_bf = _rms(proj(_OFF_CQ, Q_LORA), gq_ref[...]).astype(BF16)
    ckv_bf = _rms(proj(_OFF_CKV, KV_LORA), gkv_ref[...]).astype(BF16)
    kpe = proj(_OFF_KPE, LANES) * cm + proj(_OFF_KPE_SW, LANES) * sm
    for h in range(MLA_HEADS):
        sl = slice(h * LANES, (h + 1) * LANES)
        qa = jnp.dot(cq_bf, wq_ref[:, sl], preferred_element_type=F32)
        qb = jnp.dot(cq_bf, wqs_ref[:, sl], preferred_element_type=F32)
        qm_ref[:, sl] = ((qa * cm + qb * sm) * (MLA_SCALE * LOG2E)).astype(BF16)
        kn = jnp.dot(ckv_bf, wkv_ref[:, sl], preferred_element_type=F32)
        km_ref[:, sl] = (kn + kpe).astype(BF16)
    lane = lax.broadcasted_iota(jnp.int32, (1, MLA_HEADS * LANES), 1)
    ones_lane = jnp.where(lane % LANES == VDIM, 1.0, 0.0)
    vm_ref[...] = (jnp.dot(ckv_bf, wkv_ref[:, MLA_HEADS * LANES:], preferred_element_type=F32)
                   + ones_lane).astype(BF16)


def _proj(h, cos_t, sin_t, gm, win, gq, wq, wqs, gkv, wkv, tm=512):
    n = h.shape[0]
    row = lambda w: pl.BlockSpec((tm, w), lambda i: (i, 0))
    tab = pl.BlockSpec((2, tm, LANES), lambda i: (0, i, 0))
    sds = lambda w, dt: jax.ShapeDtypeStruct((n, w), dt)
    return pl.pallas_call(
        _proj_kernel,
        grid=(n // tm,),
        in_specs=[row(D_MODEL), tab, tab, _const_spec((1, D_MODEL)),
                  _const_spec((D_MODEL, D_IN_AUG)), _const_spec((1, Q_LORA)),
                  _const_spec((Q_LORA, MLA_HEADS * LANES)), _const_spec((Q_LORA, MLA_HEADS * LANES)),
                  _const_spec((1, KV_LORA)), _const_spec((KV_LORA, 2 * MLA_HEADS * LANES))],
        out_specs=[row(1024), row(1024), row(1024), row(512), row(512), row(512), row(512)],
        out_shape=[sds(1024, BF16), sds(1024, BF16), sds(1024, BF16), sds(512, BF16),
                   sds(512, BF16), sds(512, BF16), sds(512, F32)],
        compiler_params=_params(("parallel",)),
        name="mixer_proj",
    )(h, cos_t, sin_t, gm, win, gq, wq, wqs, gkv, wkv)


def _mla_kernel(q_ref, k_ref, v_ref, o_ref, m_ref, acc_ref, s0_ref, s1_ref, mb0_ref, mb1_ref,
                al0_ref, al1_ref, *, tq, tk, seq):
    nk = seq // tk
    nrep = tk // LANES
    slots = ((s0_ref, mb0_ref, al0_ref), (s1_ref, mb1_ref, al1_ref))
    m_ref[...] = jnp.full(m_ref.shape, -jnp.inf, F32)
    acc_ref[...] = jnp.zeros(acc_ref.shape, F32)

    def scores(c, slot):
        s_ref, mb_ref, al_ref = slots[slot]
        off = c * tk if isinstance(c, int) else pl.multiple_of(c * tk, tk)
        for h in range(2):
            q = q_ref[0, :, h * LANES:(h + 1) * LANES]
            k = k_ref[0, pl.ds(off, tk), h * LANES:(h + 1) * LANES]
            s = lax.dot_general(q, k, (((1,), (1,)), ((), ())), preferred_element_type=F32)
            s_ref[h] = s
            m_old = m_ref[h]
            m_new = jnp.maximum(m_old, jnp.max(s, axis=-1, keepdims=True))
            m_ref[h] = m_new
            mb_ref[h] = m_new
            al_ref[h] = jnp.exp2(m_old - m_new)

    def weighted_values(c, slot):
        s_ref, mb_ref, al_ref = slots[slot]
        off = c * tk if isinstance(c, int) else pl.multiple_of(c * tk, tk)
        for h in range(2):
            p = jnp.exp2(s_ref[h] - pltpu.repeat(mb_ref[h], nrep, axis=1))
            v = v_ref[0, pl.ds(off, tk), h * LANES:(h + 1) * LANES]
            acc_ref[h] = al_ref[h] * acc_ref[h] + jnp.dot(p.astype(BF16), v,
                                                          preferred_element_type=F32)

    scores(0, 0)

    def body(i, carry):
        c = 2 * i
        scores(c + 1, 1)
        weighted_values(c, 0)
        scores(c + 2, 0)
        weighted_values(c + 1, 1)
        return carry

    lax.fori_loop(0, nk // 2 - 1, body, 0)
    scores(nk - 1, 1)
    weighted_values(nk - 2, 0)
    weighted_values(nk - 1, 1)

    outs = []
    for h in range(2):
        acc = acc_ref[h]
        outs.append(acc / acc[:, VDIM:VDIM + 1])
    lane = lax.broadcasted_iota(jnp.int32, (tq, LANES), 1)
    o_ref[0] = jnp.where(lane < VDIM, outs[0], pltpu.roll(outs[1], VDIM, axis=1)).astype(BF16)


def _mla_attention(q, k, v, tq=512, tk=512):
    b, s, _ = q.shape
    npair = MLA_HEADS // 2
    state = pltpu.VMEM((2, tq, LANES), F32)
    sbuf = pltpu.VMEM((2, tq, tk), F32)
    pair = lambda bi, p, i: (bi, 0, p)
    return pl.pallas_call(
        functools.partial(_mla_kernel, tq=tq, tk=tk, seq=s),
        grid=(b, npair, s // tq),
        in_specs=[pl.BlockSpec((1, tq, 2 * LANES), lambda bi, p, i: (bi, i, p)),
                  pl.BlockSpec((1, s, 2 * LANES), pair),
                  pl.BlockSpec((1, s, 2 * LANES), pair)],
        out_specs=pl.BlockSpec((1, tq, LANES), lambda bi, p, i: (bi, i, p)),
        out_shape=jax.ShapeDtypeStruct((b, s, MLA_HEADS * VDIM), BF16),
        scratch_shapes=[state, state, sbuf, sbuf, state, state, state, state],
        compiler_params=_params(("parallel", "parallel", "parallel")),
        name="mla_attention",
    )(q, k, v)


def _log_sigmoid(x):
    return jnp.minimum(x, 0.0) - jnp.log1p(jnp.exp(-jnp.abs(x)))


def _retention_kernel(q_ref, k_ref, v_ref, g_ref, df_ref, db_ref, o_ref,
                      acc_ref, state_ref, dmat_ref, *, ch, seq):
    nc = seq // ch
    lane = lax.broadcasted_iota(jnp.int32, (ch, LANES), 1)
    row = lax.broadcasted_iota(jnp.int32, (ch, LANES), 0).astype(F32)
    head0 = lane < RET_DIM
    ri = lax.broadcasted_iota(jnp.int32, (ch, ch), 0)
    ci = lax.broadcasted_iota(jnp.int32, (ch, ch), 1)
    sr = lax.broadcasted_iota(jnp.int32, (LANES, LANES), 0)
    sc = lax.broadcasted_iota(jnp.int32, (LANES, LANES), 1)
    same_head = (sr < RET_DIM) == (sc < RET_DIM)

    def direction(lg, fwd):
        if fwd:
            q_dec = jnp.exp(lg * (row + 1.0))
            k_dec = jnp.exp(lg * (ch - 1.0 - row))
            rel = (ri - ci).astype(F32)
            keep = ri >= ci
        else:
            q_dec = jnp.exp(lg * (ch - row))
            k_dec = jnp.exp(lg * row)
            rel = (ci - ri).astype(F32)
            keep = ci > ri
        c_dec = jnp.exp(lg * float(ch))
        rel = jnp.maximum(rel, 0.0)
        for h in range(2):
            lg_h = lg[:, h * RET_DIM:h * RET_DIM + 1]
            dmat_ref[h] = jnp.where(keep, jnp.exp(lg_h * rel), 0.0)
        state_ref[...] = jnp.zeros_like(state_ref)

        def body(t, carry):
            c = t if fwd else nc - 1 - t
            off = pl.multiple_of(c * ch, ch)
            qc = q_ref[0, pl.ds(off, ch), :]
            kc = k_ref[0, pl.ds(off, ch), :]
            vc = v_ref[0, pl.ds(off, ch), :]
            zero = jnp.zeros_like(qc)
            inner = []
            for h in range(2):
                qh = jnp.where(head0, qc, zero) if h == 0 else jnp.where(head0, zero, qc)
                s = lax.dot_general(qh, kc, (((1,), (1,)), ((), ())), preferred_element_type=F32)
                s = (s * dmat_ref[h]).astype(BF16)
                inner.append(jnp.dot(s, vc, preferred_element_type=F32))
            state = state_ref[...]
            cross = jnp.dot(qc, state.astype(BF16), preferred_element_type=F32) * q_dec
            out = jnp.where(head0, inner[0], inner[1]) + cross
            if fwd:
                acc_ref[pl.ds(off, ch), :] = out
            else:
                acc_ref[pl.ds(off, ch), :] += out
            kd = (kc.astype(F32) * k_dec).astype(BF16)
            upd = lax.dot_general(kd, vc, (((0,), (0,)), ((), ())), preferred_element_type=F32)
            state_ref[...] = state * c_dec + jnp.where(same_head, upd, 0.0)
            return carry

        lax.fori_loop(0, nc, body, 0)

    direction(_log_sigmoid(df_ref[...]), True)
    direction(_log_sigmoid(db_ref[...]), False)

    def finish(t, carry):
        off = pl.multiple_of(t * ch, ch)
        o = acc_ref[pl.ds(off, ch), :]

        def head_mean(x):
            m0 = jnp.sum(jnp.where(head0, x, 0.0), axis=-1, keepdims=True)
            m1 = jnp.sum(jnp.where(head0, 0.0, x), axis=-1, keepdims=True)
            return jnp.where(head0, m0, m1) * (1.0 / RET_DIM)

        d = o - head_mean(o)
        var = head_mean(d * d)
        o_ref[0, pl.ds(off, ch), :] = (d * lax.rsqrt(var + EPS)
                                       * g_ref[0, pl.ds(off, ch), :]).astype(BF16)
        return carry

    lax.fori_loop(0, nc, finish, 0)


def _retention(q, k, v, g, dec_f, dec_b, ch=256):
    b, s, _ = q.shape
    npair = RET_HEADS // 2
    blk = pl.BlockSpec((1, s, LANES), lambda bi, p: (bi, 0, p))
    dec = pl.BlockSpec((1, LANES), lambda bi, p: (0, p))
    return pl.pallas_call(
        functools.partial(_retention_kernel, ch=ch, seq=s),
        grid=(b, npair),
        in_specs=[blk, blk, blk, blk, dec, dec],
        out_specs=blk,
        out_shape=jax.ShapeDtypeStruct((b, s, D_RET), BF16),
        scratch_shapes=[pltpu.VMEM((s, LANES), F32), pltpu.VMEM((LANES, LANES), F32),
                        pltpu.VMEM((2, ch, ch), F32)],
        compiler_params=_params(("parallel", "parallel")),
        name="retention",
    )(q, k, v, g, dec_f, dec_b)


def _swap_halves(w, heads, dim):
    k = w.shape[0]
    w4 = w.reshape(k, heads, 2, dim // 2)
    return w4[:, :, ::-1, :].reshape(k, heads * dim)


def _prep_w_in(w_in):
    c_q, c_kv, k_pe, r_q, r_k, r_v, r_g = jnp.split(
        w_in, np.cumsum([Q_LORA, KV_LORA, ROPE, D_RET, D_RET, D_RET])[:], axis=1)
    k = w_in.shape[0]
    z64 = jnp.zeros((k, 64), w_in.dtype)
    z32 = jnp.zeros((k, 32), w_in.dtype)
    kpe = jnp.concatenate([z64, k_pe, z32], axis=1)
    kpe_sw = jnp.concatenate([z64, k_pe[:, 16:], k_pe[:, :16], z32], axis=1)
    return jnp.concatenate(
        [c_q, c_kv, kpe, kpe_sw, r_q, _swap_halves(r_q, RET_HEADS, RET_DIM),
         r_k, _swap_halves(r_k, RET_HEADS, RET_DIM), r_v, r_g], axis=1).astype(BF16)


def _prep_w_uq(w_uq):
    k = w_uq.shape[0]
    w3 = w_uq.reshape(k, MLA_HEADS, NOPE + ROPE)
    nope, x1, x2 = w3[:, :, :NOPE], w3[:, :, NOPE:NOPE + 16], w3[:, :, NOPE + 16:]
    z32 = jnp.zeros((k, MLA_HEADS, 32), w_uq.dtype)
    wq = jnp.concatenate([nope, x1, x2, z32], axis=2).reshape(k, MLA_HEADS * LANES)
    wqs = jnp.concatenate([jnp.zeros_like(nope), x2, x1, z32], axis=2).reshape(k, MLA_HEADS * LANES)
    return wq.astype(BF16), wqs.astype(BF16)


def _prep_w_ukv(w_ukv):
    k = w_ukv.shape[0]
    w3 = w_ukv.reshape(k, MLA_HEADS, NOPE + VDIM)
    kn = jnp.concatenate([w3[:, :, :NOPE], jnp.zeros((k, MLA_HEADS, 64), w_ukv.dtype)], axis=2)
    v = jnp.concatenate([w3[:, :, NOPE:], jnp.zeros((k, MLA_HEADS, 64), w_ukv.dtype)], axis=2)
    return jnp.concatenate([kn.reshape(k, MLA_HEADS * LANES), v.reshape(k, MLA_HEADS * LANES)],
                           axis=1).astype(BF16)


def _lane_expand(dec):
    return jnp.repeat(dec.astype(F32), RET_DIM)[None, :]


def kernel(x, positions, ffn1_norm, ffn1_w_gate, ffn1_w_up, ffn1_w_down, mix_norm, w_in, q_norm, w_uq, kv_norm, w_ukv, ret_decay_fwd, ret_decay_bwd, w_o, ffn2_norm, ffn2_w_gate, ffn2_w_up, ffn2_w_down, final_norm):
    b, s, d = x.shape
    n = b * s
    pos_b = jnp.broadcast_to(positions.astype(F32).reshape(n, 1), (n, LANES))
    cos_t, sin_t = _rope_tables(pos_b)

    h = x.reshape(n, d)
    for l in range(ffn1_norm.shape[0]):
        h = _ffn1(h, ffn1_norm[l][None, :], ffn1_w_gate[l].astype(BF16),
                  ffn1_w_up[l].astype(BF16), ffn1_w_down[l].astype(BF16))
        wq, wqs = _prep_w_uq(w_uq[l])
        qm, km, vm, rq, rk, rv, rg = _proj(
            h, cos_t, sin_t, mix_norm[l][None, :], _prep_w_in(w_in[l]), q_norm[l][None, :],
            wq, wqs, kv_norm[l][None, :], _prep_w_ukv(w_ukv[l]))
        a = _mla_attention(qm.reshape(b, s, -1), km.reshape(b, s, -1), vm.reshape(b, s, -1))
        r = _retention(rq.reshape(b, s, -1), rk.reshape(b, s, -1), rv.reshape(b, s, -1),
                       rg.reshape(b, s, -1), _lane_expand(ret_decay_fwd[l]),
                       _lane_expand(ret_decay_bwd[l]))
        wo = w_o[l].astype(BF16)
        last = l == ffn1_norm.shape[0] - 1
        h = _out_ffn2(h, a.reshape(n, -1), r.reshape(n, -1), wo[:MLA_HEADS * VDIM],
                      wo[MLA_HEADS * VDIM:], ffn2_norm[l][None, :], ffn2_w_gate[l].astype(BF16),
                      ffn2_w_up[l].astype(BF16), ffn2_w_down[l].astype(BF16),
                      final_norm[None, :])
        assert last, "kernel is specialised to DEPTH == 1"
    return h.reshape(b, s, d)
```

```python
import functools
import math

import numpy as np
import jax
import jax.numpy as jnp
from jax import lax
from jax.experimental import pallas as pl
from jax.experimental.pallas import tpu as pltpu

F32 = jnp.float32
BF16 = jnp.bfloat16

D_MODEL = 1024
D_FF = 2816
MLA_HEADS = 8
Q_LORA = 256
KV_LORA = 128
NOPE = 64
ROPE = 32
VDIM = 64
RET_HEADS = 8
RET_DIM = 64
D_RET = RET_HEADS * RET_DIM
ROPE_THETA = 10000.0
EPS = 1e-6
MLA_SCALE = (NOPE + ROPE) ** -0.5
LOG2E = math.log2(math.e)

LANES = 128
QUAD = 256
VMEM_LIMIT = 56 * 1024 * 1024

_OFF_CQ = 0
_OFF_CKV = 256
_OFF_RQ = 768
_OFF_RK = 1280
_OFF_RV = 1792
_OFF_RG = 2304
D_IN_AUG = 2816


def _params(sem):
    return pltpu.CompilerParams(dimension_semantics=sem, vmem_limit_bytes=VMEM_LIMIT)


def _const_spec(shape):
    nd = len(shape)
    return pl.BlockSpec(shape, lambda *_: (0,) * nd, pipeline_mode=pl.Buffered(1))


def _rope_table_kernel(pos_ref, inv_ref, sgn_ref, one_ref, cos_ref, sin_ref):
    pos = pos_ref[...]
    for t in range(2):
        ang = pos * inv_ref[t]
        cos_ref[t] = jnp.cos(ang) * jnp.abs(sgn_ref[t]) + one_ref[t]
        sin_ref[t] = jnp.sin(ang) * sgn_ref[t]


def _rope_constants():
    lane = np.arange(LANES)
    inv_r = ROPE_THETA ** (-(lane % 32).astype(np.float64) / 32.0)
    sgn_r = np.ones(LANES)
    one_r = np.zeros(LANES)
    in_pe = (lane >= 64) & (lane < 96)
    inv_m = np.where(in_pe, ROPE_THETA ** (-((lane - 64) % 16).astype(np.float64) / 16.0), 0.0)
    sgn_m = np.where(in_pe, np.where(lane < 80, -1.0, 1.0), 0.0)
    one_m = np.where(lane < 64, 1.0, 0.0)
    mk = lambda a, b: jnp.asarray(np.stack([a, b])[:, None, :], F32)
    return mk(inv_r, inv_m), mk(sgn_r, sgn_m), mk(one_r, one_m)


def _rope_tables(pos_b, tm=2048):
    n = pos_b.shape[0]
    inv, sgn, one = _rope_constants()
    cspec = pl.BlockSpec((2, 1, LANES), lambda i: (0, 0, 0))
    return pl.pallas_call(
        _rope_table_kernel,
        grid=(n // tm,),
        in_specs=[pl.BlockSpec((tm, LANES), lambda i: (i, 0)), cspec, cspec, cspec],
        out_specs=[pl.BlockSpec((2, tm, LANES), lambda i: (0, i, 0))] * 2,
        out_shape=[jax.ShapeDtypeStruct((2, n, LANES), F32)] * 2,
        compiler_params=_params(("parallel",)),
        name="rope_tables",
    )(pos_b, inv, sgn, one)


def _rms(x, g):
    return x * lax.rsqrt(jnp.mean(x * x, axis=-1, keepdims=True) + EPS) * g


def _swiglu_acc(n_bf, wg_ref, wu_ref, wd_ref, ck):
    acc = None
    for j in range(D_FF // ck):
        sl = slice(j * ck, (j + 1) * ck)
        g = jnp.dot(n_bf, wg_ref[:, sl], preferred_element_type=F32)
        u = jnp.dot(n_bf, wu_ref[:, sl], preferred_element_type=F32)
        hm = (g * jax.nn.sigmoid(g) * u).astype(BF16)
        d = jnp.dot(hm, wd_ref[sl, :], preferred_element_type=F32)
        acc = d if acc is None else acc + d
    return acc


def _ffn1_kernel(x_ref, g_ref, wg_ref, wu_ref, wd_ref, o_ref, *, ck):
    x = x_ref[...]
    n_bf = _rms(x, g_ref[...]).astype(BF16)
    o_ref[...] = x + 0.5 * _swiglu_acc(n_bf, wg_ref, wu_ref, wd_ref, ck)


def _ffn1(x, g, wg, wu, wd, tm=512, ck=256):
    n = x.shape[0]
    row = pl.BlockSpec((tm, D_MODEL), lambda i: (i, 0))
    return pl.pallas_call(
        functools.partial(_ffn1_kernel, ck=ck),
        grid=(n // tm,),
        in_specs=[row, _const_spec((1, D_MODEL)), _const_spec((D_MODEL, D_FF)),
                  _const_spec((D_MODEL, D_FF)), _const_spec((D_FF, D_MODEL))],
        out_specs=row,
        out_shape=jax.ShapeDtypeStruct((n, D_MODEL), F32),
        compiler_params=_params(("parallel",)),
        name="ffn1",
    )(x, g, wg, wu, wd)


def _out_ffn2_kernel(h_ref, a_ref, r_ref, woa_ref, wor_ref, g_ref, wg_ref, wu_ref, wd_ref,
                     gf_ref, o_ref, *, ck):
    h = (h_ref[...]
         + jnp.dot(a_ref[...], woa_ref[...], preferred_element_type=F32)
         + jnp.dot(r_ref[...], wor_ref[...], preferred_element_type=F32))
    n_bf = _rms(h, g_ref[...]).astype(BF16)
    h = h + 0.5 * _swiglu_acc(n_bf, wg_ref, wu_ref, wd_ref, ck)
    o_ref[...] = _rms(h, gf_ref[...])


def _out_ffn2(h, a, r, woa, wor, g, wg, wu, wd, gf, tm=512, ck=256):
    n = h.shape[0]
    row = pl.BlockSpec((tm, D_MODEL), lambda i: (i, 0))
    half = pl.BlockSpec((tm, D_RET), lambda i: (i, 0))
    return pl.pallas_call(
        functools.partial(_out_ffn2_kernel, ck=ck),
        grid=(n // tm,),
        in_specs=[row, half, half, _const_spec((D_RET, D_MODEL)), _const_spec((D_RET, D_MODEL)),
                  _const_spec((1, D_MODEL)), _const_spec((D_MODEL, D_FF)),
                  _const_spec((D_MODEL, D_FF)), _const_spec((D_FF, D_MODEL)),
                  _const_spec((1, D_MODEL))],
        out_specs=row,
        out_shape=jax.ShapeDtypeStruct((n, D_MODEL), F32),
        compiler_params=_params(("parallel",)),
        name="out_ffn2",
    )(h, a, r, woa, wor, g, wg, wu, wd, gf)


def _proj_kernel(h_ref, cos_ref, sin_ref, gm_ref, win_ref, gq_ref, wq_ref, gkv_ref, wkv_ref,
                 qm_ref, km_ref, vm_ref, rq_ref, rk_ref, rv_ref, rg_ref):
    n_bf = _rms(h_ref[...], gm_ref[...]).astype(BF16)
    cr, sr = cos_ref[0], sin_ref[0]
    cm, sm = cos_ref[1], sin_ref[1]

    def proj(off, width):
        return jnp.dot(n_bf, win_ref[:, off:off + width], preferred_element_type=F32)

    for off, out_ref, scale in ((_OFF_RQ, rq_ref, 1.0), (_OFF_RK, rk_ref, RET_DIM ** -0.5)):
        x = proj(off, D_RET)
        for u in range(D_RET // QUAD):
            x1 = x[:, u * QUAD:u * QUAD + LANES]
            x2 = x[:, u * QUAD + LANES:(u + 1) * QUAD]
            out_ref[:, u * QUAD:u * QUAD + LANES] = ((x1 * cr - x2 * sr) * scale).astype(BF16)
            out_ref[:, u * QUAD + LANES:(u + 1) * QUAD] = ((x1 * sr + x2 * cr) * scale).astype(BF16)
    rv_ref[...] = proj(_OFF_RV, D_RET).astype(BF16)
    g = proj(_OFF_RG, D_RET)
    rg_ref[...] = (g * jax.nn.sigmoid(g)).astype(BF16)

    cq_bf = _rms(proj(_OFF_CQ, Q_LORA), gq_ref[...]).astype(BF16)
    lat = proj(_OFF_CKV, 4 * LANES)
    ckv_bf = _rms(lat[:, :KV_LORA], gkv_ref[...]).astype(BF16)
    kpe = lat[:, LANES:2 * LANES] * cm + lat[:, 2 * LANES:3 * LANES] * sm
    qq = jnp.dot(cq_bf, wq_ref[...], preferred_element_type=F32)
    kv = jnp.dot(ckv_bf, wkv_ref[...], preferred_element_type=F32)
    nh = MLA_HEADS * LANES
    for h in range(MLA_HEADS):
        sl = slice(h * LANES, (h + 1) * LANES)
        sw = slice(nh + h * LANES, nh + (h + 1) * LANES)
        qm_ref[:, sl] = ((qq[:, sl] * cm + qq[:, sw] * sm) * (MLA_SCALE * LOG2E)).astype(BF16)
        km_ref[:, sl] = (kv[:, sl] + kpe).astype(BF16)
    lane = lax.broadcasted_iota(jnp.int32, (1, nh), 1)
    ones_lane = jnp.where(lane % LANES == VDIM, 1.0, 0.0)
    vm_ref[...] = (kv[:, nh:] + ones_lane).astype(BF16)


def _proj(h, cos_t, sin_t, gm, win, gq, wq, gkv, wkv, tm=512):
    n = h.shape[0]
    row = lambda w: pl.BlockSpec((tm, w), lambda i: (i, 0))
    tab = pl.BlockSpec((2, tm, LANES), lambda i: (0, i, 0))
    sds = lambda w: jax.ShapeDtypeStruct((n, w), BF16)
    return pl.pallas_call(
        _proj_kernel,
        grid=(n // tm,),
        in_specs=[row(D_MODEL), tab, tab, _const_spec((1, D_MODEL)),
                  _const_spec((D_MODEL, D_IN_AUG)), _const_spec((1, Q_LORA)),
                  _const_spec((Q_LORA, 2 * MLA_HEADS * LANES)),
                  _const_spec((1, KV_LORA)), _const_spec((KV_LORA, 2 * MLA_HEADS * LANES))],
        out_specs=[row(1024), row(1024), row(1024), row(512), row(512), row(512), row(512)],
        out_shape=[sds(1024), sds(1024), sds(1024), sds(512), sds(512), sds(512), sds(512)],
        compiler_params=_params(("parallel",)),
        name="mixer_proj",
    )(h, cos_t, sin_t, gm, win, gq, wq, gkv, wkv)


def _mla_kernel(q_ref, k_ref, v_ref, o_ref, m_ref, acc_ref, s0_ref, s1_ref, mb0_ref, mb1_ref,
                al0_ref, al1_ref, *, tq, tk, seq, unroll):
    nk = seq // tk
    nrep = tk // LANES
    slots = ((s0_ref, mb0_ref, al0_ref), (s1_ref, mb1_ref, al1_ref))
    m_ref[...] = jnp.full(m_ref.shape, -jnp.inf, F32)
    acc_ref[...] = jnp.zeros(acc_ref.shape, F32)

    def scores(c, slot):
        s_ref, mb_ref, al_ref = slots[slot]
        off = c * tk if isinstance(c, int) else pl.multiple_of(c * tk, tk)
        for h in range(2):
            q = q_ref[0, :, h * LANES:(h + 1) * LANES]
            k = k_ref[0, pl.ds(off, tk), h * LANES:(h + 1) * LANES]
            s = lax.dot_general(q, k, (((1,), (1,)), ((), ())), preferred_element_type=F32)
            s_ref[h] = s
            m_old = m_ref[h]
            m_new = jnp.maximum(m_old, jnp.max(s, axis=-1, keepdims=True))
            m_ref[h] = m_new
            mb_ref[h] = m_new
            al_ref[h] = jnp.exp2(m_old - m_new)

    def weighted_values(c, slot):
        s_ref, mb_ref, al_ref = slots[slot]
        off = c * tk if isinstance(c, int) else pl.multiple_of(c * tk, tk)
        for h in range(2):
            mb = mb_ref[h]
            p = jnp.exp2(s_ref[h] - jnp.concatenate([mb] * nrep, axis=1))
            v = v_ref[0, pl.ds(off, tk), h * LANES:(h + 1) * LANES]
            acc_ref[h] = al_ref[h] * acc_ref[h] + jnp.dot(p.astype(BF16), v,
                                                          preferred_element_type=F32)

    def chunk_group(c0, last):
        for j in range(unroll):
            if not (last and j == unroll - 1):
                scores(c0 + j + 1, (j + 1) % 2)
            weighted_values(c0 + j, j % 2)

    scores(0, 0)

    def body(i, carry):
        chunk_group(i * unroll, False)
        return carry

    lax.fori_loop(0, nk // unroll - 1, body, 0)
    chunk_group(nk - unroll, True)

    outs = []
    for h in range(2):
        acc = acc_ref[h]
        outs.append(acc / acc[:, VDIM:VDIM + 1])
    lane = lax.broadcasted_iota(jnp.int32, (tq, LANES), 1)
    o_ref[0] = jnp.where(lane < VDIM, outs[0], pltpu.roll(outs[1], VDIM, axis=1)).astype(BF16)


def _mla_attention(q, k, v, tq=512, tk=512, unroll=4):
    b, s, _ = q.shape
    npair = MLA_HEADS // 2
    assert unroll % 2 == 0 and (s // tk) % unroll == 0
    state = pltpu.VMEM((2, tq, LANES), F32)
    sbuf = pltpu.VMEM((2, tq, tk), F32)
    pair = lambda bi, p, i: (bi, 0, p)
    return pl.pallas_call(
        functools.partial(_mla_kernel, tq=tq, tk=tk, seq=s, unroll=unroll),
        grid=(b, npair, s // tq),
        in_specs=[pl.BlockSpec((1, tq, 2 * LANES), lambda bi, p, i: (bi, i, p)),
                  pl.BlockSpec((1, s, 2 * LANES), pair),
                  pl.BlockSpec((1, s, 2 * LANES), pair)],
        out_specs=pl.BlockSpec((1, tq, LANES), lambda bi, p, i: (bi, i, p)),
        out_shape=jax.ShapeDtypeStruct((b, s, MLA_HEADS * VDIM), BF16),
        scratch_shapes=[state, state, sbuf, sbuf, state, state, state, state],
        compiler_params=_params(("parallel", "parallel", "parallel")),
        name="mla_attention",
    )(q, k, v)


def _log_sigmoid(x):
    return jnp.minimum(x, 0.0) - jnp.log1p(jnp.exp(-jnp.abs(x)))


def _retention_kernel(q_ref, k_ref, v_ref, g_ref, dec_ref, o_ref, acc_lo_ref, acc_hi_ref,
                      stf_ref, stb_ref, dmf_ref, dmb_ref, tab_ref, ones_ref, *, ch, seq):
    nc = seq // ch
    half = nc // 2
    nh = QUAD // RET_DIM
    lane = lax.broadcasted_iota(jnp.int32, (ch, QUAD), 1)
    row = lax.broadcasted_iota(jnp.int32, (ch, QUAD), 0).astype(F32)
    khead = (lane >> 5) & (nh - 1)
    vhead = lane >> 6
    ri = lax.broadcasted_iota(jnp.int32, (ch, ch), 0)
    ci = lax.broadcasted_iota(jnp.int32, (ch, ch), 1)
    sd = lax.broadcasted_iota(jnp.int32, (QUAD, QUAD), 0)
    se = lax.broadcasted_iota(jnp.int32, (QUAD, QUAD), 1)
    state_mask = ((sd >> 5) & (nh - 1)) == (se >> 6)
    ones_ref[...] = jnp.where((sd >> 6) == (se >> 6), 1.0, 0.0).astype(BF16)

    lg = _log_sigmoid(dec_ref[...])
    tab_ref[0] = jnp.exp(lg[0:1] * (row + 1.0))
    tab_ref[1] = jnp.exp(lg[1:2] * (ch - 1.0 - row))
    tab_ref[2] = jnp.exp(lg[2:3] * (ch - row))
    tab_ref[3] = jnp.exp(lg[3:4] * row)
    c_dec = (jnp.exp(lg[0:1] * float(ch)), jnp.exp(lg[2:3] * float(ch)))
    rel_f = jnp.maximum(ri - ci, 0).astype(F32)
    rel_b = jnp.maximum(ci - ri, 0).astype(F32)
    for h in range(nh):
        dmf_ref[h] = jnp.where(ri >= ci, jnp.exp(lg[0:1, h * RET_DIM:h * RET_DIM + 1] * rel_f), 0.0)
        dmb_ref[h] = jnp.where(ci > ri, jnp.exp(lg[2:3, h * RET_DIM:h * RET_DIM + 1] * rel_b), 0.0)
    stf_ref[...] = jnp.zeros_like(stf_ref)
    stb_ref[...] = jnp.zeros_like(stb_ref)

    def chunk(c, fwd):
        dm_ref, st_ref = (dmf_ref, stf_ref) if fwd else (dmb_ref, stb_ref)
        off = pl.multiple_of(c * ch, ch)
        qc = q_ref[0, pl.ds(off, ch), :]
        kc = k_ref[0, pl.ds(off, ch), :]
        vc = v_ref[0, pl.ds(off, ch), :]
        zero = jnp.zeros_like(qc)
        out = None
        for h in range(nh):
            qh = jnp.where(khead == h, qc, zero)
            s = lax.dot_general(qh, kc, (((1,), (1,)), ((), ())), preferred_element_type=F32)
            s = (s * dm_ref[h]).astype(BF16)
            ih = jnp.dot(s, vc, preferred_element_type=F32)
            out = ih if out is None else jnp.where(vhead == h, ih, out)
        state = st_ref[...]
        cross = jnp.dot(qc, state.astype(BF16), preferred_element_type=F32)
        out = out + cross * tab_ref[0 if fwd else 2]
        kd = (kc.astype(F32) * tab_ref[1 if fwd else 3]).astype(BF16)
        upd = lax.dot_general(kd, vc, (((0,), (0,)), ((), ())), preferred_element_type=F32)
        st_ref[...] = state * c_dec[0 if fwd else 1] + jnp.where(state_mask, upd, 0.0)
        return out

    def head_sum(x):
        hi = x.astype(BF16)
        lo = (x - hi.astype(F32)).astype(BF16)
        return (jnp.dot(hi, ones_ref[...], preferred_element_type=F32)
                + jnp.dot(lo, ones_ref[...], preferred_element_type=F32))

    def finish(c, tot):
        off = pl.multiple_of(c * ch, ch)
        d = tot - head_sum(tot) * (1.0 / RET_DIM)
        var = head_sum(d * d) * (1.0 / RET_DIM)
        gate = g_ref[0, pl.ds(off, ch), :].astype(F32)
        o_ref[0, pl.ds(off, ch), :] = (d * lax.rsqrt(var + EPS) * gate).astype(BF16)

    def first(t, carry):
        cb = nc - 1 - t
        acc_lo_ref[pl.ds(pl.multiple_of(t * ch, ch), ch), :] = chunk(t, True)
        acc_hi_ref[pl.ds(pl.multiple_of((cb - half) * ch, ch), ch), :] = chunk(cb, False)
        return carry

    lax.fori_loop(0, half, first, 0)

    def second(t, carry):
        cb = nc - 1 - t
        finish(t, chunk(t, True)
               + acc_hi_ref[pl.ds(pl.multiple_of((t - half) * ch, ch), ch), :])
        finish(cb, chunk(cb, False) + acc_lo_ref[pl.ds(pl.multiple_of(cb * ch, ch), ch), :])
        return carry

    lax.fori_loop(half, nc, second, 0)


def _retention(q, k, v, g, dec, ch=256):
    b, s, _ = q.shape
    nquad = D_RET // QUAD
    blk = pl.BlockSpec((1, s, QUAD), lambda bi, u: (bi, 0, u))
    return pl.pallas_call(
        functools.partial(_retention_kernel, ch=ch, seq=s),
        grid=(b, nquad),
        in_specs=[blk, blk, blk, blk, pl.BlockSpec((4, QUAD), lambda bi, u: (0, u))],
        out_specs=blk,
        out_shape=jax.ShapeDtypeStruct((b, s, D_RET), BF16),
        scratch_shapes=[pltpu.VMEM((s // 2, QUAD), F32), pltpu.VMEM((s // 2, QUAD), F32),
                        pltpu.VMEM((QUAD, QUAD), F32), pltpu.VMEM((QUAD, QUAD), F32),
                        pltpu.VMEM((QUAD // RET_DIM, ch, ch), F32),
                        pltpu.VMEM((QUAD // RET_DIM, ch, ch), F32),
                        pltpu.VMEM((4, ch, QUAD), F32), pltpu.VMEM((QUAD, QUAD), BF16)],
        compiler_params=_params(("parallel", "parallel")),
        name="retention",
    )(q, k, v, g, dec)


def _ret_qk_perm():
    new = np.arange(D_RET)
    quad, j = new // QUAD, new % QUAD
    head = 4 * quad + (j % LANES) // 32
    dim = (j // LANES) * 32 + j % 32
    return head * RET_DIM + dim


def _ret_lane_heads():
    lane = np.arange(D_RET)
    v_head = lane // RET_DIM
    k_head = 4 * (lane // QUAD) + ((lane % QUAD) % LANES) // 32
    return v_head, k_head


def _prep_w_in(w_in):
    c_q, c_kv, k_pe, r_q, r_k, r_v, r_g = jnp.split(
        w_in, np.cumsum([Q_LORA, KV_LORA, ROPE, D_RET, D_RET, D_RET])[:], axis=1)
    k = w_in.shape[0]
    z = lambda w: jnp.zeros((k, w), w_in.dtype)
    kpe = jnp.concatenate([z(64), k_pe, z(32)], axis=1)
    kpe_sw = jnp.concatenate([z(64), k_pe[:, 16:], k_pe[:, :16], z(32)], axis=1)
    perm = _ret_qk_perm()
    return jnp.concatenate([c_q, c_kv, kpe, kpe_sw, z(LANES), r_q[:, perm], r_k[:, perm],
                            r_v, r_g], axis=1).astype(BF16)


def _prep_w_uq(w_uq):
    k = w_uq.shape[0]
    w3 = w_uq.reshape(k, MLA_HEADS, NOPE + ROPE)
    nope, x1, x2 = w3[:, :, :NOPE], w3[:, :, NOPE:NOPE + 16], w3[:, :, NOPE + 16:]
    z32 = jnp.zeros((k, MLA_HEADS, 32), w_uq.dtype)
    wq = jnp.concatenate([nope, x1, x2, z32], axis=2).reshape(k, MLA_HEADS * LANES)
    wqs = jnp.concatenate([jnp.zeros_like(nope), x2, x1, z32], axis=2).reshape(k, MLA_HEADS * LANES)
    return jnp.concatenate([wq, wqs], axis=1).astype(BF16)


def _prep_w_ukv(w_ukv):
    k = w_ukv.shape[0]
    w3 = w_ukv.reshape(k, MLA_HEADS, NOPE + VDIM)
    kn = jnp.concatenate([w3[:, :, :NOPE], jnp.zeros((k, MLA_HEADS, 64), w_ukv.dtype)], axis=2)
    v = jnp.concatenate([w3[:, :, NOPE:], jnp.zeros((k, MLA_HEADS, 64), w_ukv.dtype)], axis=2)
    return jnp.concatenate([kn.reshape(k, MLA_HEADS * LANES), v.reshape(k, MLA_HEADS * LANES)],
                           axis=1).astype(BF16)


def _prep_decay(dec_f, dec_b):
    v_head, k_head = _ret_lane_heads()
    f, b = dec_f.astype(F32), dec_b.astype(F32)
    return jnp.stack([f[v_head], f[k_head], b[v_head], b[k_head]])


def kernel(x, positions, ffn1_norm, ffn1_w_gate, ffn1_w_up, ffn1_w_down, mix_norm, w_in, q_norm, w_uq, kv_norm, w_ukv, ret_decay_fwd, ret_decay_bwd, w_o, ffn2_norm, ffn2_w_gate, ffn2_w_up, ffn2_w_down, final_norm):
    b, s, d = x.shape
    n = b * s
    assert ffn1_norm.shape[0] == 1, "specialised to DEPTH == 1 (the final norm is fused into the layer)"
    pos_b = jnp.broadcast_to(positions.astype(F32).reshape(n, 1), (n, LANES))
    cos_t, sin_t = _rope_tables(pos_b)

    h = _ffn1(x.reshape(n, d), ffn1_norm[0][None, :], ffn1_w_gate[0].astype(BF16),
              ffn1_w_up[0].astype(BF16), ffn1_w_down[0].astype(BF16))
    qm, km, vm, rq, rk, rv, rg = _proj(
        h, cos_t, sin_t, mix_norm[0][None, :], _prep_w_in(w_in[0]), q_norm[0][None, :],
        _prep_w_uq(w_uq[0]), kv_norm[0][None, :], _prep_w_ukv(w_ukv[0]))
    a = _mla_attention(qm.reshape(b, s, -1), km.reshape(b, s, -1), vm.reshape(b, s, -1))
    r = _retention(rq.reshape(b, s, -1), rk.reshape(b, s, -1), rv.reshape(b, s, -1),
                   rg.reshape(b, s, -1), _prep_decay(ret_decay_fwd[0], ret_decay_bwd[0]))
    wo = w_o[0].astype(BF16)
    out = _out_ffn2(h, a.reshape(n, -1), r.reshape(n, -1), wo[:MLA_HEADS * VDIM],
                    wo[MLA_HEADS * VDIM:], ffn2_norm[0][None, :], ffn2_w_gate[0].astype(BF16),
                    ffn2_w_up[0].astype(BF16), ffn2_w_down[0].astype(BF16), final_norm[None, :])
    return out.reshape(b, s, d)
```

```python
import functools
import math

import numpy as np
import jax
import jax.numpy as jnp
from jax import lax
from jax.experimental import pallas as pl
from jax.experimental.pallas import tpu as pltpu

F32 = jnp.float32
BF16 = jnp.bfloat16

D_MODEL = 1024
D_FF = 2816
MLA_HEADS = 8
Q_LORA = 256
KV_LORA = 128
NOPE = 64
ROPE = 32
VDIM = 64
RET_HEADS = 8
RET_DIM = 64
D_RET = RET_HEADS * RET_DIM
ROPE_THETA = 10000.0
EPS = 1e-6
MLA_SCALE = (NOPE + ROPE) ** -0.5
LOG2E = math.log2(math.e)

LANES = 128
QUAD = 256
VMEM_LIMIT = 56 * 1024 * 1024

_OFF_CQ = 0
_OFF_CKV = 256
_OFF_RQ = 768
_OFF_RK = 1280
_OFF_RV = 1792
_OFF_RG = 2304
D_IN_AUG = 2816


def _params(sem):
    return pltpu.CompilerParams(dimension_semantics=sem, vmem_limit_bytes=VMEM_LIMIT)


def _const_spec(shape):
    nd = len(shape)
    return pl.BlockSpec(shape, lambda *_: (0,) * nd, pipeline_mode=pl.Buffered(1))


N_FREQ = 64


def _rope_constants():
    row = np.arange(N_FREQ)
    inv = np.where(row < 32, ROPE_THETA ** (-(row % 32) / 32.0),
                   np.where(row < 48, ROPE_THETA ** (-((row - 32) % 16) / 16.0), 0.0))
    pc = np.zeros((N_FREQ, 2 * LANES))
    ps = np.zeros((N_FREQ, 2 * LANES))
    for l in range(LANES):
        pc[l % 32, l] = 1.0
        ps[l % 32, l] = 1.0
    for l in range(NOPE, NOPE + ROPE):
        r = 32 + (l - NOPE) % 16
        pc[r, LANES + l] = 1.0
        ps[r, LANES + l] = -1.0 if l < NOPE + 16 else 1.0
    return jnp.asarray(inv[:, None], F32), jnp.asarray(pc, BF16), jnp.asarray(ps, BF16)


def _rope_tables(pos_row, inv_col, pc, ps):
    ang = inv_col * pos_row
    dn = (((0,), (0,)), ((), ()))

    def place(x, p):
        hi = x.astype(BF16)
        lo = (x - hi.astype(F32)).astype(BF16)
        return (lax.dot_general(hi, p, dn, preferred_element_type=F32)
                + lax.dot_general(lo, p, dn, preferred_element_type=F32))

    return place(jnp.cos(ang), pc), place(jnp.sin(ang), ps)


def _rms(x, g):
    return x * lax.rsqrt(jnp.mean(x * x, axis=-1, keepdims=True) + EPS) * g


def _swiglu_acc(n_bf, wg_ref, wu_ref, wd_ref, ck):
    acc = None
    for j in range(D_FF // ck):
        sl = slice(j * ck, (j + 1) * ck)
        g = jnp.dot(n_bf, wg_ref[:, sl], preferred_element_type=F32)
        u = jnp.dot(n_bf, wu_ref[:, sl], preferred_element_type=F32)
        hm = (g * jax.nn.sigmoid(g) * u).astype(BF16)
        d = jnp.dot(hm, wd_ref[sl, :], preferred_element_type=F32)
        acc = d if acc is None else acc + d
    return acc


def _ffn1_kernel(x_ref, g_ref, wg_ref, wu_ref, wd_ref, o_ref, *, ck):
    x = x_ref[...]
    n_bf = _rms(x, g_ref[...]).astype(BF16)
    o_ref[...] = x + 0.5 * _swiglu_acc(n_bf, wg_ref, wu_ref, wd_ref, ck)


def _ffn1(x, g, wg, wu, wd, tm=512, ck=256):
    n = x.shape[0]
    row = pl.BlockSpec((tm, D_MODEL), lambda i: (i, 0))
    return pl.pallas_call(
        functools.partial(_ffn1_kernel, ck=ck),
        grid=(n // tm,),
        in_specs=[row, _const_spec((1, D_MODEL)), _const_spec((D_MODEL, D_FF)),
                  _const_spec((D_MODEL, D_FF)), _const_spec((D_FF, D_MODEL))],
        out_specs=row,
        out_shape=jax.ShapeDtypeStruct((n, D_MODEL), F32),
        compiler_params=_params(("parallel",)),
        name="ffn1",
    )(x, g, wg, wu, wd)


def _out_ffn2_kernel(h_ref, a_ref, r_ref, woa_ref, wor_ref, g_ref, wg_ref, wu_ref, wd_ref,
                     gf_ref, o_ref, *, ck):
    h = (h_ref[...]
         + jnp.dot(a_ref[...], woa_ref[...], preferred_element_type=F32)
         + jnp.dot(r_ref[...], wor_ref[...], preferred_element_type=F32))
    n_bf = _rms(h, g_ref[...]).astype(BF16)
    h = h + 0.5 * _swiglu_acc(n_bf, wg_ref, wu_ref, wd_ref, ck)
    o_ref[...] = _rms(h, gf_ref[...])


def _out_ffn2(h, a, r, woa, wor, g, wg, wu, wd, gf, tm=512, ck=256):
    n = h.shape[0]
    row = pl.BlockSpec((tm, D_MODEL), lambda i: (i, 0))
    half = pl.BlockSpec((tm, D_RET), lambda i: (i, 0))
    return pl.pallas_call(
        functools.partial(_out_ffn2_kernel, ck=ck),
        grid=(n // tm,),
        in_specs=[row, half, half, _const_spec((D_RET, D_MODEL)), _const_spec((D_RET, D_MODEL)),
                  _const_spec((1, D_MODEL)), _const_spec((D_MODEL, D_FF)),
                  _const_spec((D_MODEL, D_FF)), _const_spec((D_FF, D_MODEL)),
                  _const_spec((1, D_MODEL))],
        out_specs=row,
        out_shape=jax.ShapeDtypeStruct((n, D_MODEL), F32),
        compiler_params=_params(("parallel",)),
        name="out_ffn2",
    )(h, a, r, woa, wor, g, wg, wu, wd, gf)


def _proj_kernel(h_ref, pos_ref, inv_ref, pc_ref, ps_ref, gm_ref, win_ref, gq_ref, wq_ref, gkv_ref,
                 wkv_ref, qm_ref, km_ref, vm_ref, rq_ref, rk_ref, rv_ref, rg_ref):
    n_bf = _rms(h_ref[...], gm_ref[...]).astype(BF16)
    cos_t, sin_t = _rope_tables(pos_ref[...], inv_ref[...], pc_ref[...], ps_ref[...])
    cr, sr = cos_t[:, :LANES], sin_t[:, :LANES]
    nope_lane = lax.broadcasted_iota(jnp.int32, (1, LANES), 1) < NOPE
    cm, sm = cos_t[:, LANES:] + jnp.where(nope_lane, 1.0, 0.0), sin_t[:, LANES:]

    def proj(off, width):
        return jnp.dot(n_bf, win_ref[:, off:off + width], preferred_element_type=F32)

    for off, out_ref, scale in ((_OFF_RQ, rq_ref, 1.0), (_OFF_RK, rk_ref, RET_DIM ** -0.5)):
        x = proj(off, D_RET)
        for u in range(D_RET // QUAD):
            x1 = x[:, u * QUAD:u * QUAD + LANES]
            x2 = x[:, u * QUAD + LANES:(u + 1) * QUAD]
            out_ref[:, u * QUAD:u * QUAD + LANES] = ((x1 * cr - x2 * sr) * scale).astype(BF16)
            out_ref[:, u * QUAD + LANES:(u + 1) * QUAD] = ((x1 * sr + x2 * cr) * scale).astype(BF16)
    rv_ref[...] = proj(_OFF_RV, D_RET).astype(BF16)
    g = proj(_OFF_RG, D_RET)
    rg_ref[...] = (g * jax.nn.sigmoid(g)).astype(BF16)

    cq_bf = _rms(proj(_OFF_CQ, Q_LORA), gq_ref[...]).astype(BF16)
    lat = proj(_OFF_CKV, 4 * LANES)
    ckv_bf = _rms(lat[:, :KV_LORA], gkv_ref[...]).astype(BF16)
    kpe = lat[:, LANES:2 * LANES] * cm + lat[:, 2 * LANES:3 * LANES] * sm
    qq = jnp.dot(cq_bf, wq_ref[...], preferred_element_type=F32)
    kv = jnp.dot(ckv_bf, wkv_ref[...], preferred_element_type=F32)
    nh = MLA_HEADS * LANES
    for h in range(MLA_HEADS):
        sl = slice(h * LANES, (h + 1) * LANES)
        sw = slice(nh + h * LANES, nh + (h + 1) * LANES)
        qm_ref[:, sl] = ((qq[:, sl] * cm + qq[:, sw] * sm) * (MLA_SCALE * LOG2E)).astype(BF16)
        km_ref[:, sl] = (kv[:, sl] + kpe).astype(BF16)
    lane = lax.broadcasted_iota(jnp.int32, (1, nh), 1)
    ones_lane = jnp.where(lane % LANES == VDIM, 1.0, 0.0)
    vm_ref[...] = (kv[:, nh:] + ones_lane).astype(BF16)


def _proj(h, pos_row, gm, win, gq, wq, gkv, wkv, tm=512):
    n = h.shape[0]
    row = lambda w: pl.BlockSpec((tm, w), lambda i: (i, 0))
    sds = lambda w: jax.ShapeDtypeStruct((n, w), BF16)
    inv_col, pc, ps = _rope_constants()
    return pl.pallas_call(
        _proj_kernel,
        grid=(n // tm,),
        in_specs=[row(D_MODEL), pl.BlockSpec((1, tm), lambda i: (0, i)),
                  _const_spec((N_FREQ, 1)), _const_spec((N_FREQ, 2 * LANES)),
                  _const_spec((N_FREQ, 2 * LANES)), _const_spec((1, D_MODEL)),
                  _const_spec((D_MODEL, D_IN_AUG)), _const_spec((1, Q_LORA)),
                  _const_spec((Q_LORA, 2 * MLA_HEADS * LANES)),
                  _const_spec((1, KV_LORA)), _const_spec((KV_LORA, 2 * MLA_HEADS * LANES))],
        out_specs=[row(1024), row(1024), row(1024), row(512), row(512), row(512), row(512)],
        out_shape=[sds(1024), sds(1024), sds(1024), sds(512), sds(512), sds(512), sds(512)],
        compiler_params=_params(("parallel",)),
        name="mixer_proj",
    )(h, pos_row, inv_col, pc, ps, gm, win, gq, wq, gkv, wkv)


def _mla_kernel(q_ref, k_ref, v_ref, o_ref, m_ref, acc_ref, s0_ref, s1_ref, mb0_ref, mb1_ref,
                al0_ref, al1_ref, *, tq, tk, seq, unroll):
    nk = seq // tk
    nrep = tk // LANES
    slots = ((s0_ref, mb0_ref, al0_ref), (s1_ref, mb1_ref, al1_ref))
    m_ref[...] = jnp.full(m_ref.shape, -jnp.inf, F32)
    acc_ref[...] = jnp.zeros(acc_ref.shape, F32)

    def scores(c, slot):
        s_ref, mb_ref, al_ref = slots[slot]
        off = c * tk if isinstance(c, int) else pl.multiple_of(c * tk, tk)
        for h in range(2):
            q = q_ref[0, :, h * LANES:(h + 1) * LANES]
            k = k_ref[0, pl.ds(off, tk), h * LANES:(h + 1) * LANES]
            s = lax.dot_general(q, k, (((1,), (1,)), ((), ())), preferred_element_type=F32)
            s_ref[h] = s
            m_old = m_ref[h]
            m_new = jnp.maximum(m_old, jnp.max(s, axis=-1, keepdims=True))
            m_ref[h] = m_new
            mb_ref[h] = m_new
            al_ref[h] = jnp.exp2(m_old - m_new)

    def weighted_values(c, slot):
        s_ref, mb_ref, al_ref = slots[slot]
        off = c * tk if isinstance(c, int) else pl.multiple_of(c * tk, tk)
        for h in range(2):
            mb = mb_ref[h]
            p = jnp.exp2(s_ref[h] - jnp.concatenate([mb] * nrep, axis=1))
            v = v_ref[0, pl.ds(off, tk), h * LANES:(h + 1) * LANES]
            acc_ref[h] = al_ref[h] * acc_ref[h] + jnp.dot(p.astype(BF16), v,
                                                          preferred_element_type=F32)

    def chunk_group(c0, last):
        for j in range(unroll):
            if not (last and j == unroll - 1):
                scores(c0 + j + 1, (j + 1) % 2)
            weighted_values(c0 + j, j % 2)

    scores(0, 0)

    def body(i, carry):
        chunk_group(i * unroll, False)
        return carry

    lax.fori_loop(0, nk // unroll - 1, body, 0)
    chunk_group(nk - unroll, True)

    outs = []
    for h in range(2):
        acc = acc_ref[h]
        outs.append(acc / acc[:, VDIM:VDIM + 1])
    lane = lax.broadcasted_iota(jnp.int32, (tq, LANES), 1)
    o_ref[0] = jnp.where(lane < VDIM, outs[0], pltpu.roll(outs[1], VDIM, axis=1)).astype(BF16)


def _mla_attention(q, k, v, tq=1024, tk=512, unroll=4):
    b, s, _ = q.shape
    npair = MLA_HEADS // 2
    assert unroll % 2 == 0 and (s // tk) % unroll == 0
    state = pltpu.VMEM((2, tq, LANES), F32)
    sbuf = pltpu.VMEM((2, tq, tk), F32)
    pair = lambda bi, p, i: (bi, 0, p)
    return pl.pallas_call(
        functools.partial(_mla_kernel, tq=tq, tk=tk, seq=s, unroll=unroll),
        grid=(b, npair, s // tq),
        in_specs=[pl.BlockSpec((1, tq, 2 * LANES), lambda bi, p, i: (bi, i, p)),
                  pl.BlockSpec((1, s, 2 * LANES), pair),
                  pl.BlockSpec((1, s, 2 * LANES), pair)],
        out_specs=pl.BlockSpec((1, tq, LANES), lambda bi, p, i: (bi, i, p)),
        out_shape=jax.ShapeDtypeStruct((b, s, MLA_HEADS * VDIM), BF16),
        scratch_shapes=[state, state, sbuf, sbuf, state, state, state, state],
        compiler_params=_params(("parallel", "parallel", "parallel")),
        name="mla_attention",
    )(q, k, v)


def _log_sigmoid(x):
    return jnp.minimum(x, 0.0) - jnp.log1p(jnp.exp(-jnp.abs(x)))


def _retention_kernel(q_ref, k_ref, v_ref, g_ref, dec_ref, o_ref, acc_lo_ref, acc_hi_ref,
                      stf_ref, stb_ref, dmf_ref, dmb_ref, tab_ref, ones_ref, *, ch, seq, unroll):
    nc = seq // ch
    half = nc // 2
    nh = QUAD // RET_DIM
    lane = lax.broadcasted_iota(jnp.int32, (ch, QUAD), 1)
    row = lax.broadcasted_iota(jnp.int32, (ch, QUAD), 0).astype(F32)
    khead = (lane >> 5) & (nh - 1)
    vhead = lane >> 6
    ri = lax.broadcasted_iota(jnp.int32, (ch, ch), 0)
    ci = lax.broadcasted_iota(jnp.int32, (ch, ch), 1)
    sd = lax.broadcasted_iota(jnp.int32, (QUAD, QUAD), 0)
    se = lax.broadcasted_iota(jnp.int32, (QUAD, QUAD), 1)
    state_mask = ((sd >> 5) & (nh - 1)) == (se >> 6)
    ones_ref[...] = jnp.where((sd >> 6) == (se >> 6), 1.0, 0.0).astype(BF16)

    lg = _log_sigmoid(dec_ref[...])
    tab_ref[0] = jnp.exp(lg[0:1] * (row + 1.0))
    tab_ref[1] = jnp.exp(lg[1:2] * (ch - 1.0 - row))
    tab_ref[2] = jnp.exp(lg[2:3] * (ch - row))
    tab_ref[3] = jnp.exp(lg[3:4] * row)
    c_dec = (jnp.exp(lg[0:1] * float(ch)), jnp.exp(lg[2:3] * float(ch)))
    rel_f = jnp.maximum(ri - ci, 0).astype(F32)
    rel_b = jnp.maximum(ci - ri, 0).astype(F32)
    for h in range(nh):
        dmf_ref[h] = jnp.where(ri >= ci, jnp.exp(lg[0:1, h * RET_DIM:h * RET_DIM + 1] * rel_f), 0.0)
        dmb_ref[h] = jnp.where(ci > ri, jnp.exp(lg[2:3, h * RET_DIM:h * RET_DIM + 1] * rel_b), 0.0)
    stf_ref[...] = jnp.zeros_like(stf_ref)
    stb_ref[...] = jnp.zeros_like(stb_ref)

    def chunks(items):
        loaded = []
        for c, _ in items:
            off = pl.multiple_of(c * ch, ch)
            loaded.append((q_ref[0, pl.ds(off, ch), :], k_ref[0, pl.ds(off, ch), :],
                           v_ref[0, pl.ds(off, ch), :]))
        scores = []
        for qc, kc, _ in loaded:
            zero = jnp.zeros_like(qc)
            scores.append([lax.dot_general(jnp.where(khead == h, qc, zero), kc,
                                           (((1,), (1,)), ((), ())), preferred_element_type=F32)
                           for h in range(nh)])
        outs = []
        for (_, fwd), (_, _, vc), ss in zip(items, loaded, scores):
            dm_ref = dmf_ref if fwd else dmb_ref
            out = None
            for h in range(nh):
                ih = jnp.dot((ss[h] * dm_ref[h]).astype(BF16), vc, preferred_element_type=F32)
                out = ih if out is None else jnp.where(vhead == h, ih, out)
            outs.append(out)
        upds = []
        for (_, fwd), (_, kc, vc) in zip(items, loaded):
            kd = (kc.astype(F32) * tab_ref[1 if fwd else 3]).astype(BF16)
            upds.append(lax.dot_general(kd, vc, (((0,), (0,)), ((), ())),
                                        preferred_element_type=F32))
        for i, ((_, fwd), (qc, _, _)) in enumerate(zip(items, loaded)):
            st_ref = stf_ref if fwd else stb_ref
            state = st_ref[...]
            cross = jnp.dot(qc, state.astype(BF16), preferred_element_type=F32)
            outs[i] = outs[i] + cross * tab_ref[0 if fwd else 2]
            st_ref[...] = state * c_dec[0 if fwd else 1] + jnp.where(state_mask, upds[i], 0.0)
        return outs

    def head_sum(x):
        hi = x.astype(BF16)
        lo = (x - hi.astype(F32)).astype(BF16)
        return (jnp.dot(hi, ones_ref[...], preferred_element_type=F32)
                + jnp.dot(lo, ones_ref[...], preferred_element_type=F32))

    def finish(c, tot):
        off = pl.multiple_of(c * ch, ch)
        d = tot - head_sum(tot) * (1.0 / RET_DIM)
        var = head_sum(d * d) * (1.0 / RET_DIM)
        gate = g_ref[0, pl.ds(off, ch), :].astype(F32)
        o_ref[0, pl.ds(off, ch), :] = (d * lax.rsqrt(var + EPS) * gate).astype(BF16)

    def rows(ref, c):
        return ref.at[pl.ds(pl.multiple_of(c * ch, ch), ch), :]

    def pairs(t):
        items = []
        for j in range(unroll):
            cf = t * unroll + j
            items += [(cf, True), (nc - 1 - cf, False)]
        return items

    def first(t, carry):
        items = pairs(t)
        for (c, fwd), out in zip(items, chunks(items)):
            if fwd:
                rows(acc_lo_ref, c)[...] = out
            else:
                rows(acc_hi_ref, c - half)[...] = out
        return carry

    lax.fori_loop(0, half // unroll, first, 0)

    def second(t, carry):
        items = pairs(t)
        for (c, fwd), out in zip(items, chunks(items)):
            other = rows(acc_hi_ref, c - half) if fwd else rows(acc_lo_ref, c)
            finish(c, out + other[...])
        return carry

    lax.fori_loop(half // unroll, nc // unroll, second, 0)


def _retention(q, k, v, g, dec, ch=256, unroll=2):
    b, s, _ = q.shape
    nquad = D_RET // QUAD
    assert (s // ch // 2) % unroll == 0
    blk = pl.BlockSpec((1, s, QUAD), lambda bi, u: (bi, 0, u))
    return pl.pallas_call(
        functools.partial(_retention_kernel, ch=ch, seq=s, unroll=unroll),
        grid=(b, nquad),
        in_specs=[blk, blk, blk, blk, pl.BlockSpec((4, QUAD), lambda bi, u: (0, u))],
        out_specs=blk,
        out_shape=jax.ShapeDtypeStruct((b, s, D_RET), BF16),
        scratch_shapes=[pltpu.VMEM((s // 2, QUAD), F32), pltpu.VMEM((s // 2, QUAD), F32),
                        pltpu.VMEM((QUAD, QUAD), F32), pltpu.VMEM((QUAD, QUAD), F32),
                        pltpu.VMEM((QUAD // RET_DIM, ch, ch), F32),
                        pltpu.VMEM((QUAD // RET_DIM, ch, ch), F32),
                        pltpu.VMEM((4, ch, QUAD), F32), pltpu.VMEM((QUAD, QUAD), BF16)],
        compiler_params=_params(("parallel", "parallel")),
        name="retention",
    )(q, k, v, g, dec)


def _ret_qk_perm():
    new = np.arange(D_RET)
    quad, j = new // QUAD, new % QUAD
    head = 4 * quad + (j % LANES) // 32
    dim = (j // LANES) * 32 + j % 32
    return head * RET_DIM + dim


def _ret_lane_heads():
    lane = np.arange(D_RET)
    v_head = lane // RET_DIM
    k_head = 4 * (lane // QUAD) + ((lane % QUAD) % LANES) // 32
    return v_head, k_head


def _prep_w_in(w_in):
    c_q, c_kv, k_pe, r_q, r_k, r_v, r_g = jnp.split(
        w_in, np.cumsum([Q_LORA, KV_LORA, ROPE, D_RET, D_RET, D_RET])[:], axis=1)
    k = w_in.shape[0]
    z = lambda w: jnp.zeros((k, w), w_in.dtype)
    kpe = jnp.concatenate([z(64), k_pe, z(32)], axis=1)
    kpe_sw = jnp.concatenate([z(64), k_pe[:, 16:], k_pe[:, :16], z(32)], axis=1)
    perm = _ret_qk_perm()
    return jnp.concatenate([c_q, c_kv, kpe, kpe_sw, z(LANES), r_q[:, perm], r_k[:, perm],
                            r_v, r_g], axis=1).astype(BF16)


def _prep_w_uq(w_uq):
    k = w_uq.shape[0]
    w3 = w_uq.reshape(k, MLA_HEADS, NOPE + ROPE)
    nope, x1, x2 = w3[:, :, :NOPE], w3[:, :, NOPE:NOPE + 16], w3[:, :, NOPE + 16:]
    z32 = jnp.zeros((k, MLA_HEADS, 32), w_uq.dtype)
    wq = jnp.concatenate([nope, x1, x2, z32], axis=2).reshape(k, MLA_HEADS * LANES)
    wqs = jnp.concatenate([jnp.zeros_like(nope), x2, x1, z32], axis=2).reshape(k, MLA_HEADS * LANES)
    return jnp.concatenate([wq, wqs], axis=1).astype(BF16)


def _prep_w_ukv(w_ukv):
    k = w_ukv.shape[0]
    w3 = w_ukv.reshape(k, MLA_HEADS, NOPE + VDIM)
    kn = jnp.concatenate([w3[:, :, :NOPE], jnp.zeros((k, MLA_HEADS, 64), w_ukv.dtype)], axis=2)
    v = jnp.concatenate([w3[:, :, NOPE:], jnp.zeros((k, MLA_HEADS, 64), w_ukv.dtype)], axis=2)
    return jnp.concatenate([kn.reshape(k, MLA_HEADS * LANES), v.reshape(k, MLA_HEADS * LANES)],
                           axis=1).astype(BF16)


def _prep_decay(dec_f, dec_b):
    v_head, k_head = _ret_lane_heads()
    f, b = dec_f.astype(F32), dec_b.astype(F32)
    return jnp.stack([f[v_head], f[k_head], b[v_head], b[k_head]])


def kernel(x, positions, ffn1_norm, ffn1_w_gate, ffn1_w_up, ffn1_w_down, mix_norm, w_in, q_norm, w_uq, kv_norm, w_ukv, ret_decay_fwd, ret_decay_bwd, w_o, ffn2_norm, ffn2_w_gate, ffn2_w_up, ffn2_w_down, final_norm):
    b, s, d = x.shape
    n = b * s
    assert ffn1_norm.shape[0] == 1, "specialised to DEPTH == 1 (the final norm is fused into the layer)"
    pos_row = positions.astype(F32).reshape(1, n)

    h = _ffn1(x.reshape(n, d), ffn1_norm[0][None, :], ffn1_w_gate[0].astype(BF16),
              ffn1_w_up[0].astype(BF16), ffn1_w_down[0].astype(BF16))
    qm, km, vm, rq, rk, rv, rg = _proj(
        h, pos_row, mix_norm[0][None, :], _prep_w_in(w_in[0]), q_norm[0][None, :],
        _prep_w_uq(w_uq[0]), kv_norm[0][None, :], _prep_w_ukv(w_ukv[0]))
    a = _mla_attention(qm.reshape(b, s, -1), km.reshape(b, s, -1), vm.reshape(b, s, -1))
    r = _retention(rq.reshape(b, s, -1), rk.reshape(b, s, -1), rv.reshape(b, s, -1),
                   rg.reshape(b, s, -1), _prep_decay(ret_decay_fwd[0], ret_decay_bwd[0]))
    wo = w_o[0].astype(BF16)
    out = _out_ffn2(h, a.reshape(n, -1), r.reshape(n, -1), wo[:MLA_HEADS * VDIM],
                    wo[MLA_HEADS * VDIM:], ffn2_norm[0][None, :], ffn2_w_gate[0].astype(BF16),
                    ffn2_w_up[0].astype(BF16), ffn2_w_down[0].astype(BF16), final_norm[None, :])
    return out.reshape(b, s, d)
```

```python
import functools
import math

import numpy as np
import jax
import jax.numpy as jnp
from jax import lax
from jax.experimental import pallas as pl
from jax.experimental.pallas import tpu as pltpu

F32 = jnp.float32
BF16 = jnp.bfloat16

D_MODEL = 1024
D_FF = 2816
MLA_HEADS = 8
Q_LORA = 256
KV_LORA = 128
NOPE = 64
ROPE = 32
VDIM = 64
RET_HEADS = 8
RET_DIM = 64
D_RET = RET_HEADS * RET_DIM
ROPE_THETA = 10000.0
EPS = 1e-6
MLA_SCALE = (NOPE + ROPE) ** -0.5
LOG2E = math.log2(math.e)

LANES = 128
QUAD = 256
VMEM_LIMIT = 56 * 1024 * 1024

_OFF_CQ = 0
_OFF_CKV = 256
_OFF_RQ = 768
_OFF_RK = 1280
_OFF_RV = 1792
_OFF_RG = 2304
D_IN_AUG = 2816


def _params(sem):
    return pltpu.CompilerParams(dimension_semantics=sem, vmem_limit_bytes=VMEM_LIMIT)


def _const_spec(shape):
    nd = len(shape)
    return pl.BlockSpec(shape, lambda *_: (0,) * nd, pipeline_mode=pl.Buffered(1))


N_FREQ = 64


def _rope_constants():
    row = np.arange(N_FREQ)
    inv = np.where(row < 32, ROPE_THETA ** (-(row % 32) / 32.0),
                   np.where(row < 48, ROPE_THETA ** (-((row - 32) % 16) / 16.0), 0.0))
    pc = np.zeros((N_FREQ, 2 * LANES))
    ps = np.zeros((N_FREQ, 2 * LANES))
    for l in range(LANES):
        pc[l % 32, l] = 1.0
        ps[l % 32, l] = 1.0
    for l in range(NOPE, NOPE + ROPE):
        r = 32 + (l - NOPE) % 16
        pc[r, LANES + l] = 1.0
        ps[r, LANES + l] = -1.0 if l < NOPE + 16 else 1.0
    return jnp.asarray(inv[:, None], F32), jnp.asarray(pc, BF16), jnp.asarray(ps, BF16)


def _rope_tables(pos_row, inv_col, pc, ps):
    ang = inv_col * pos_row
    dn = (((0,), (0,)), ((), ()))

    def place(x, p):
        hi = x.astype(BF16)
        lo = (x - hi.astype(F32)).astype(BF16)
        return (lax.dot_general(hi, p, dn, preferred_element_type=F32)
                + lax.dot_general(lo, p, dn, preferred_element_type=F32))

    return place(jnp.cos(ang), pc), place(jnp.sin(ang), ps)


def _rms(x, g):
    return x * lax.rsqrt(jnp.mean(x * x, axis=-1, keepdims=True) + EPS) * g


def _swiglu_acc(n_bf, wg_ref, wu_ref, wd_ref, ck):
    acc = None
    for j in range(D_FF // ck):
        sl = slice(j * ck, (j + 1) * ck)
        g = jnp.dot(n_bf, wg_ref[:, sl].astype(BF16), preferred_element_type=F32)
        u = jnp.dot(n_bf, wu_ref[:, sl].astype(BF16), preferred_element_type=F32)
        hm = (g * jax.nn.sigmoid(g) * u).astype(BF16)
        d = jnp.dot(hm, wd_ref[sl, :].astype(BF16), preferred_element_type=F32)
        acc = d if acc is None else acc + d
    return acc


def _ffn1_kernel(x_ref, g_ref, wg_ref, wu_ref, wd_ref, o_ref, *, ck):
    x = x_ref[...]
    n_bf = _rms(x, g_ref[...]).astype(BF16)
    o_ref[...] = x + 0.5 * _swiglu_acc(n_bf, wg_ref, wu_ref, wd_ref, ck)


def _ffn1(x, g, wg, wu, wd, tm=512, ck=256):
    n = x.shape[0]
    row = pl.BlockSpec((tm, D_MODEL), lambda i: (i, 0))
    return pl.pallas_call(
        functools.partial(_ffn1_kernel, ck=ck),
        grid=(n // tm,),
        in_specs=[row, _const_spec((1, D_MODEL)), _const_spec((D_MODEL, D_FF)),
                  _const_spec((D_MODEL, D_FF)), _const_spec((D_FF, D_MODEL))],
        out_specs=row,
        out_shape=jax.ShapeDtypeStruct((n, D_MODEL), F32),
        compiler_params=_params(("parallel",)),
        name="ffn1",
    )(x, g, wg, wu, wd)


def _out_ffn2_kernel(h_ref, a_ref, r_ref, wo_ref, g_ref, wg_ref, wu_ref, wd_ref, gf_ref, o_ref,
                     *, ck):
    h = (h_ref[...]
         + jnp.dot(a_ref[...], wo_ref[:MLA_HEADS * VDIM, :], preferred_element_type=F32)
         + jnp.dot(r_ref[...], wo_ref[MLA_HEADS * VDIM:, :], preferred_element_type=F32))
    n_bf = _rms(h, g_ref[...]).astype(BF16)
    h = h + 0.5 * _swiglu_acc(n_bf, wg_ref, wu_ref, wd_ref, ck)
    o_ref[...] = _rms(h, gf_ref[...])


def _out_ffn2(h, a, r, wo, g, wg, wu, wd, gf, tm=512, ck=256):
    n = h.shape[0]
    row = pl.BlockSpec((tm, D_MODEL), lambda i: (i, 0))
    half = pl.BlockSpec((tm, D_RET), lambda i: (i, 0))
    return pl.pallas_call(
        functools.partial(_out_ffn2_kernel, ck=ck),
        grid=(n // tm,),
        in_specs=[row, half, half, _const_spec((MLA_HEADS * VDIM + D_RET, D_MODEL)),
                  _const_spec((1, D_MODEL)), _const_spec((D_MODEL, D_FF)),
                  _const_spec((D_MODEL, D_FF)), _const_spec((D_FF, D_MODEL)),
                  _const_spec((1, D_MODEL))],
        out_specs=row,
        out_shape=jax.ShapeDtypeStruct((n, D_MODEL), F32),
        compiler_params=_params(("parallel",)),
        name="out_ffn2",
    )(h, a, r, wo, g, wg, wu, wd, gf)


def _proj_kernel(h_ref, pos_ref, inv_ref, pc_ref, ps_ref, gm_ref, win_ref, gq_ref, wq_ref, gkv_ref,
                 wkv_ref, qm_ref, km_ref, vm_ref, rq_ref, rk_ref, rv_ref, rg_ref):
    n_bf = _rms(h_ref[...], gm_ref[...]).astype(BF16)

    def proj(off, width):
        return jnp.dot(n_bf, win_ref[:, off:off + width], preferred_element_type=F32)

    def ret_rope(x, out_ref):
        for u in range(D_RET // QUAD):
            x1 = x[:, u * QUAD:u * QUAD + LANES]
            x2 = x[:, u * QUAD + LANES:(u + 1) * QUAD]
            out_ref[:, u * QUAD:u * QUAD + LANES] = (x1 * cr - x2 * sr).astype(BF16)
            out_ref[:, u * QUAD + LANES:(u + 1) * QUAD] = (x1 * sr + x2 * cr).astype(BF16)

    cq_bf = _rms(proj(_OFF_CQ, Q_LORA), gq_ref[...]).astype(BF16)
    lat = proj(_OFF_CKV, 4 * LANES)
    ckv_bf = _rms(lat[:, :KV_LORA], gkv_ref[...]).astype(BF16)
    xq = proj(_OFF_RQ, D_RET)

    cos_t, sin_t = _rope_tables(pos_ref[...], inv_ref[...], pc_ref[...], ps_ref[...])
    cr, sr = cos_t[:, :LANES], sin_t[:, :LANES]
    nope_lane = lax.broadcasted_iota(jnp.int32, (1, LANES), 1) < NOPE
    cm, sm = cos_t[:, LANES:] + jnp.where(nope_lane, 1.0, 0.0), sin_t[:, LANES:]

    xk = proj(_OFF_RK, D_RET)
    ret_rope(xq, rq_ref)
    rv_ref[...] = proj(_OFF_RV, D_RET).astype(BF16)
    ret_rope(xk, rk_ref)
    g = proj(_OFF_RG, D_RET)
    rg_ref[...] = (g * jax.nn.sigmoid(g)).astype(BF16)

    qq = jnp.dot(cq_bf, wq_ref[...], preferred_element_type=F32)
    kv = jnp.dot(ckv_bf, wkv_ref[...], preferred_element_type=F32)
    nh = MLA_HEADS * LANES
    kpe = lat[:, LANES:2 * LANES] * cm + lat[:, 2 * LANES:3 * LANES] * sm
    cq_s, sq_s = cm * (MLA_SCALE * LOG2E), sm * (MLA_SCALE * LOG2E)
    for h in range(MLA_HEADS):
        sl = slice(h * LANES, (h + 1) * LANES)
        sw = slice(nh + h * LANES, nh + (h + 1) * LANES)
        qm_ref[:, sl] = (qq[:, sl] * cq_s + qq[:, sw] * sq_s).astype(BF16)
        km_ref[:, sl] = (kv[:, sl] + kpe).astype(BF16)
    lane = lax.broadcasted_iota(jnp.int32, (1, nh), 1)
    ones_lane = jnp.where(lane % LANES == VDIM, 1.0, 0.0)
    vm_ref[...] = (kv[:, nh:] + ones_lane).astype(BF16)


def _proj(h, pos_row, gm, win, gq, wq, gkv, wkv, tm=512):
    n = h.shape[0]
    row = lambda w: pl.BlockSpec((tm, w), lambda i: (i, 0))
    sds = lambda w: jax.ShapeDtypeStruct((n, w), BF16)
    inv_col, pc, ps = _rope_constants()
    return pl.pallas_call(
        _proj_kernel,
        grid=(n // tm,),
        in_specs=[row(D_MODEL), pl.BlockSpec((1, tm), lambda i: (0, i)),
                  _const_spec((N_FREQ, 1)), _const_spec((N_FREQ, 2 * LANES)),
                  _const_spec((N_FREQ, 2 * LANES)), _const_spec((1, D_MODEL)),
                  _const_spec((D_MODEL, D_IN_AUG)), _const_spec((1, Q_LORA)),
                  _const_spec((Q_LORA, 2 * MLA_HEADS * LANES)),
                  _const_spec((1, KV_LORA)), _const_spec((KV_LORA, 2 * MLA_HEADS * LANES))],
        out_specs=[row(1024), row(1024), row(1024), row(512), row(512), row(512), row(512)],
        out_shape=[sds(1024), sds(1024), sds(1024), sds(512), sds(512), sds(512), sds(512)],
        compiler_params=_params(("parallel",)),
        name="mixer_proj",
    )(h, pos_row, inv_col, pc, ps, gm, win, gq, wq, gkv, wkv)


def _mla_kernel(q_ref, k_ref, v_ref, o_ref, m_ref, acc_ref, s0_ref, s1_ref, mb0_ref, mb1_ref,
                al0_ref, al1_ref, *, tq, tk, seq, unroll):
    nk = seq // tk
    nrep = tk // LANES
    slots = ((s0_ref, mb0_ref, al0_ref), (s1_ref, mb1_ref, al1_ref))
    m_ref[...] = jnp.full(m_ref.shape, -jnp.inf, F32)
    acc_ref[...] = jnp.zeros(acc_ref.shape, F32)

    def scores(c, slot):
        s_ref, mb_ref, al_ref = slots[slot]
        off = c * tk if isinstance(c, int) else pl.multiple_of(c * tk, tk)
        for h in range(2):
            q = q_ref[0, :, h * LANES:(h + 1) * LANES]
            k = k_ref[0, pl.ds(off, tk), h * LANES:(h + 1) * LANES]
            s = lax.dot_general(q, k, (((1,), (1,)), ((), ())), preferred_element_type=F32)
            s_ref[h] = s
            m_old = m_ref[h]
            m_new = jnp.maximum(m_old, jnp.max(s, axis=-1, keepdims=True))
            m_ref[h] = m_new
            mb_ref[h] = m_new
            al_ref[h] = jnp.exp2(m_old - m_new)

    def weighted_values(c, slot):
        s_ref, mb_ref, al_ref = slots[slot]
        off = c * tk if isinstance(c, int) else pl.multiple_of(c * tk, tk)
        for h in range(2):
            mb = mb_ref[h]
            p = jnp.exp2(s_ref[h] - jnp.concatenate([mb] * nrep, axis=1))
            v = v_ref[0, pl.ds(off, tk), h * LANES:(h + 1) * LANES]
            acc_ref[h] = al_ref[h] * acc_ref[h] + jnp.dot(p.astype(BF16), v,
                                                          preferred_element_type=F32)

    def chunk_group(c0, last):
        for j in range(unroll):
            if not (last and j == unroll - 1):
                scores(c0 + j + 1, (j + 1) % 2)
            weighted_values(c0 + j, j % 2)

    scores(0, 0)

    def body(i, carry):
        chunk_group(i * unroll, False)
        return carry

    lax.fori_loop(0, nk // unroll - 1, body, 0)
    chunk_group(nk - unroll, True)

    outs = []
    for h in range(2):
        acc = acc_ref[h]
        outs.append(acc / acc[:, VDIM:VDIM + 1])
    lane = lax.broadcasted_iota(jnp.int32, (tq, LANES), 1)
    o_ref[0] = jnp.where(lane < VDIM, outs[0], pltpu.roll(outs[1], VDIM, axis=1)).astype(BF16)


def _mla_attention(q, k, v, tq=1024, tk=512, unroll=4):
    b, s, _ = q.shape
    npair = MLA_HEADS // 2
    assert unroll % 2 == 0 and (s // tk) % unroll == 0
    state = pltpu.VMEM((2, tq, LANES), F32)
    sbuf = pltpu.VMEM((2, tq, tk), F32)
    pair = lambda bi, p, i: (bi, 0, p)
    return pl.pallas_call(
        functools.partial(_mla_kernel, tq=tq, tk=tk, seq=s, unroll=unroll),
        grid=(b, npair, s // tq),
        in_specs=[pl.BlockSpec((1, tq, 2 * LANES), lambda bi, p, i: (bi, i, p)),
                  pl.BlockSpec((1, s, 2 * LANES), pair),
                  pl.BlockSpec((1, s, 2 * LANES), pair)],
        out_specs=pl.BlockSpec((1, tq, LANES), lambda bi, p, i: (bi, i, p)),
        out_shape=jax.ShapeDtypeStruct((b, s, MLA_HEADS * VDIM), BF16),
        scratch_shapes=[state, state, sbuf, sbuf, state, state, state, state],
        compiler_params=_params(("parallel", "parallel", "parallel")),
        name="mla_attention",
    )(q, k, v)


def _log_sigmoid(x):
    return jnp.minimum(x, 0.0) - jnp.log1p(jnp.exp(-jnp.abs(x)))


def _retention_kernel(q_ref, k_ref, v_ref, g_ref, dec_ref, o_ref, acc_lo_ref, acc_hi_ref,
                      stf_ref, stb_ref, dmf_ref, dmb_ref, tab_ref, ones_ref, *, ch, seq, unroll):
    nc = seq // ch
    half = nc // 2
    nh = QUAD // RET_DIM
    lane = lax.broadcasted_iota(jnp.int32, (ch, QUAD), 1)
    row = lax.broadcasted_iota(jnp.int32, (ch, QUAD), 0).astype(F32)
    khead = (lane >> 5) & (nh - 1)
    vhead = lane >> 6
    ri = lax.broadcasted_iota(jnp.int32, (ch, ch), 0)
    ci = lax.broadcasted_iota(jnp.int32, (ch, ch), 1)
    sd = lax.broadcasted_iota(jnp.int32, (QUAD, QUAD), 0)
    se = lax.broadcasted_iota(jnp.int32, (QUAD, QUAD), 1)
    state_mask = ((sd >> 5) & (nh - 1)) == (se >> 6)
    ones_ref[...] = jnp.where((sd >> 6) == (se >> 6), 1.0, 0.0).astype(BF16)

    lg = _log_sigmoid(dec_ref[...])
    tab_ref[0] = jnp.exp(lg[0:1] * (row + 1.0))
    tab_ref[1] = jnp.exp(lg[1:2] * (ch - 1.0 - row))
    tab_ref[2] = jnp.exp(lg[2:3] * (ch - row))
    tab_ref[3] = jnp.exp(lg[3:4] * row)
    c_dec = (jnp.exp(lg[0:1] * float(ch)), jnp.exp(lg[2:3] * float(ch)))
    rel_f = jnp.maximum(ri - ci, 0).astype(F32)
    rel_b = jnp.maximum(ci - ri, 0).astype(F32)
    for h in range(nh):
        dmf_ref[h] = jnp.where(ri >= ci, jnp.exp(lg[0:1, h * RET_DIM:h * RET_DIM + 1] * rel_f), 0.0)
        dmb_ref[h] = jnp.where(ci > ri, jnp.exp(lg[2:3, h * RET_DIM:h * RET_DIM + 1] * rel_b), 0.0)
    stf_ref[...] = jnp.zeros_like(stf_ref)
    stb_ref[...] = jnp.zeros_like(stb_ref)

    def chunks(items):
        loaded = []
        for c, _ in items:
            off = pl.multiple_of(c * ch, ch)
            loaded.append((q_ref[0, pl.ds(off, ch), :], k_ref[0, pl.ds(off, ch), :],
                           v_ref[0, pl.ds(off, ch), :]))
        scores = []
        for qc, kc, _ in loaded:
            zero = jnp.zeros_like(qc)
            scores.append([lax.dot_general(jnp.where(khead == h, qc, zero), kc,
                                           (((1,), (1,)), ((), ())), preferred_element_type=F32)
                           for h in range(nh)])
        outs = []
        for (_, fwd), (_, _, vc), ss in zip(items, loaded, scores):
            dm_ref = dmf_ref if fwd else dmb_ref
            out = None
            for h in range(nh):
                ih = jnp.dot((ss[h] * dm_ref[h]).astype(BF16), vc, preferred_element_type=F32)
                out = ih if out is None else jnp.where(vhead == h, ih, out)
            outs.append(out)
        upds = []
        for (_, fwd), (_, kc, vc) in zip(items, loaded):
            kd = (kc.astype(F32) * tab_ref[1 if fwd else 3]).astype(BF16)
            upds.append(lax.dot_general(kd, vc, (((0,), (0,)), ((), ())),
                                        preferred_element_type=F32))
        for i, ((_, fwd), (qc, _, _)) in enumerate(zip(items, loaded)):
            st_ref = stf_ref if fwd else stb_ref
            state = st_ref[...]
            cross = jnp.dot(qc, state.astype(BF16), preferred_element_type=F32)
            outs[i] = outs[i] + cross * tab_ref[0 if fwd else 2]
            st_ref[...] = state * c_dec[0 if fwd else 1] + jnp.where(state_mask, upds[i], 0.0)
        return outs

    def head_sum(x):
        hi = x.astype(BF16)
        lo = (x - hi.astype(F32)).astype(BF16)
        return (jnp.dot(hi, ones_ref[...], preferred_element_type=F32)
                + jnp.dot(lo, ones_ref[...], preferred_element_type=F32))

    def finish(c, tot):
        off = pl.multiple_of(c * ch, ch)
        d = tot - head_sum(tot) * (1.0 / RET_DIM)
        var = head_sum(d * d) * (1.0 / RET_DIM)
        gate = g_ref[0, pl.ds(off, ch), :].astype(F32)
        o_ref[0, pl.ds(off, ch), :] = (d * lax.rsqrt(var + EPS) * gate).astype(BF16)

    def rows(ref, c):
        return ref.at[pl.ds(pl.multiple_of(c * ch, ch), ch), :]

    def pairs(t):
        items = []
        for j in range(unroll):
            cf = t * unroll + j
            items += [(cf, True), (nc - 1 - cf, False)]
        return items

    def first(t, carry):
        items = pairs(t)
        for (c, fwd), out in zip(items, chunks(items)):
            if fwd:
                rows(acc_lo_ref, c)[...] = out
            else:
                rows(acc_hi_ref, c - half)[...] = out
        return carry

    lax.fori_loop(0, half // unroll, first, 0)

    def second(t, carry):
        items = pairs(t)
        for (c, fwd), out in zip(items, chunks(items)):
            other = rows(acc_hi_ref, c - half) if fwd else rows(acc_lo_ref, c)
            finish(c, out + other[...])
        return carry

    lax.fori_loop(half // unroll, nc // unroll, second, 0)


def _retention(q, k, v, g, dec, ch=256, unroll=2):
    b, s, _ = q.shape
    nquad = D_RET // QUAD
    assert (s // ch // 2) % unroll == 0
    blk = pl.BlockSpec((1, s, QUAD), lambda bi, u: (bi, 0, u))
    return pl.pallas_call(
        functools.partial(_retention_kernel, ch=ch, seq=s, unroll=unroll),
        grid=(b, nquad),
        in_specs=[blk, blk, blk, blk, pl.BlockSpec((4, QUAD), lambda bi, u: (0, u))],
        out_specs=blk,
        out_shape=jax.ShapeDtypeStruct((b, s, D_RET), BF16),
        scratch_shapes=[pltpu.VMEM((s // 2, QUAD), F32), pltpu.VMEM((s // 2, QUAD), F32),
                        pltpu.VMEM((QUAD, QUAD), F32), pltpu.VMEM((QUAD, QUAD), F32),
                        pltpu.VMEM((QUAD // RET_DIM, ch, ch), F32),
                        pltpu.VMEM((QUAD // RET_DIM, ch, ch), F32),
                        pltpu.VMEM((4, ch, QUAD), F32), pltpu.VMEM((QUAD, QUAD), BF16)],
        compiler_params=_params(("parallel", "parallel")),
        name="retention",
    )(q, k, v, g, dec)


def _ret_qk_perm():
    new = np.arange(D_RET)
    quad, j = new // QUAD, new % QUAD
    head = 4 * quad + (j % LANES) // 32
    dim = (j // LANES) * 32 + j % 32
    return head * RET_DIM + dim


def _ret_lane_heads():
    lane = np.arange(D_RET)
    v_head = lane // RET_DIM
    k_head = 4 * (lane // QUAD) + ((lane % QUAD) % LANES) // 32
    return v_head, k_head


def _prep_w_in(w_in):
    c_q, c_kv, k_pe, r_q, r_k, r_v, r_g = jnp.split(
        w_in, np.cumsum([Q_LORA, KV_LORA, ROPE, D_RET, D_RET, D_RET])[:], axis=1)
    k = w_in.shape[0]
    z = lambda w: jnp.zeros((k, w), w_in.dtype)
    kpe = jnp.concatenate([z(64), k_pe, z(32)], axis=1)
    kpe_sw = jnp.concatenate([z(64), k_pe[:, 16:], k_pe[:, :16], z(32)], axis=1)
    perm = _ret_qk_perm()
    return jnp.concatenate([c_q, c_kv, kpe, kpe_sw, z(LANES), r_q[:, perm],
                            r_k[:, perm] * (RET_DIM ** -0.5), r_v, r_g], axis=1).astype(BF16)


def _prep_w_uq(w_uq):
    k = w_uq.shape[0]
    w3 = w_uq.reshape(k, MLA_HEADS, NOPE + ROPE)
    nope, x1, x2 = w3[:, :, :NOPE], w3[:, :, NOPE:NOPE + 16], w3[:, :, NOPE + 16:]
    z32 = jnp.zeros((k, MLA_HEADS, 32), w_uq.dtype)
    wq = jnp.concatenate([nope, x1, x2, z32], axis=2).reshape(k, MLA_HEADS * LANES)
    wqs = jnp.concatenate([jnp.zeros_like(nope), x2, x1, z32], axis=2).reshape(k, MLA_HEADS * LANES)
    return jnp.concatenate([wq, wqs], axis=1).astype(BF16)


def _prep_w_ukv(w_ukv):
    k = w_ukv.shape[0]
    w3 = w_ukv.reshape(k, MLA_HEADS, NOPE + VDIM)
    kn = jnp.concatenate([w3[:, :, :NOPE], jnp.zeros((k, MLA_HEADS, 64), w_ukv.dtype)], axis=2)
    v = jnp.concatenate([w3[:, :, NOPE:], jnp.zeros((k, MLA_HEADS, 64), w_ukv.dtype)], axis=2)
    return jnp.concatenate([kn.reshape(k, MLA_HEADS * LANES), v.reshape(k, MLA_HEADS * LANES)],
                           axis=1).astype(BF16)


def _prep_decay(dec_f, dec_b):
    v_head, k_head = _ret_lane_heads()
    f, b = dec_f.astype(F32), dec_b.astype(F32)
    return jnp.stack([f[v_head], f[k_head], b[v_head], b[k_head]])


def kernel(x, positions, ffn1_norm, ffn1_w_gate, ffn1_w_up, ffn1_w_down, mix_norm, w_in, q_norm, w_uq, kv_norm, w_ukv, ret_decay_fwd, ret_decay_bwd, w_o, ffn2_norm, ffn2_w_gate, ffn2_w_up, ffn2_w_down, final_norm):
    b, s, d = x.shape
    n = b * s
    assert ffn1_norm.shape[0] == 1, "specialised to DEPTH == 1 (the final norm is fused into the layer)"
    pos_row = positions.astype(F32).reshape(1, n)

    h = _ffn1(x.reshape(n, d), ffn1_norm[0][None, :], ffn1_w_gate[0], ffn1_w_up[0],
              ffn1_w_down[0])
    qm, km, vm, rq, rk, rv, rg = _proj(
        h, pos_row, mix_norm[0][None, :], _prep_w_in(w_in[0]), q_norm[0][None, :],
        _prep_w_uq(w_uq[0]), kv_norm[0][None, :], _prep_w_ukv(w_ukv[0]))
    a = _mla_attention(qm.reshape(b, s, -1), km.reshape(b, s, -1), vm.reshape(b, s, -1))
    r = _retention(rq.reshape(b, s, -1), rk.reshape(b, s, -1), rv.reshape(b, s, -1),
                   rg.reshape(b, s, -1), _prep_decay(ret_decay_fwd[0], ret_decay_bwd[0]))
    wo = w_o[0].astype(BF16)
    out = _out_ffn2(h, a.reshape(n, -1), r.reshape(n, -1), wo, ffn2_norm[0][None, :],
                    ffn2_w_gate[0], ffn2_w_up[0], ffn2_w_down[0], final_norm[None, :])
    return out.reshape(b, s, d)
```

```python
import functools
import math

import numpy as np
import jax
import jax.numpy as jnp
from jax import lax
from jax.experimental import pallas as pl
from jax.experimental.pallas import tpu as pltpu

F32 = jnp.float32
BF16 = jnp.bfloat16

D_MODEL = 1024
D_FF = 2816
MLA_HEADS = 8
Q_LORA = 256
KV_LORA = 128
NOPE = 64
ROPE = 32
VDIM = 64
RET_HEADS = 8
RET_DIM = 64
D_RET = RET_HEADS * RET_DIM
ROPE_THETA = 10000.0
EPS = 1e-6
MLA_SCALE = (NOPE + ROPE) ** -0.5
LOG2E = math.log2(math.e)

LANES = 128
QUAD = 256
VMEM_LIMIT = 56 * 1024 * 1024

_OFF_CQ = 0
_OFF_CKV = 256
_OFF_RQ = 768
_OFF_RK = 1280
_OFF_RV = 1792
_OFF_RG = 2304
D_IN_AUG = 2816


def _params(sem):
    return pltpu.CompilerParams(dimension_semantics=sem, vmem_limit_bytes=VMEM_LIMIT)


def _const_spec(shape):
    nd = len(shape)
    return pl.BlockSpec(shape, lambda *_: (0,) * nd, pipeline_mode=pl.Buffered(1))


N_FREQ = 64


def _rope_constants():
    row = np.arange(N_FREQ)
    inv = np.where(row < 32, ROPE_THETA ** (-(row % 32) / 32.0),
                   np.where(row < 48, ROPE_THETA ** (-((row - 32) % 16) / 16.0), 0.0))
    pc = np.zeros((N_FREQ, 2 * LANES))
    ps = np.zeros((N_FREQ, 2 * LANES))
    for l in range(LANES):
        pc[l % 32, l] = 1.0
        ps[l % 32, l] = 1.0
    for l in range(NOPE, NOPE + ROPE):
        r = 32 + (l - NOPE) % 16
        pc[r, LANES + l] = 1.0
        ps[r, LANES + l] = -1.0 if l < NOPE + 16 else 1.0
    return jnp.asarray(inv[:, None], F32), jnp.asarray(pc, BF16), jnp.asarray(ps, BF16)


def _rope_tables(pos_row, inv_col, pc, ps):
    ang = inv_col * pos_row
    dn = (((0,), (0,)), ((), ()))

    def place(x, p):
        hi = x.astype(BF16)
        lo = (x - hi.astype(F32)).astype(BF16)
        return (lax.dot_general(hi, p, dn, preferred_element_type=F32)
                + lax.dot_general(lo, p, dn, preferred_element_type=F32))

    return place(jnp.cos(ang), pc), place(jnp.sin(ang), ps)


def _rms(x, g):
    return x * lax.rsqrt(jnp.mean(x * x, axis=-1, keepdims=True) + EPS) * g


def _swiglu_acc(n_bf, wg_ref, wu_ref, wd_ref, ck):
    acc = None
    for j in range(D_FF // ck):
        sl = slice(j * ck, (j + 1) * ck)
        g = jnp.dot(n_bf, wg_ref[:, sl].astype(BF16), preferred_element_type=F32)
        u = jnp.dot(n_bf, wu_ref[:, sl].astype(BF16), preferred_element_type=F32)
        hm = (g * jax.nn.sigmoid(g) * u).astype(BF16)
        d = jnp.dot(hm, wd_ref[sl, :].astype(BF16), preferred_element_type=F32)
        acc = d if acc is None else acc + d
    return acc


def _ffn1_kernel(x_ref, g_ref, wg_ref, wu_ref, wd_ref, o_ref, *, ck):
    x = x_ref[...]
    n_bf = _rms(x, g_ref[...]).astype(BF16)
    o_ref[...] = x + 0.5 * _swiglu_acc(n_bf, wg_ref, wu_ref, wd_ref, ck)


def _ffn1(x, g, wg, wu, wd, tm=512, ck=256):
    n = x.shape[0]
    row = pl.BlockSpec((tm, D_MODEL), lambda i: (i, 0))
    return pl.pallas_call(
        functools.partial(_ffn1_kernel, ck=ck),
        grid=(n // tm,),
        in_specs=[row, _const_spec((1, D_MODEL)), _const_spec((D_MODEL, D_FF)),
                  _const_spec((D_MODEL, D_FF)), _const_spec((D_FF, D_MODEL))],
        out_specs=row,
        out_shape=jax.ShapeDtypeStruct((n, D_MODEL), F32),
        compiler_params=_params(("parallel",)),
        name="ffn1",
    )(x, g, wg, wu, wd)


def _out_ffn2_kernel(h_ref, a_ref, r_ref, wo_ref, g_ref, wg_ref, wu_ref, wd_ref, gf_ref, o_ref,
                     *, ck):
    h = (h_ref[...]
         + jnp.dot(a_ref[...], wo_ref[:MLA_HEADS * VDIM, :], preferred_element_type=F32)
         + jnp.dot(r_ref[...], wo_ref[MLA_HEADS * VDIM:, :], preferred_element_type=F32))
    n_bf = _rms(h, g_ref[...]).astype(BF16)
    h = h + 0.5 * _swiglu_acc(n_bf, wg_ref, wu_ref, wd_ref, ck)
    o_ref[...] = _rms(h, gf_ref[...])


def _out_ffn2(h, a, r, wo, g, wg, wu, wd, gf, tm=512, ck=256):
    n = h.shape[0]
    row = pl.BlockSpec((tm, D_MODEL), lambda i: (i, 0))
    half = pl.BlockSpec((tm, D_RET), lambda i: (i, 0))
    return pl.pallas_call(
        functools.partial(_out_ffn2_kernel, ck=ck),
        grid=(n // tm,),
        in_specs=[row, half, half, _const_spec((MLA_HEADS * VDIM + D_RET, D_MODEL)),
                  _const_spec((1, D_MODEL)), _const_spec((D_MODEL, D_FF)),
                  _const_spec((D_MODEL, D_FF)), _const_spec((D_FF, D_MODEL)),
                  _const_spec((1, D_MODEL))],
        out_specs=row,
        out_shape=jax.ShapeDtypeStruct((n, D_MODEL), F32),
        compiler_params=_params(("parallel",)),
        name="out_ffn2",
    )(h, a, r, wo, g, wg, wu, wd, gf)


def _proj_kernel(h_ref, pos_ref, inv_ref, pc_ref, ps_ref, gm_ref, win_ref, gq_ref, wq_ref, gkv_ref,
                 wkv_ref, qm_ref, km_ref, vm_ref, rq_ref, rk_ref, rv_ref, rg_ref):
    n_bf = _rms(h_ref[...], gm_ref[...]).astype(BF16)

    def proj(off, width):
        return jnp.dot(n_bf, win_ref[:, off:off + width], preferred_element_type=F32)

    def ret_rope(x, out_ref):
        for u in range(D_RET // QUAD):
            x1 = x[:, u * QUAD:u * QUAD + LANES]
            x2 = x[:, u * QUAD + LANES:(u + 1) * QUAD]
            out_ref[:, u * QUAD:u * QUAD + LANES] = (x1 * cr - x2 * sr).astype(BF16)
            out_ref[:, u * QUAD + LANES:(u + 1) * QUAD] = (x1 * sr + x2 * cr).astype(BF16)

    cq_bf = _rms(proj(_OFF_CQ, Q_LORA), gq_ref[...]).astype(BF16)
    lat = proj(_OFF_CKV, 4 * LANES)
    ckv_bf = _rms(lat[:, :KV_LORA], gkv_ref[...]).astype(BF16)
    xq = proj(_OFF_RQ, D_RET)

    cos_t, sin_t = _rope_tables(pos_ref[...], inv_ref[...], pc_ref[...], ps_ref[...])
    cr, sr = cos_t[:, :LANES], sin_t[:, :LANES]
    nope_lane = lax.broadcasted_iota(jnp.int32, (1, LANES), 1) < NOPE
    cm, sm = cos_t[:, LANES:] + jnp.where(nope_lane, 1.0, 0.0), sin_t[:, LANES:]

    xk = proj(_OFF_RK, D_RET)
    ret_rope(xq, rq_ref)
    rv_ref[...] = proj(_OFF_RV, D_RET).astype(BF16)
    ret_rope(xk, rk_ref)
    g = proj(_OFF_RG, D_RET)
    rg_ref[...] = (g * jax.nn.sigmoid(g)).astype(BF16)

    qq = jnp.dot(cq_bf, wq_ref[...], preferred_element_type=F32)
    kv = jnp.dot(ckv_bf, wkv_ref[...], preferred_element_type=F32)
    nh = MLA_HEADS * LANES
    kpe = lat[:, LANES:2 * LANES] * cm + lat[:, 2 * LANES:3 * LANES] * sm
    cq_s, sq_s = cm * (MLA_SCALE * LOG2E), sm * (MLA_SCALE * LOG2E)
    for h in range(MLA_HEADS):
        sl = slice(h * LANES, (h + 1) * LANES)
        sw = slice(nh + h * LANES, nh + (h + 1) * LANES)
        qm_ref[sl, :] = (qq[:, sl] * cq_s + qq[:, sw] * sq_s).astype(BF16).T
        km_ref[:, sl] = (kv[:, sl] + kpe).astype(BF16)
    lane = lax.broadcasted_iota(jnp.int32, (1, nh), 1)
    ones_lane = jnp.where(lane % LANES == VDIM, 1.0, 0.0)
    vm_ref[...] = (kv[:, nh:] + ones_lane).astype(BF16).T


def _proj(h, pos_row, gm, win, gq, wq, gkv, wkv, tm=512):
    n = h.shape[0]
    row = lambda w: pl.BlockSpec((tm, w), lambda i: (i, 0))
    sds = lambda w: jax.ShapeDtypeStruct((n, w), BF16)
    col = pl.BlockSpec((MLA_HEADS * LANES, tm), lambda i: (0, i))
    sds_t = jax.ShapeDtypeStruct((MLA_HEADS * LANES, n), BF16)
    inv_col, pc, ps = _rope_constants()
    return pl.pallas_call(
        _proj_kernel,
        grid=(n // tm,),
        in_specs=[row(D_MODEL), pl.BlockSpec((1, tm), lambda i: (0, i)),
                  _const_spec((N_FREQ, 1)), _const_spec((N_FREQ, 2 * LANES)),
                  _const_spec((N_FREQ, 2 * LANES)), _const_spec((1, D_MODEL)),
                  _const_spec((D_MODEL, D_IN_AUG)), _const_spec((1, Q_LORA)),
                  _const_spec((Q_LORA, 2 * MLA_HEADS * LANES)),
                  _const_spec((1, KV_LORA)), _const_spec((KV_LORA, 2 * MLA_HEADS * LANES))],
        out_specs=[col, row(1024), col, row(512), row(512), row(512), row(512)],
        out_shape=[sds_t, sds(1024), sds_t, sds(512), sds(512), sds(512), sds(512)],
        compiler_params=_params(("parallel",)),
        name="mixer_proj",
    )(h, pos_row, inv_col, pc, ps, gm, win, gq, wq, gkv, wkv)


V_ROWS = 80


def _mla_kernel(qt_ref, k_ref, vt_ref, o_ref, m_ref, acc_ref, s0_ref, s1_ref, mb0_ref, mb1_ref,
                al0_ref, al1_ref, *, tq, tk, seq, unroll):
    nk = seq // tk
    slots = ((s0_ref, mb0_ref, al0_ref), (s1_ref, mb1_ref, al1_ref))
    m_ref[...] = jnp.full(m_ref.shape, -jnp.inf, F32)
    acc_ref[...] = jnp.zeros(acc_ref.shape, F32)

    def scores(c, slot):
        s_ref, mb_ref, al_ref = slots[slot]
        off = c * tk if isinstance(c, int) else pl.multiple_of(c * tk, tk)
        for h in range(2):
            k = k_ref[0, pl.ds(off, tk), h * LANES:(h + 1) * LANES]
            s = jnp.dot(k, qt_ref[h * LANES:(h + 1) * LANES, :], preferred_element_type=F32)
            s_ref[h] = s
            m_old = m_ref[h]
            m_new = jnp.maximum(m_old, jnp.max(s, axis=0, keepdims=True))
            m_ref[h] = m_new
            mb_ref[h] = m_new
            al_ref[h] = jnp.exp2(m_old - m_new)

    def weighted_values(c, slot):
        s_ref, mb_ref, al_ref = slots[slot]
        off = c * tk if isinstance(c, int) else pl.multiple_of(c * tk, tk)
        for h in range(2):
            p = jnp.exp2(s_ref[h] - mb_ref[h][0:1, :])
            vt = vt_ref[h * LANES:h * LANES + V_ROWS, pl.ds(off, tk)]
            acc_ref[h] = al_ref[h][0:1, :] * acc_ref[h] + jnp.dot(vt, p.astype(BF16),
                                                                  preferred_element_type=F32)

    def chunk_group(c0, last):
        for j in range(unroll):
            if not (last and j == unroll - 1):
                scores(c0 + j + 1, (j + 1) % 2)
            weighted_values(c0 + j, j % 2)

    scores(0, 0)

    def body(i, carry):
        chunk_group(i * unroll, False)
        return carry

    lax.fori_loop(0, nk // unroll - 1, body, 0)
    chunk_group(nk - unroll, True)

    outs = []
    for h in range(2):
        acc = acc_ref[h]
        outs.append(acc[:VDIM, :] / acc[VDIM:VDIM + 1, :])
    o_ref[0] = jnp.concatenate(outs, axis=0).T.astype(BF16)


def _mla_attention(qt, k, vt, tq=1024, tk=512, unroll=4):
    b, s, _ = k.shape
    npair = MLA_HEADS // 2
    assert unroll % 2 == 0 and (s // tk) % unroll == 0
    nq = s // tq
    rowstate = pltpu.VMEM((2, 8, tq), F32)
    sbuf = pltpu.VMEM((2, tk, tq), F32)
    return pl.pallas_call(
        functools.partial(_mla_kernel, tq=tq, tk=tk, seq=s, unroll=unroll),
        grid=(b, npair, nq),
        in_specs=[pl.BlockSpec((2 * LANES, tq), lambda bi, p, i: (p, bi * nq + i)),
                  pl.BlockSpec((1, s, 2 * LANES), lambda bi, p, i: (bi, 0, p)),
                  pl.BlockSpec((2 * LANES, s), lambda bi, p, i: (p, bi))],
        out_specs=pl.BlockSpec((1, tq, LANES), lambda bi, p, i: (bi, i, p)),
        out_shape=jax.ShapeDtypeStruct((b, s, MLA_HEADS * VDIM), BF16),
        scratch_shapes=[rowstate, pltpu.VMEM((2, V_ROWS, tq), F32), sbuf, sbuf,
                        rowstate, rowstate, rowstate, rowstate],
        compiler_params=_params(("parallel", "parallel", "parallel")),
        name="mla_attention",
    )(qt, k, vt)


def _log_sigmoid(x):
    return jnp.minimum(x, 0.0) - jnp.log1p(jnp.exp(-jnp.abs(x)))


def _retention_kernel(q_ref, k_ref, v_ref, g_ref, dec_ref, o_ref, acc_lo_ref, acc_hi_ref,
                      stf_ref, stb_ref, dmf_ref, dmb_ref, tab_ref, ones_ref, *, ch, seq, unroll):
    nc = seq // ch
    half = nc // 2
    nh = QUAD // RET_DIM
    lane = lax.broadcasted_iota(jnp.int32, (ch, QUAD), 1)
    row = lax.broadcasted_iota(jnp.int32, (ch, QUAD), 0).astype(F32)
    khead = (lane >> 5) & (nh - 1)
    vhead = lane >> 6
    ri = lax.broadcasted_iota(jnp.int32, (ch, ch), 0)
    ci = lax.broadcasted_iota(jnp.int32, (ch, ch), 1)
    sd = lax.broadcasted_iota(jnp.int32, (QUAD, QUAD), 0)
    se = lax.broadcasted_iota(jnp.int32, (QUAD, QUAD), 1)
    state_mask = ((sd >> 5) & (nh - 1)) == (se >> 6)
    ones_ref[...] = jnp.where((sd >> 6) == (se >> 6), 1.0, 0.0).astype(BF16)

    lg = _log_sigmoid(dec_ref[...])
    tab_ref[0] = jnp.exp(lg[0:1] * (row + 1.0))
    tab_ref[1] = jnp.exp(lg[1:2] * (ch - 1.0 - row))
    tab_ref[2] = jnp.exp(lg[2:3] * (ch - row))
    tab_ref[3] = jnp.exp(lg[3:4] * row)
    c_dec = (jnp.exp(lg[0:1] * float(ch)), jnp.exp(lg[2:3] * float(ch)))
    rel_f = jnp.maximum(ri - ci, 0).astype(F32)
    rel_b = jnp.maximum(ci - ri, 0).astype(F32)
    for h in range(nh):
        dmf_ref[h] = jnp.where(ri >= ci, jnp.exp(lg[0:1, h * RET_DIM:h * RET_DIM + 1] * rel_f), 0.0)
        dmb_ref[h] = jnp.where(ci > ri, jnp.exp(lg[2:3, h * RET_DIM:h * RET_DIM + 1] * rel_b), 0.0)
    stf_ref[...] = jnp.zeros_like(stf_ref)
    stb_ref[...] = jnp.zeros_like(stb_ref)

    def chunks(items):
        loaded = []
        for c, _ in items:
            off = pl.multiple_of(c * ch, ch)
            loaded.append((q_ref[0, pl.ds(off, ch), :], k_ref[0, pl.ds(off, ch), :],
                           v_ref[0, pl.ds(off, ch), :]))
        scores = []
        for qc, kc, _ in loaded:
            zero = jnp.zeros_like(qc)
            scores.append([lax.dot_general(jnp.where(khead == h, qc, zero), kc,
                                           (((1,), (1,)), ((), ())), preferred_element_type=F32)
                           for h in range(nh)])
        outs = []
        for (_, fwd), (_, _, vc), ss in zip(items, loaded, scores):
            dm_ref = dmf_ref if fwd else dmb_ref
            out = None
            for h in range(nh):
                ih = jnp.dot((ss[h] * dm_ref[h]).astype(BF16), vc, preferred_element_type=F32)
                out = ih if out is None else jnp.where(vhead == h, ih, out)
            outs.append(out)
        upds = []
        for (_, fwd), (_, kc, vc) in zip(items, loaded):
            kd = (kc.astype(F32) * tab_ref[1 if fwd else 3]).astype(BF16)
            upds.append(lax.dot_general(kd, vc, (((0,), (0,)), ((), ())),
                                        preferred_element_type=F32))
        for i, ((_, fwd), (qc, _, _)) in enumerate(zip(items, loaded)):
            st_ref = stf_ref if fwd else stb_ref
            state = st_ref[...]
            cross = jnp.dot(qc, state.astype(BF16), preferred_element_type=F32)
            outs[i] = outs[i] + cross * tab_ref[0 if fwd else 2]
            st_ref[...] = state * c_dec[0 if fwd else 1] + jnp.where(state_mask, upds[i], 0.0)
        return outs

    def head_sum(x):
        hi = x.astype(BF16)
        lo = (x - hi.astype(F32)).astype(BF16)
        return (jnp.dot(hi, ones_ref[...], preferred_element_type=F32)
                + jnp.dot(lo, ones_ref[...], preferred_element_type=F32))

    def finish(c, tot):
        off = pl.multiple_of(c * ch, ch)
        d = tot - head_sum(tot) * (1.0 / RET_DIM)
        var = head_sum(d * d) * (1.0 / RET_DIM)
        gate = g_ref[0, pl.ds(off, ch), :].astype(F32)
        o_ref[0, pl.ds(off, ch), :] = (d * lax.rsqrt(var + EPS) * gate).astype(BF16)

    def rows(ref, c):
        return ref.at[pl.ds(pl.multiple_of(c * ch, ch), ch), :]

    def pairs(t):
        items = []
        for j in range(unroll):
            cf = t * unroll + j
            items += [(cf, True), (nc - 1 - cf, False)]
        return items

    def first(t, carry):
        items = pairs(t)
        for (c, fwd), out in zip(items, chunks(items)):
            if fwd:
                rows(acc_lo_ref, c)[...] = out
            else:
                rows(acc_hi_ref, c - half)[...] = out
        return carry

    lax.fori_loop(0, half // unroll, first, 0)

    def second(t, carry):
        items = pairs(t)
        for (c, fwd), out in zip(items, chunks(items)):
            other = rows(acc_hi_ref, c - half) if fwd else rows(acc_lo_ref, c)
            finish(c, out + other[...])
        return carry

    lax.fori_loop(half // unroll, nc // unroll, second, 0)


def _retention(q, k, v, g, dec, ch=256, unroll=2):
    b, s, _ = q.shape
    nquad = D_RET // QUAD
    assert (s // ch // 2) % unroll == 0
    blk = pl.BlockSpec((1, s, QUAD), lambda bi, u: (bi, 0, u))
    return pl.pallas_call(
        functools.partial(_retention_kernel, ch=ch, seq=s, unroll=unroll),
        grid=(b, nquad),
        in_specs=[blk, blk, blk, blk, pl.BlockSpec((4, QUAD), lambda bi, u: (0, u))],
        out_specs=blk,
        out_shape=jax.ShapeDtypeStruct((b, s, D_RET), BF16),
        scratch_shapes=[pltpu.VMEM((s // 2, QUAD), F32), pltpu.VMEM((s // 2, QUAD), F32),
                        pltpu.VMEM((QUAD, QUAD), F32), pltpu.VMEM((QUAD, QUAD), F32),
                        pltpu.VMEM((QUAD // RET_DIM, ch, ch), F32),
                        pltpu.VMEM((QUAD // RET_DIM, ch, ch), F32),
                        pltpu.VMEM((4, ch, QUAD), F32), pltpu.VMEM((QUAD, QUAD), BF16)],
        compiler_params=_params(("parallel", "parallel")),
        name="retention",
    )(q, k, v, g, dec)


def _ret_qk_perm():
    new = np.arange(D_RET)
    quad, j = new // QUAD, new % QUAD
    head = 4 * quad + (j % LANES) // 32
    dim = (j // LANES) * 32 + j % 32
    return head * RET_DIM + dim


def _ret_lane_heads():
    lane = np.arange(D_RET)
    v_head = lane // RET_DIM
    k_head = 4 * (lane // QUAD) + ((lane % QUAD) % LANES) // 32
    return v_head, k_head


def _prep_w_in(w_in):
    c_q, c_kv, k_pe, r_q, r_k, r_v, r_g = jnp.split(
        w_in, np.cumsum([Q_LORA, KV_LORA, ROPE, D_RET, D_RET, D_RET])[:], axis=1)
    k = w_in.shape[0]
    z = lambda w: jnp.zeros((k, w), w_in.dtype)
    kpe = jnp.concatenate([z(64), k_pe, z(32)], axis=1)
    kpe_sw = jnp.concatenate([z(64), k_pe[:, 16:], k_pe[:, :16], z(32)], axis=1)
    perm = _ret_qk_perm()
    return jnp.concatenate([c_q, c_kv, kpe, kpe_sw, z(LANES), r_q[:, perm],
                            r_k[:, perm] * (RET_DIM ** -0.5), r_v, r_g], axis=1).astype(BF16)


def _prep_w_uq(w_uq):
    k = w_uq.shape[0]
    w3 = w_uq.reshape(k, MLA_HEADS, NOPE + ROPE)
    nope, x1, x2 = w3[:, :, :NOPE], w3[:, :, NOPE:NOPE + 16], w3[:, :, NOPE + 16:]
    z32 = jnp.zeros((k, MLA_HEADS, 32), w_uq.dtype)
    wq = jnp.concatenate([nope, x1, x2, z32], axis=2).reshape(k, MLA_HEADS * LANES)
    wqs = jnp.concatenate([jnp.zeros_like(nope), x2, x1, z32], axis=2).reshape(k, MLA_HEADS * LANES)
    return jnp.concatenate([wq, wqs], axis=1).astype(BF16)


def _prep_w_ukv(w_ukv):
    k = w_ukv.shape[0]
    w3 = w_ukv.reshape(k, MLA_HEADS, NOPE + VDIM)
    kn = jnp.concatenate([w3[:, :, :NOPE], jnp.zeros((k, MLA_HEADS, 64), w_ukv.dtype)], axis=2)
    v = jnp.concatenate([w3[:, :, NOPE:], jnp.zeros((k, MLA_HEADS, 64), w_ukv.dtype)], axis=2)
    return jnp.concatenate([kn.reshape(k, MLA_HEADS * LANES), v.reshape(k, MLA_HEADS * LANES)],
                           axis=1).astype(BF16)


def _prep_decay(dec_f, dec_b):
    v_head, k_head = _ret_lane_heads()
    f, b = dec_f.astype(F32), dec_b.astype(F32)
    return jnp.stack([f[v_head], f[k_head], b[v_head], b[k_head]])


def kernel(x, positions, ffn1_norm, ffn1_w_gate, ffn1_w_up, ffn1_w_down, mix_norm, w_in, q_norm, w_uq, kv_norm, w_ukv, ret_decay_fwd, ret_decay_bwd, w_o, ffn2_norm, ffn2_w_gate, ffn2_w_up, ffn2_w_down, final_norm):
    b, s, d = x.shape
    n = b * s
    assert ffn1_norm.shape[0] == 1, "specialised to DEPTH == 1 (the final norm is fused into the layer)"
    pos_row = positions.astype(F32).reshape(1, n)

    h = _ffn1(x.reshape(n, d), ffn1_norm[0][None, :], ffn1_w_gate[0], ffn1_w_up[0],
              ffn1_w_down[0])
    qm, km, vm, rq, rk, rv, rg = _proj(
        h, pos_row, mix_norm[0][None, :], _prep_w_in(w_in[0]), q_norm[0][None, :],
        _prep_w_uq(w_uq[0]), kv_norm[0][None, :], _prep_w_ukv(w_ukv[0]))
    a = _mla_attention(qm, km.reshape(b, s, -1), vm)
    r = _retention(rq.reshape(b, s, -1), rk.reshape(b, s, -1), rv.reshape(b, s, -1),
                   rg.reshape(b, s, -1), _prep_decay(ret_decay_fwd[0], ret_decay_bwd[0]))
    wo = w_o[0].astype(BF16)
    out = _out_ffn2(h, a.reshape(n, -1), r.reshape(n, -1), wo, ffn2_norm[0][None, :],
                    ffn2_w_gate[0], ffn2_w_up[0], ffn2_w_down[0], final_norm[None, :])
    return out.reshape(b, s, d)
```

```python
import functools
import math

import numpy as np
import jax
import jax.numpy as jnp
from jax import lax
from jax.experimental import pallas as pl
from jax.experimental.pallas import tpu as pltpu

F32 = jnp.float32
BF16 = jnp.bfloat16

D_MODEL = 1024
D_FF = 2816
MLA_HEADS = 8
Q_LORA = 256
KV_LORA = 128
NOPE = 64
ROPE = 32
VDIM = 64
RET_HEADS = 8
RET_DIM = 64
D_RET = RET_HEADS * RET_DIM
ROPE_THETA = 10000.0
EPS = 1e-6
MLA_SCALE = (NOPE + ROPE) ** -0.5
LOG2E = math.log2(math.e)

LANES = 128
QUAD = 256
VMEM_LIMIT = 56 * 1024 * 1024

_OFF_CQ = 0
_OFF_CKV = 256
_OFF_RQ = 768
_OFF_RK = 1280
_OFF_RV = 1792
_OFF_RG = 2304
D_IN_AUG = 2816


def _params(sem):
    return pltpu.CompilerParams(dimension_semantics=sem, vmem_limit_bytes=VMEM_LIMIT)


def _const_spec(shape):
    nd = len(shape)
    return pl.BlockSpec(shape, lambda *_: (0,) * nd, pipeline_mode=pl.Buffered(1))


N_FREQ = 64


def _rope_constants():
    row = np.arange(N_FREQ)
    inv = np.where(row < 32, ROPE_THETA ** (-(row % 32) / 32.0),
                   np.where(row < 48, ROPE_THETA ** (-((row - 32) % 16) / 16.0), 0.0))
    pc = np.zeros((N_FREQ, 2 * LANES))
    ps = np.zeros((N_FREQ, 2 * LANES))
    for l in range(LANES):
        pc[l % 32, l] = 1.0
        ps[l % 32, l] = 1.0
    for l in range(NOPE, NOPE + ROPE):
        r = 32 + (l - NOPE) % 16
        pc[r, LANES + l] = 1.0
        ps[r, LANES + l] = -1.0 if l < NOPE + 16 else 1.0
    return jnp.asarray(inv[:, None], F32), jnp.asarray(pc, BF16), jnp.asarray(ps, BF16)


def _rope_tables(pos_row, inv_col, pc, ps):
    ang = inv_col * pos_row
    dn = (((0,), (0,)), ((), ()))

    def place(x, p):
        hi = x.astype(BF16)
        lo = (x - hi.astype(F32)).astype(BF16)
        return (lax.dot_general(hi, p, dn, preferred_element_type=F32)
                + lax.dot_general(lo, p, dn, preferred_element_type=F32))

    return place(jnp.cos(ang), pc), place(jnp.sin(ang), ps)


def _rms(x, g):
    return x * lax.rsqrt(jnp.mean(x * x, axis=-1, keepdims=True) + EPS) * g


def _swiglu_acc(n_bf, wg_ref, wu_ref, wd_ref, ck):
    acc = None
    for j in range(D_FF // ck):
        sl = slice(j * ck, (j + 1) * ck)
        g = jnp.dot(n_bf, wg_ref[:, sl].astype(BF16), preferred_element_type=F32)
        u = jnp.dot(n_bf, wu_ref[:, sl].astype(BF16), preferred_element_type=F32)
        hm = (g * jax.nn.sigmoid(g) * u).astype(BF16)
        d = jnp.dot(hm, wd_ref[sl, :].astype(BF16), preferred_element_type=F32)
        acc = d if acc is None else acc + d
    return acc


def _ffn1_kernel(x_ref, g_ref, wg_ref, wu_ref, wd_ref, o_ref, *, ck):
    x = x_ref[...]
    n_bf = _rms(x, g_ref[...]).astype(BF16)
    o_ref[...] = x + 0.5 * _swiglu_acc(n_bf, wg_ref, wu_ref, wd_ref, ck)


def _ffn1(x, g, wg, wu, wd, tm=512, ck=256):
    n = x.shape[0]
    row = pl.BlockSpec((tm, D_MODEL), lambda i: (i, 0))
    return pl.pallas_call(
        functools.partial(_ffn1_kernel, ck=ck),
        grid=(n // tm,),
        in_specs=[row, _const_spec((1, D_MODEL)), _const_spec((D_MODEL, D_FF)),
                  _const_spec((D_MODEL, D_FF)), _const_spec((D_FF, D_MODEL))],
        out_specs=row,
        out_shape=jax.ShapeDtypeStruct((n, D_MODEL), F32),
        compiler_params=_params(("parallel",)),
        name="ffn1",
    )(x, g, wg, wu, wd)


def _out_ffn2_kernel(h_ref, a_ref, r_ref, wo_ref, g_ref, wg_ref, wu_ref, wd_ref, gf_ref, o_ref,
                     *, ck):
    h = (h_ref[...]
         + jnp.dot(a_ref[...], wo_ref[:MLA_HEADS * VDIM, :], preferred_element_type=F32)
         + jnp.dot(r_ref[...], wo_ref[MLA_HEADS * VDIM:, :], preferred_element_type=F32))
    n_bf = _rms(h, g_ref[...]).astype(BF16)
    h = h + 0.5 * _swiglu_acc(n_bf, wg_ref, wu_ref, wd_ref, ck)
    o_ref[...] = _rms(h, gf_ref[...])


def _out_ffn2(h, a, r, wo, g, wg, wu, wd, gf, tm=512, ck=256):
    n = h.shape[0]
    row = pl.BlockSpec((tm, D_MODEL), lambda i: (i, 0))
    half = pl.BlockSpec((tm, D_RET), lambda i: (i, 0))
    return pl.pallas_call(
        functools.partial(_out_ffn2_kernel, ck=ck),
        grid=(n // tm,),
        in_specs=[row, half, half, _const_spec((MLA_HEADS * VDIM + D_RET, D_MODEL)),
                  _const_spec((1, D_MODEL)), _const_spec((D_MODEL, D_FF)),
                  _const_spec((D_MODEL, D_FF)), _const_spec((D_FF, D_MODEL)),
                  _const_spec((1, D_MODEL))],
        out_specs=row,
        out_shape=jax.ShapeDtypeStruct((n, D_MODEL), F32),
        compiler_params=_params(("parallel",)),
        name="out_ffn2",
    )(h, a, r, wo, g, wg, wu, wd, gf)


def _proj_kernel(h_ref, pos_ref, inv_ref, pc_ref, ps_ref, gm_ref, win_ref, gq_ref, wq_ref, gkv_ref,
                 wkv_ref, qm_ref, km_ref, vm_ref, rq_ref, rk_ref, rv_ref, rg_ref):
    n_bf = _rms(h_ref[...], gm_ref[...]).astype(BF16)

    def proj(off, width):
        return jnp.dot(n_bf, win_ref[:, off:off + width], preferred_element_type=F32)

    def ret_rope(x, out_ref):
        for u in range(D_RET // QUAD):
            x1 = x[:, u * QUAD:u * QUAD + LANES]
            x2 = x[:, u * QUAD + LANES:(u + 1) * QUAD]
            out_ref[:, u * QUAD:u * QUAD + LANES] = (x1 * cr - x2 * sr).astype(BF16)
            out_ref[:, u * QUAD + LANES:(u + 1) * QUAD] = (x1 * sr + x2 * cr).astype(BF16)

    cq_bf = _rms(proj(_OFF_CQ, Q_LORA), gq_ref[...]).astype(BF16)
    lat = proj(_OFF_CKV, 4 * LANES)
    ckv_bf = _rms(lat[:, :KV_LORA], gkv_ref[...]).astype(BF16)
    xq = proj(_OFF_RQ, D_RET)

    cos_t, sin_t = _rope_tables(pos_ref[...], inv_ref[...], pc_ref[...], ps_ref[...])
    cr, sr = cos_t[:, :LANES], sin_t[:, :LANES]
    nope_lane = lax.broadcasted_iota(jnp.int32, (1, LANES), 1) < NOPE
    cm, sm = cos_t[:, LANES:] + jnp.where(nope_lane, 1.0, 0.0), sin_t[:, LANES:]

    xk = proj(_OFF_RK, D_RET)
    ret_rope(xq, rq_ref)
    rv_ref[...] = proj(_OFF_RV, D_RET).astype(BF16)
    ret_rope(xk, rk_ref)
    g = proj(_OFF_RG, D_RET)
    rg_ref[...] = (g * jax.nn.sigmoid(g)).astype(BF16)

    qq = jnp.dot(cq_bf, wq_ref[...], preferred_element_type=F32)
    kv = jnp.dot(ckv_bf, wkv_ref[...], preferred_element_type=F32)
    nh = MLA_HEADS * LANES
    kpe = lat[:, LANES:2 * LANES] * cm + lat[:, 2 * LANES:3 * LANES] * sm
    cq_s, sq_s = cm * (MLA_SCALE * LOG2E), sm * (MLA_SCALE * LOG2E)
    for h in range(MLA_HEADS):
        sl = slice(h * LANES, (h + 1) * LANES)
        sw = slice(nh + h * LANES, nh + (h + 1) * LANES)
        qm_ref[sl, :] = (qq[:, sl] * cq_s + qq[:, sw] * sq_s).astype(BF16).T
        km_ref[:, sl] = (kv[:, sl] + kpe).astype(BF16)
    lane = lax.broadcasted_iota(jnp.int32, (1, nh), 1)
    ones_lane = jnp.where(lane % LANES == VDIM, 1.0, 0.0)
    vm_ref[...] = (kv[:, nh:] + ones_lane).astype(BF16).T


def _proj(h, pos_row, gm, win, gq, wq, gkv, wkv, tm=512):
    n = h.shape[0]
    row = lambda w: pl.BlockSpec((tm, w), lambda i: (i, 0))
    sds = lambda w: jax.ShapeDtypeStruct((n, w), BF16)
    col = pl.BlockSpec((MLA_HEADS * LANES, tm), lambda i: (0, i))
    sds_t = jax.ShapeDtypeStruct((MLA_HEADS * LANES, n), BF16)
    inv_col, pc, ps = _rope_constants()
    return pl.pallas_call(
        _proj_kernel,
        grid=(n // tm,),
        in_specs=[row(D_MODEL), pl.BlockSpec((1, tm), lambda i: (0, i)),
                  _const_spec((N_FREQ, 1)), _const_spec((N_FREQ, 2 * LANES)),
                  _const_spec((N_FREQ, 2 * LANES)), _const_spec((1, D_MODEL)),
                  _const_spec((D_MODEL, D_IN_AUG)), _const_spec((1, Q_LORA)),
                  _const_spec((Q_LORA, 2 * MLA_HEADS * LANES)),
                  _const_spec((1, KV_LORA)), _const_spec((KV_LORA, 2 * MLA_HEADS * LANES))],
        out_specs=[col, row(1024), col, row(512), row(512), row(512), row(512)],
        out_shape=[sds_t, sds(1024), sds_t, sds(512), sds(512), sds(512), sds(512)],
        compiler_params=_params(("parallel",)),
        name="mixer_proj",
    )(h, pos_row, inv_col, pc, ps, gm, win, gq, wq, gkv, wkv)


V_ROWS = 80


def _mla_kernel(qt_ref, k_ref, vt_ref, o_ref, m_ref, acc_ref, s0_ref, s1_ref, mb0_ref, mb1_ref,
                al0_ref, al1_ref, *, tq, tk, seq, unroll):
    nk = seq // tk
    slots = ((s0_ref, mb0_ref, al0_ref), (s1_ref, mb1_ref, al1_ref))
    m_ref[...] = jnp.full(m_ref.shape, -jnp.inf, F32)
    acc_ref[...] = jnp.zeros(acc_ref.shape, F32)

    def scores(c, slot):
        s_ref, mb_ref, al_ref = slots[slot]
        off = c * tk if isinstance(c, int) else pl.multiple_of(c * tk, tk)
        for h in range(2):
            k = k_ref[0, pl.ds(off, tk), h * LANES:(h + 1) * LANES]
            s = jnp.dot(k, qt_ref[h * LANES:(h + 1) * LANES, :], preferred_element_type=F32)
            s_ref[h] = s
            m_old = m_ref[h]
            m_new = jnp.maximum(m_old, jnp.max(s, axis=0, keepdims=True))
            m_ref[h] = m_new
            mb_ref[h] = m_new
            al_ref[h] = jnp.exp2(m_old - m_new)

    def weighted_values(c, slot):
        s_ref, mb_ref, al_ref = slots[slot]
        off = c * tk if isinstance(c, int) else pl.multiple_of(c * tk, tk)
        for h in range(2):
            p = jnp.exp2(s_ref[h] - mb_ref[h][0:1, :])
            vt = vt_ref[h * LANES:h * LANES + V_ROWS, pl.ds(off, tk)]
            acc_ref[h] = al_ref[h][0:1, :] * acc_ref[h] + jnp.dot(vt, p.astype(BF16),
                                                                  preferred_element_type=F32)

    def chunk_group(c0, last):
        for j in range(unroll):
            if not (last and j == unroll - 1):
                scores(c0 + j + 1, (j + 1) % 2)
            weighted_values(c0 + j, j % 2)

    scores(0, 0)

    def body(i, carry):
        chunk_group(i * unroll, False)
        return carry

    lax.fori_loop(0, nk // unroll - 1, body, 0)
    chunk_group(nk - unroll, True)

    outs = []
    for h in range(2):
        acc = acc_ref[h]
        outs.append(acc[:VDIM, :] / acc[VDIM:VDIM + 1, :])
    o_ref[0] = jnp.concatenate(outs, axis=0).T.astype(BF16)


def _mla_attention(qt, k, vt, tq=1024, tk=1024, unroll=2):
    b, s, _ = k.shape
    npair = MLA_HEADS // 2
    assert unroll % 2 == 0 and (s // tk) % unroll == 0
    nq = s // tq
    rowstate = pltpu.VMEM((2, 8, tq), F32)
    sbuf = pltpu.VMEM((2, tk, tq), F32)
    return pl.pallas_call(
        functools.partial(_mla_kernel, tq=tq, tk=tk, seq=s, unroll=unroll),
        grid=(b, npair, nq),
        in_specs=[pl.BlockSpec((2 * LANES, tq), lambda bi, p, i: (p, bi * nq + i)),
                  pl.BlockSpec((1, s, 2 * LANES), lambda bi, p, i: (bi, 0, p)),
                  pl.BlockSpec((2 * LANES, s), lambda bi, p, i: (p, bi))],
        out_specs=pl.BlockSpec((1, tq, LANES), lambda bi, p, i: (bi, i, p)),
        out_shape=jax.ShapeDtypeStruct((b, s, MLA_HEADS * VDIM), BF16),
        scratch_shapes=[rowstate, pltpu.VMEM((2, V_ROWS, tq), F32), sbuf, sbuf,
                        rowstate, rowstate, rowstate, rowstate],
        compiler_params=_params(("parallel", "parallel", "parallel")),
        name="mla_attention",
    )(qt, k, vt)


def _log_sigmoid(x):
    return jnp.minimum(x, 0.0) - jnp.log1p(jnp.exp(-jnp.abs(x)))


def _retention_kernel(q_ref, k_ref, v_ref, g_ref, dec_ref, o_ref, acc_lo_ref, acc_hi_ref,
                      stf_ref, stb_ref, dmf_ref, dmb_ref, tab_ref, ones_ref, *, ch, seq, unroll):
    nc = seq // ch
    half = nc // 2
    nh = QUAD // RET_DIM
    lane = lax.broadcasted_iota(jnp.int32, (ch, QUAD), 1)
    row = lax.broadcasted_iota(jnp.int32, (ch, QUAD), 0).astype(F32)
    khead = (lane >> 5) & (nh - 1)
    vhead = lane >> 6
    ri = lax.broadcasted_iota(jnp.int32, (ch, ch), 0)
    ci = lax.broadcasted_iota(jnp.int32, (ch, ch), 1)
    sd = lax.broadcasted_iota(jnp.int32, (QUAD, QUAD), 0)
    se = lax.broadcasted_iota(jnp.int32, (QUAD, QUAD), 1)
    state_mask = ((sd >> 5) & (nh - 1)) == (se >> 6)
    ones_ref[...] = jnp.where((sd >> 6) == (se >> 6), 1.0, 0.0).astype(BF16)

    lg = _log_sigmoid(dec_ref[...])
    tab_ref[0] = jnp.exp(lg[0:1] * (row + 1.0))
    tab_ref[1] = jnp.exp(lg[1:2] * (ch - 1.0 - row))
    tab_ref[2] = jnp.exp(lg[2:3] * (ch - row))
    tab_ref[3] = jnp.exp(lg[3:4] * row)
    c_dec = (jnp.exp(lg[0:1] * float(ch)), jnp.exp(lg[2:3] * float(ch)))
    rel_f = jnp.maximum(ri - ci, 0).astype(F32)
    rel_b = jnp.maximum(ci - ri, 0).astype(F32)
    for h in range(nh):
        dmf_ref[h] = jnp.where(ri >= ci, jnp.exp(lg[0:1, h * RET_DIM:h * RET_DIM + 1] * rel_f), 0.0)
        dmb_ref[h] = jnp.where(ci > ri, jnp.exp(lg[2:3, h * RET_DIM:h * RET_DIM + 1] * rel_b), 0.0)
    stf_ref[...] = jnp.zeros_like(stf_ref)
    stb_ref[...] = jnp.zeros_like(stb_ref)

    def chunks(items):
        loaded = []
        for c, _ in items:
            off = pl.multiple_of(c * ch, ch)
            loaded.append((q_ref[0, pl.ds(off, ch), :], k_ref[0, pl.ds(off, ch), :],
                           v_ref[0, pl.ds(off, ch), :]))
        scores = []
        for qc, kc, _ in loaded:
            zero = jnp.zeros_like(qc)
            scores.append([lax.dot_general(jnp.where(khead == h, qc, zero), kc,
                                           (((1,), (1,)), ((), ())), preferred_element_type=F32)
                           for h in range(nh)])
        outs = []
        for (_, fwd), (_, _, vc), ss in zip(items, loaded, scores):
            dm_ref = dmf_ref if fwd else dmb_ref
            out = None
            for h in range(nh):
                ih = jnp.dot((ss[h] * dm_ref[h]).astype(BF16), vc, preferred_element_type=F32)
                out = ih if out is None else jnp.where(vhead == h, ih, out)
            outs.append(out)
        upds = []
        for (_, fwd), (_, kc, vc) in zip(items, loaded):
            kd = (kc.astype(F32) * tab_ref[1 if fwd else 3]).astype(BF16)
            upds.append(lax.dot_general(kd, vc, (((0,), (0,)), ((), ())),
                                        preferred_element_type=F32))
        for i, ((_, fwd), (qc, _, _)) in enumerate(zip(items, loaded)):
            st_ref = stf_ref if fwd else stb_ref
            state = st_ref[...]
            cross = jnp.dot(qc, state.astype(BF16), preferred_element_type=F32)
            outs[i] = outs[i] + cross * tab_ref[0 if fwd else 2]
            st_ref[...] = state * c_dec[0 if fwd else 1] + jnp.where(state_mask, upds[i], 0.0)
        return outs

    def head_sum(x):
        hi = x.astype(BF16)
        lo = (x - hi.astype(F32)).astype(BF16)
        return (jnp.dot(hi, ones_ref[...], preferred_element_type=F32)
                + jnp.dot(lo, ones_ref[...], preferred_element_type=F32))

    def finish(c, tot):
        off = pl.multiple_of(c * ch, ch)
        d = tot - head_sum(tot) * (1.0 / RET_DIM)
        var = head_sum(d * d) * (1.0 / RET_DIM)
        gate = g_ref[0, pl.ds(off, ch), :].astype(F32)
        o_ref[0, pl.ds(off, ch), :] = (d * lax.rsqrt(var + EPS) * gate).astype(BF16)

    def rows(ref, c):
        return ref.at[pl.ds(pl.multiple_of(c * ch, ch), ch), :]

    def pairs(t):
        items = []
        for j in range(unroll):
            cf = t * unroll + j
            items += [(cf, True), (nc - 1 - cf, False)]
        return items

    def first(t, carry):
        items = pairs(t)
        for (c, fwd), out in zip(items, chunks(items)):
            if fwd:
                rows(acc_lo_ref, c)[...] = out
            else:
                rows(acc_hi_ref, c - half)[...] = out
        return carry

    lax.fori_loop(0, half // unroll, first, 0)

    def second(t, carry):
        items = pairs(t)
        for (c, fwd), out in zip(items, chunks(items)):
            other = rows(acc_hi_ref, c - half) if fwd else rows(acc_lo_ref, c)
            finish(c, out + other[...])
        return carry

    lax.fori_loop(half // unroll, nc // unroll, second, 0)


def _retention(q, k, v, g, dec, ch=256, unroll=2):
    b, s, _ = q.shape
    nquad = D_RET // QUAD
    assert (s // ch // 2) % unroll == 0
    blk = pl.BlockSpec((1, s, QUAD), lambda bi, u: (bi, 0, u))
    return pl.pallas_call(
        functools.partial(_retention_kernel, ch=ch, seq=s, unroll=unroll),
        grid=(b, nquad),
        in_specs=[blk, blk, blk, blk, pl.BlockSpec((4, QUAD), lambda bi, u: (0, u))],
        out_specs=blk,
        out_shape=jax.ShapeDtypeStruct((b, s, D_RET), BF16),
        scratch_shapes=[pltpu.VMEM((s // 2, QUAD), F32), pltpu.VMEM((s // 2, QUAD), F32),
                        pltpu.VMEM((QUAD, QUAD), F32), pltpu.VMEM((QUAD, QUAD), F32),
                        pltpu.VMEM((QUAD // RET_DIM, ch, ch), F32),
                        pltpu.VMEM((QUAD // RET_DIM, ch, ch), F32),
                        pltpu.VMEM((4, ch, QUAD), F32), pltpu.VMEM((QUAD, QUAD), BF16)],
        compiler_params=_params(("parallel", "parallel")),
        name="retention",
    )(q, k, v, g, dec)


def _ret_qk_perm():
    new = np.arange(D_RET)
    quad, j = new // QUAD, new % QUAD
    head = 4 * quad + (j % LANES) // 32
    dim = (j // LANES) * 32 + j % 32
    return head * RET_DIM + dim


def _ret_lane_heads():
    lane = np.arange(D_RET)
    v_head = lane // RET_DIM
    k_head = 4 * (lane // QUAD) + ((lane % QUAD) % LANES) // 32
    return v_head, k_head


def _prep_w_in(w_in):
    c_q, c_kv, k_pe, r_q, r_k, r_v, r_g = jnp.split(
        w_in, np.cumsum([Q_LORA, KV_LORA, ROPE, D_RET, D_RET, D_RET])[:], axis=1)
    k = w_in.shape[0]
    z = lambda w: jnp.zeros((k, w), w_in.dtype)
    kpe = jnp.concatenate([z(64), k_pe, z(32)], axis=1)
    kpe_sw = jnp.concatenate([z(64), k_pe[:, 16:], k_pe[:, :16], z(32)], axis=1)
    perm = _ret_qk_perm()
    return jnp.concatenate([c_q, c_kv, kpe, kpe_sw, z(LANES), r_q[:, perm],
                            r_k[:, perm] * (RET_DIM ** -0.5), r_v, r_g], axis=1).astype(BF16)


def _prep_w_uq(w_uq):
    k = w_uq.shape[0]
    w3 = w_uq.reshape(k, MLA_HEADS, NOPE + ROPE)
    nope, x1, x2 = w3[:, :, :NOPE], w3[:, :, NOPE:NOPE + 16], w3[:, :, NOPE + 16:]
    z32 = jnp.zeros((k, MLA_HEADS, 32), w_uq.dtype)
    wq = jnp.concatenate([nope, x1, x2, z32], axis=2).reshape(k, MLA_HEADS * LANES)
    wqs = jnp.concatenate([jnp.zeros_like(nope), x2, x1, z32], axis=2).reshape(k, MLA_HEADS * LANES)
    return jnp.concatenate([wq, wqs], axis=1).astype(BF16)


def _prep_w_ukv(w_ukv):
    k = w_ukv.shape[0]
    w3 = w_ukv.reshape(k, MLA_HEADS, NOPE + VDIM)
    kn = jnp.concatenate([w3[:, :, :NOPE], jnp.zeros((k, MLA_HEADS, 64), w_ukv.dtype)], axis=2)
    v = jnp.concatenate([w3[:, :, NOPE:], jnp.zeros((k, MLA_HEADS, 64), w_ukv.dtype)], axis=2)
    return jnp.concatenate([kn.reshape(k, MLA_HEADS * LANES), v.reshape(k, MLA_HEADS * LANES)],
                           axis=1).astype(BF16)


def _prep_decay(dec_f, dec_b):
    v_head, k_head = _ret_lane_heads()
    f, b = dec_f.astype(F32), dec_b.astype(F32)
    return jnp.stack([f[v_head], f[k_head], b[v_head], b[k_head]])


def kernel(x, positions, ffn1_norm, ffn1_w_gate, ffn1_w_up, ffn1_w_down, mix_norm, w_in, q_norm, w_uq, kv_norm, w_ukv, ret_decay_fwd, ret_decay_bwd, w_o, ffn2_norm, ffn2_w_gate, ffn2_w_up, ffn2_w_down, final_norm):
    b, s, d = x.shape
    n = b * s
    assert ffn1_norm.shape[0] == 1, "specialised to DEPTH == 1 (the final norm is fused into the layer)"
    pos_row = positions.astype(F32).reshape(1, n)

    h = _ffn1(x.reshape(n, d), ffn1_norm[0][None, :], ffn1_w_gate[0], ffn1_w_up[0],
              ffn1_w_down[0])
    qm, km, vm, rq, rk, rv, rg = _proj(
        h, pos_row, mix_norm[0][None, :], _prep_w_in(w_in[0]), q_norm[0][None, :],
        _prep_w_uq(w_uq[0]), kv_norm[0][None, :], _prep_w_ukv(w_ukv[0]))
    a = _mla_attention(qm, km.reshape(b, s, -1), vm)
    r = _retention(rq.reshape(b, s, -1), rk.reshape(b, s, -1), rv.reshape(b, s, -1),
                   rg.reshape(b, s, -1), _prep_decay(ret_decay_fwd[0], ret_decay_bwd[0]))
    wo = w_o[0].astype(BF16)
    out = _out_ffn2(h, a.reshape(n, -1), r.reshape(n, -1), wo, ffn2_norm[0][None, :],
                    ffn2_w_gate[0], ffn2_w_up[0], ffn2_w_down[0], final_norm[None, :])
    return out.reshape(b, s, d)
```

```python
import functools
import math

import numpy as np
import jax
import jax.numpy as jnp
from jax import lax
from jax.experimental import pallas as pl
from jax.experimental.pallas import tpu as pltpu

F32 = jnp.float32
BF16 = jnp.bfloat16

D_MODEL = 1024
D_FF = 2816
MLA_HEADS = 8
Q_LORA = 256
KV_LORA = 128
NOPE = 64
ROPE = 32
VDIM = 64
RET_HEADS = 8
RET_DIM = 64
D_RET = RET_HEADS * RET_DIM
ROPE_THETA = 10000.0
EPS = 1e-6
MLA_SCALE = (NOPE + ROPE) ** -0.5
LOG2E = math.log2(math.e)

LANES = 128
QUAD = 256
VMEM_LIMIT = 56 * 1024 * 1024

_OFF_CQ = 0
_OFF_CKV = 256
_OFF_RQ = 768
_OFF_RK = 1280
_OFF_RV = 1792
_OFF_RG = 2304
D_IN_AUG = 2816


def _params(sem):
    return pltpu.CompilerParams(dimension_semantics=sem, vmem_limit_bytes=VMEM_LIMIT)


def _const_spec(shape):
    nd = len(shape)
    return pl.BlockSpec(shape, lambda *_: (0,) * nd, pipeline_mode=pl.Buffered(1))


N_FREQ = 64


def _rope_constants():
    row = np.arange(N_FREQ)
    inv = np.where(row < 32, ROPE_THETA ** (-(row % 32) / 32.0),
                   np.where(row < 48, ROPE_THETA ** (-((row - 32) % 16) / 16.0), 0.0))
    pc = np.zeros((N_FREQ, 2 * LANES))
    ps = np.zeros((N_FREQ, 2 * LANES))
    for l in range(LANES):
        pc[l % 32, l] = 1.0
        ps[l % 32, l] = 1.0
    for l in range(NOPE, NOPE + ROPE):
        r = 32 + (l - NOPE) % 16
        pc[r, LANES + l] = 1.0
        ps[r, LANES + l] = -1.0 if l < NOPE + 16 else 1.0
    return jnp.asarray(inv[:, None], F32), jnp.asarray(pc, BF16), jnp.asarray(ps, BF16)


def _rope_tables(pos_row, inv_col, pc, ps):
    ang = inv_col * pos_row
    dn = (((0,), (0,)), ((), ()))

    def place(x, p):
        hi = x.astype(BF16)
        lo = (x - hi.astype(F32)).astype(BF16)
        return (lax.dot_general(hi, p, dn, preferred_element_type=F32)
                + lax.dot_general(lo, p, dn, preferred_element_type=F32))

    return place(jnp.cos(ang), pc), place(jnp.sin(ang), ps)


def _rms(x, g):
    return x * lax.rsqrt(jnp.mean(x * x, axis=-1, keepdims=True) + EPS) * g


def _swiglu_acc(n_bf, wg_ref, wu_ref, wd_ref, ck):
    acc = None
    for j in range(D_FF // ck):
        sl = slice(j * ck, (j + 1) * ck)
        g = jnp.dot(n_bf, wg_ref[:, sl].astype(BF16), preferred_element_type=F32)
        u = jnp.dot(n_bf, wu_ref[:, sl].astype(BF16), preferred_element_type=F32)
        hm = (g * jax.nn.sigmoid(g) * u).astype(BF16)
        d = jnp.dot(hm, wd_ref[sl, :].astype(BF16), preferred_element_type=F32)
        acc = d if acc is None else acc + d
    return acc


def _ffn1_kernel(x_ref, g_ref, wg_ref, wu_ref, wd_ref, o_ref, *, ck):
    x = x_ref[...]
    n_bf = _rms(x, g_ref[...]).astype(BF16)
    o_ref[...] = x + 0.5 * _swiglu_acc(n_bf, wg_ref, wu_ref, wd_ref, ck)


def _ffn1(x, g, wg, wu, wd, tm=512, ck=256):
    n = x.shape[0]
    row = pl.BlockSpec((tm, D_MODEL), lambda i: (i, 0))
    return pl.pallas_call(
        functools.partial(_ffn1_kernel, ck=ck),
        grid=(n // tm,),
        in_specs=[row, _const_spec((1, D_MODEL)), _const_spec((D_MODEL, D_FF)),
                  _const_spec((D_MODEL, D_FF)), _const_spec((D_FF, D_MODEL))],
        out_specs=row,
        out_shape=jax.ShapeDtypeStruct((n, D_MODEL), F32),
        compiler_params=_params(("parallel",)),
        name="ffn1",
    )(x, g, wg, wu, wd)


def _out_ffn2_kernel(h_ref, a_ref, r_ref, wo_ref, g_ref, wg_ref, wu_ref, wd_ref, gf_ref, o_ref,
                     *, ck):
    h = (h_ref[...]
         + jnp.dot(a_ref[...], wo_ref[:MLA_HEADS * VDIM, :], preferred_element_type=F32)
         + jnp.dot(r_ref[...], wo_ref[MLA_HEADS * VDIM:, :], preferred_element_type=F32))
    n_bf = _rms(h, g_ref[...]).astype(BF16)
    h = h + 0.5 * _swiglu_acc(n_bf, wg_ref, wu_ref, wd_ref, ck)
    o_ref[...] = _rms(h, gf_ref[...])


def _out_ffn2(h, a, r, wo, g, wg, wu, wd, gf, tm=512, ck=256):
    n = h.shape[0]
    row = pl.BlockSpec((tm, D_MODEL), lambda i: (i, 0))
    half = pl.BlockSpec((tm, D_RET), lambda i: (i, 0))
    return pl.pallas_call(
        functools.partial(_out_ffn2_kernel, ck=ck),
        grid=(n // tm,),
        in_specs=[row, half, half, _const_spec((MLA_HEADS * VDIM + D_RET, D_MODEL)),
                  _const_spec((1, D_MODEL)), _const_spec((D_MODEL, D_FF)),
                  _const_spec((D_MODEL, D_FF)), _const_spec((D_FF, D_MODEL)),
                  _const_spec((1, D_MODEL))],
        out_specs=row,
        out_shape=jax.ShapeDtypeStruct((n, D_MODEL), F32),
        compiler_params=_params(("parallel",)),
        name="out_ffn2",
    )(h, a, r, wo, g, wg, wu, wd, gf)


def _proj_kernel(h_ref, pos_ref, inv_ref, pc_ref, ps_ref, gm_ref, win_ref, gq_ref, wq_ref, gkv_ref,
                 wkv_ref, qm_ref, km_ref, vm_ref, rq_ref, rk_ref, rv_ref, rg_ref):
    n_bf = _rms(h_ref[...], gm_ref[...]).astype(BF16)

    def proj(off, width):
        return jnp.dot(n_bf, win_ref[:, off:off + width], preferred_element_type=F32)

    def ret_rope(x, out_ref):
        for u in range(D_RET // QUAD):
            x1 = x[:, u * QUAD:u * QUAD + LANES]
            x2 = x[:, u * QUAD + LANES:(u + 1) * QUAD]
            out_ref[:, u * QUAD:u * QUAD + LANES] = (x1 * cr - x2 * sr).astype(BF16)
            out_ref[:, u * QUAD + LANES:(u + 1) * QUAD] = (x1 * sr + x2 * cr).astype(BF16)

    cq_bf = _rms(proj(_OFF_CQ, Q_LORA), gq_ref[...]).astype(BF16)
    lat = proj(_OFF_CKV, 4 * LANES)
    ckv_bf = _rms(lat[:, :KV_LORA], gkv_ref[...]).astype(BF16)
    xq = proj(_OFF_RQ, D_RET)

    cos_t, sin_t = _rope_tables(pos_ref[...], inv_ref[...], pc_ref[...], ps_ref[...])
    cr, sr = cos_t[:, :LANES], sin_t[:, :LANES]
    nope_lane = lax.broadcasted_iota(jnp.int32, (1, LANES), 1) < NOPE
    cm, sm = cos_t[:, LANES:] + jnp.where(nope_lane, 1.0, 0.0), sin_t[:, LANES:]

    xk = proj(_OFF_RK, D_RET)
    ret_rope(xq, rq_ref)
    rv_ref[...] = proj(_OFF_RV, D_RET).astype(BF16)
    ret_rope(xk, rk_ref)
    g = proj(_OFF_RG, D_RET)
    rg_ref[...] = (g * jax.nn.sigmoid(g)).astype(BF16)

    qq = jnp.dot(cq_bf, wq_ref[...], preferred_element_type=F32)
    kv = jnp.dot(ckv_bf, wkv_ref[...], preferred_element_type=F32)
    nh = MLA_HEADS * LANES
    kpe = lat[:, LANES:2 * LANES] * cm + lat[:, 2 * LANES:3 * LANES] * sm
    cq_s, sq_s = cm * (MLA_SCALE * LOG2E), sm * (MLA_SCALE * LOG2E)
    for h in range(MLA_HEADS):
        sl = slice(h * LANES, (h + 1) * LANES)
        sw = slice(nh + h * LANES, nh + (h + 1) * LANES)
        qm_ref[sl, :] = (qq[:, sl] * cq_s + qq[:, sw] * sq_s).astype(BF16).T
        km_ref[:, sl] = (kv[:, sl] + kpe).astype(BF16)
    lane = lax.broadcasted_iota(jnp.int32, (1, nh), 1)
    ones_lane = jnp.where(lane % LANES == VDIM, 1.0, 0.0)
    vm_ref[...] = (kv[:, nh:] + ones_lane).astype(BF16).T


def _proj(h, pos_row, gm, win, gq, wq, gkv, wkv, tm=512):
    n = h.shape[0]
    row = lambda w: pl.BlockSpec((tm, w), lambda i: (i, 0))
    sds = lambda w: jax.ShapeDtypeStruct((n, w), BF16)
    col = pl.BlockSpec((MLA_HEADS * LANES, tm), lambda i: (0, i))
    sds_t = jax.ShapeDtypeStruct((MLA_HEADS * LANES, n), BF16)
    inv_col, pc, ps = _rope_constants()
    return pl.pallas_call(
        _proj_kernel,
        grid=(n // tm,),
        in_specs=[row(D_MODEL), pl.BlockSpec((1, tm), lambda i: (0, i)),
                  _const_spec((N_FREQ, 1)), _const_spec((N_FREQ, 2 * LANES)),
                  _const_spec((N_FREQ, 2 * LANES)), _const_spec((1, D_MODEL)),
                  _const_spec((D_MODEL, D_IN_AUG)), _const_spec((1, Q_LORA)),
                  _const_spec((Q_LORA, 2 * MLA_HEADS * LANES)),
                  _const_spec((1, KV_LORA)), _const_spec((KV_LORA, 2 * MLA_HEADS * LANES))],
        out_specs=[col, row(1024), col, row(512), row(512), row(512), row(512)],
        out_shape=[sds_t, sds(1024), sds_t, sds(512), sds(512), sds(512), sds(512)],
        compiler_params=_params(("parallel",)),
        name="mixer_proj",
    )(h, pos_row, inv_col, pc, ps, gm, win, gq, wq, gkv, wkv)


V_ROWS = 80


def _mla_kernel(qt_ref, k_ref, vt_ref, o_ref, m_ref, acc_ref, s0_ref, s1_ref, mb0_ref, mb1_ref,
                al0_ref, al1_ref, *, tq, tk, tqs, seq, unroll):
    nk = seq // tk
    slots = ((s0_ref, mb0_ref, al0_ref), (s1_ref, mb1_ref, al1_ref))
    m_ref[...] = jnp.full(m_ref.shape, -jnp.inf, F32)
    acc_ref[...] = jnp.zeros(acc_ref.shape, F32)

    def scores(c, slot, qs):
        s_ref, mb_ref, al_ref = slots[slot]
        off = c * tk if isinstance(c, int) else pl.multiple_of(c * tk, tk)
        for h in range(2):
            k = k_ref[0, pl.ds(off, tk), h * LANES:(h + 1) * LANES]
            s = jnp.dot(k, qt_ref[h * LANES:(h + 1) * LANES, qs], preferred_element_type=F32)
            s_ref[h, :, qs] = s
            m_old = m_ref[h, :, qs]
            m_new = jnp.maximum(m_old, jnp.max(s, axis=0, keepdims=True))
            m_ref[h, :, qs] = m_new
            mb_ref[h, :, qs] = m_new
            al_ref[h, :, qs] = jnp.exp2(m_old - m_new)

    def weighted_values(c, slot, qs):
        s_ref, mb_ref, al_ref = slots[slot]
        off = c * tk if isinstance(c, int) else pl.multiple_of(c * tk, tk)
        for h in range(2):
            p = jnp.exp2(s_ref[h, :, qs] - mb_ref[h, 0:1, qs])
            vt = vt_ref[h * LANES:h * LANES + V_ROWS, pl.ds(off, tk)]
            acc_ref[h, :, qs] = (al_ref[h, 0:1, qs] * acc_ref[h, :, qs]
                                 + jnp.dot(vt, p.astype(BF16), preferred_element_type=F32))

    q_slices = [slice(i * tqs, (i + 1) * tqs) for i in range(tq // tqs)]

    def chunk_group(c0, last):
        for j in range(unroll):
            for qs in q_slices:
                if not (last and j == unroll - 1):
                    scores(c0 + j + 1, (j + 1) % 2, qs)
                weighted_values(c0 + j, j % 2, qs)

    for qs in q_slices:
        scores(0, 0, qs)

    def body(i, carry):
        chunk_group(i * unroll, False)
        return carry

    lax.fori_loop(0, nk // unroll - 1, body, 0)
    chunk_group(nk - unroll, True)

    outs = []
    for h in range(2):
        acc = acc_ref[h]
        outs.append(acc[:VDIM, :] / acc[VDIM:VDIM + 1, :])
    o_ref[0] = jnp.concatenate(outs, axis=0).T.astype(BF16)


def _mla_attention(qt, k, vt, tq=1024, tk=512, tqs=256, unroll=4):
    b, s, _ = k.shape
    npair = MLA_HEADS // 2
    assert unroll % 2 == 0 and (s // tk) % unroll == 0
    nq = s // tq
    rowstate = pltpu.VMEM((2, 8, tq), F32)
    sbuf = pltpu.VMEM((2, tk, tq), F32)
    return pl.pallas_call(
        functools.partial(_mla_kernel, tq=tq, tk=tk, tqs=tqs, seq=s, unroll=unroll),
        grid=(b, npair, nq),
        in_specs=[pl.BlockSpec((2 * LANES, tq), lambda bi, p, i: (p, bi * nq + i)),
                  pl.BlockSpec((1, s, 2 * LANES), lambda bi, p, i: (bi, 0, p)),
                  pl.BlockSpec((2 * LANES, s), lambda bi, p, i: (p, bi))],
        out_specs=pl.BlockSpec((1, tq, LANES), lambda bi, p, i: (bi, i, p)),
        out_shape=jax.ShapeDtypeStruct((b, s, MLA_HEADS * VDIM), BF16),
        scratch_shapes=[rowstate, pltpu.VMEM((2, V_ROWS, tq), F32), sbuf, sbuf,
                        rowstate, rowstate, rowstate, rowstate],
        compiler_params=_params(("parallel", "parallel", "parallel")),
        name="mla_attention",
    )(qt, k, vt)


def _log_sigmoid(x):
    return jnp.minimum(x, 0.0) - jnp.log1p(jnp.exp(-jnp.abs(x)))


def _retention_kernel(q_ref, k_ref, v_ref, g_ref, dec_ref, o_ref, acc_lo_ref, acc_hi_ref,
                      stf_ref, stb_ref, dmf_ref, dmb_ref, tab_ref, ones_ref, *, ch, seq, unroll):
    nc = seq // ch
    half = nc // 2
    nh = QUAD // RET_DIM
    lane = lax.broadcasted_iota(jnp.int32, (ch, QUAD), 1)
    row = lax.broadcasted_iota(jnp.int32, (ch, QUAD), 0).astype(F32)
    khead = (lane >> 5) & (nh - 1)
    vhead = lane >> 6
    ri = lax.broadcasted_iota(jnp.int32, (ch, ch), 0)
    ci = lax.broadcasted_iota(jnp.int32, (ch, ch), 1)
    sd = lax.broadcasted_iota(jnp.int32, (QUAD, QUAD), 0)
    se = lax.broadcasted_iota(jnp.int32, (QUAD, QUAD), 1)
    state_mask = ((sd >> 5) & (nh - 1)) == (se >> 6)
    ones_ref[...] = jnp.where((sd >> 6) == (se >> 6), 1.0, 0.0).astype(BF16)

    lg = _log_sigmoid(dec_ref[...])
    tab_ref[0] = jnp.exp(lg[0:1] * (row + 1.0))
    tab_ref[1] = jnp.exp(lg[1:2] * (ch - 1.0 - row))
    tab_ref[2] = jnp.exp(lg[2:3] * (ch - row))
    tab_ref[3] = jnp.exp(lg[3:4] * row)
    c_dec = (jnp.exp(lg[0:1] * float(ch)), jnp.exp(lg[2:3] * float(ch)))
    rel_f = jnp.maximum(ri - ci, 0).astype(F32)
    rel_b = jnp.maximum(ci - ri, 0).astype(F32)
    for h in range(nh):
        dmf_ref[h] = jnp.where(ri >= ci, jnp.exp(lg[0:1, h * RET_DIM:h * RET_DIM + 1] * rel_f), 0.0)
        dmb_ref[h] = jnp.where(ci > ri, jnp.exp(lg[2:3, h * RET_DIM:h * RET_DIM + 1] * rel_b), 0.0)
    stf_ref[...] = jnp.zeros_like(stf_ref)
    stb_ref[...] = jnp.zeros_like(stb_ref)

    def chunks(items):
        loaded = []
        for c, _ in items:
            off = pl.multiple_of(c * ch, ch)
            loaded.append((q_ref[0, pl.ds(off, ch), :], k_ref[0, pl.ds(off, ch), :],
                           v_ref[0, pl.ds(off, ch), :]))
        scores = []
        for qc, kc, _ in loaded:
            zero = jnp.zeros_like(qc)
            scores.append([lax.dot_general(jnp.where(khead == h, qc, zero), kc,
                                           (((1,), (1,)), ((), ())), preferred_element_type=F32)
                           for h in range(nh)])
        outs = []
        for (_, fwd), (_, _, vc), ss in zip(items, loaded, scores):
            dm_ref = dmf_ref if fwd else dmb_ref
            out = None
            for h in range(nh):
                ih = jnp.dot((ss[h] * dm_ref[h]).astype(BF16), vc, preferred_element_type=F32)
                out = ih if out is None else jnp.where(vhead == h, ih, out)
            outs.append(out)
        upds = []
        for (_, fwd), (_, kc, vc) in zip(items, loaded):
            kd = (kc.astype(F32) * tab_ref[1 if fwd else 3]).astype(BF16)
            upds.append(lax.dot_general(kd, vc, (((0,), (0,)), ((), ())),
                                        preferred_element_type=F32))
        for i, ((_, fwd), (qc, _, _)) in enumerate(zip(items, loaded)):
            st_ref = stf_ref if fwd else stb_ref
            state = st_ref[...]
            cross = jnp.dot(qc, state.astype(BF16), preferred_element_type=F32)
            outs[i] = outs[i] + cross * tab_ref[0 if fwd else 2]
            st_ref[...] = state * c_dec[0 if fwd else 1] + jnp.where(state_mask, upds[i], 0.0)
        return outs

    def head_sum(x):
        hi = x.astype(BF16)
        lo = (x - hi.astype(F32)).astype(BF16)
        return (jnp.dot(hi, ones_ref[...], preferred_element_type=F32)
                + jnp.dot(lo, ones_ref[...], preferred_element_type=F32))

    def finish(c, tot):
        off = pl.multiple_of(c * ch, ch)
        d = tot - head_sum(tot) * (1.0 / RET_DIM)
        var = head_sum(d * d) * (1.0 / RET_DIM)
        gate = g_ref[0, pl.ds(off, ch), :].astype(F32)
        o_ref[0, pl.ds(off, ch), :] = (d * lax.rsqrt(var + EPS) * gate).astype(BF16)

    def rows(ref, c):
        return ref.at[pl.ds(pl.multiple_of(c * ch, ch), ch), :]

    def pairs(t):
        items = []
        for j in range(unroll):
            cf = t * unroll + j
            items += [(cf, True), (nc - 1 - cf, False)]
        return items

    def first(t, carry):
        items = pairs(t)
        for (c, fwd), out in zip(items, chunks(items)):
            if fwd:
                rows(acc_lo_ref, c)[...] = out
            else:
                rows(acc_hi_ref, c - half)[...] = out
        return carry

    lax.fori_loop(0, half // unroll, first, 0)

    def second(t, carry):
        items = pairs(t)
        for (c, fwd), out in zip(items, chunks(items)):
            other = rows(acc_hi_ref, c - half) if fwd else rows(acc_lo_ref, c)
            finish(c, out + other[...])
        return carry

    lax.fori_loop(half // unroll, nc // unroll, second, 0)


def _retention(q, k, v, g, dec, ch=256, unroll=2):
    b, s, _ = q.shape
    nquad = D_RET // QUAD
    assert (s // ch // 2) % unroll == 0
    blk = pl.BlockSpec((1, s, QUAD), lambda bi, u: (bi, 0, u))
    return pl.pallas_call(
        functools.partial(_retention_kernel, ch=ch, seq=s, unroll=unroll),
        grid=(b, nquad),
        in_specs=[blk, blk, blk, blk, pl.BlockSpec((4, QUAD), lambda bi, u: (0, u))],
        out_specs=blk,
        out_shape=jax.ShapeDtypeStruct((b, s, D_RET), BF16),
        scratch_shapes=[pltpu.VMEM((s // 2, QUAD), F32), pltpu.VMEM((s // 2, QUAD), F32),
                        pltpu.VMEM((QUAD, QUAD), F32), pltpu.VMEM((QUAD, QUAD), F32),
                        pltpu.VMEM((QUAD // RET_DIM, ch, ch), F32),
                        pltpu.VMEM((QUAD // RET_DIM, ch, ch), F32),
                        pltpu.VMEM((4, ch, QUAD), F32), pltpu.VMEM((QUAD, QUAD), BF16)],
        compiler_params=_params(("parallel", "parallel")),
        name="retention",
    )(q, k, v, g, dec)


def _ret_qk_perm():
    new = np.arange(D_RET)
    quad, j = new // QUAD, new % QUAD
    head = 4 * quad + (j % LANES) // 32
    dim = (j // LANES) * 32 + j % 32
    return head * RET_DIM + dim


def _ret_lane_heads():
    lane = np.arange(D_RET)
    v_head = lane // RET_DIM
    k_head = 4 * (lane // QUAD) + ((lane % QUAD) % LANES) // 32
    return v_head, k_head


def _prep_w_in(w_in):
    c_q, c_kv, k_pe, r_q, r_k, r_v, r_g = jnp.split(
        w_in, np.cumsum([Q_LORA, KV_LORA, ROPE, D_RET, D_RET, D_RET])[:], axis=1)
    k = w_in.shape[0]
    z = lambda w: jnp.zeros((k, w), w_in.dtype)
    kpe = jnp.concatenate([z(64), k_pe, z(32)], axis=1)
    kpe_sw = jnp.concatenate([z(64), k_pe[:, 16:], k_pe[:, :16], z(32)], axis=1)
    perm = _ret_qk_perm()
    return jnp.concatenate([c_q, c_kv, kpe, kpe_sw, z(LANES), r_q[:, perm],
                            r_k[:, perm] * (RET_DIM ** -0.5), r_v, r_g], axis=1).astype(BF16)


def _prep_w_uq(w_uq):
    k = w_uq.shape[0]
    w3 = w_uq.reshape(k, MLA_HEADS, NOPE + ROPE)
    nope, x1, x2 = w3[:, :, :NOPE], w3[:, :, NOPE:NOPE + 16], w3[:, :, NOPE + 16:]
    z32 = jnp.zeros((k, MLA_HEADS, 32), w_uq.dtype)
    wq = jnp.concatenate([nope, x1, x2, z32], axis=2).reshape(k, MLA_HEADS * LANES)
    wqs = jnp.concatenate([jnp.zeros_like(nope), x2, x1, z32], axis=2).reshape(k, MLA_HEADS * LANES)
    return jnp.concatenate([wq, wqs], axis=1).astype(BF16)


def _prep_w_ukv(w_ukv):
    k = w_ukv.shape[0]
    w3 = w_ukv.reshape(k, MLA_HEADS, NOPE + VDIM)
    kn = jnp.concatenate([w3[:, :, :NOPE], jnp.zeros((k, MLA_HEADS, 64), w_ukv.dtype)], axis=2)
    v = jnp.concatenate([w3[:, :, NOPE:], jnp.zeros((k, MLA_HEADS, 64), w_ukv.dtype)], axis=2)
    return jnp.concatenate([kn.reshape(k, MLA_HEADS * LANES), v.reshape(k, MLA_HEADS * LANES)],
                           axis=1).astype(BF16)


def _prep_decay(dec_f, dec_b):
    v_head, k_head = _ret_lane_heads()
    f, b = dec_f.astype(F32), dec_b.astype(F32)
    return jnp.stack([f[v_head], f[k_head], b[v_head], b[k_head]])


def kernel(x, positions, ffn1_norm, ffn1_w_gate, ffn1_w_up, ffn1_w_down, mix_norm, w_in, q_norm, w_uq, kv_norm, w_ukv, ret_decay_fwd, ret_decay_bwd, w_o, ffn2_norm, ffn2_w_gate, ffn2_w_up, ffn2_w_down, final_norm):
    b, s, d = x.shape
    n = b * s
    assert ffn1_norm.shape[0] == 1, "specialised to DEPTH == 1 (the final norm is fused into the layer)"
    pos_row = positions.astype(F32).reshape(1, n)

    h = _ffn1(x.reshape(n, d), ffn1_norm[0][None, :], ffn1_w_gate[0], ffn1_w_up[0],
              ffn1_w_down[0])
    qm, km, vm, rq, rk, rv, rg = _proj(
        h, pos_row, mix_norm[0][None, :], _prep_w_in(w_in[0]), q_norm[0][None, :],
        _prep_w_uq(w_uq[0]), kv_norm[0][None, :], _prep_w_ukv(w_ukv[0]))
    a = _mla_attention(qm, km.reshape(b, s, -1), vm)
    r = _retention(rq.reshape(b, s, -1), rk.reshape(b, s, -1), rv.reshape(b, s, -1),
                   rg.reshape(b, s, -1), _prep_decay(ret_decay_fwd[0], ret_decay_bwd[0]))
    wo = w_o[0].astype(BF16)
    out = _out_ffn2(h, a.reshape(n, -1), r.reshape(n, -1), wo, ffn2_norm[0][None, :],
                    ffn2_w_gate[0], ffn2_w_up[0], ffn2_w_down[0], final_norm[None, :])
    return out.reshape(b, s, d)
```

```python
import functools
import math

import numpy as np
import jax
import jax.numpy as jnp
from jax import lax
from jax.experimental import pallas as pl
from jax.experimental.pallas import tpu as pltpu

F32 = jnp.float32
BF16 = jnp.bfloat16

D_MODEL = 1024
D_FF = 2816
MLA_HEADS = 8
Q_LORA = 256
KV_LORA = 128
NOPE = 64
ROPE = 32
VDIM = 64
RET_HEADS = 8
RET_DIM = 64
D_RET = RET_HEADS * RET_DIM
ROPE_THETA = 10000.0
EPS = 1e-6
MLA_SCALE = (NOPE + ROPE) ** -0.5
LOG2E = math.log2(math.e)

LANES = 128
QUAD = 256
VMEM_LIMIT = 56 * 1024 * 1024

_OFF_CQ = 0
_OFF_CKV = 256
_OFF_RQ = 768
_OFF_RK = 1280
_OFF_RV = 1792
_OFF_RG = 2304
D_IN_AUG = 2816


def _params(sem):
    return pltpu.CompilerParams(dimension_semantics=sem, vmem_limit_bytes=VMEM_LIMIT)


def _const_spec(shape):
    nd = len(shape)
    return pl.BlockSpec(shape, lambda *_: (0,) * nd, pipeline_mode=pl.Buffered(1))


N_FREQ = 64


def _rope_constants():
    row = np.arange(N_FREQ)
    inv = np.where(row < 32, ROPE_THETA ** (-(row % 32) / 32.0),
                   np.where(row < 48, ROPE_THETA ** (-((row - 32) % 16) / 16.0), 0.0))
    pc = np.zeros((N_FREQ, 2 * LANES))
    ps = np.zeros((N_FREQ, 2 * LANES))
    for l in range(LANES):
        pc[l % 32, l] = 1.0
        ps[l % 32, l] = 1.0
    for l in range(NOPE, NOPE + ROPE):
        r = 32 + (l - NOPE) % 16
        pc[r, LANES + l] = 1.0
        ps[r, LANES + l] = -1.0 if l < NOPE + 16 else 1.0
    return jnp.asarray(inv[:, None], F32), jnp.asarray(pc, BF16), jnp.asarray(ps, BF16)


def _rope_tables(pos_row, inv_col, pc, ps):
    ang = inv_col * pos_row
    dn = (((0,), (0,)), ((), ()))

    def place(x, p):
        hi = x.astype(BF16)
        lo = (x - hi.astype(F32)).astype(BF16)
        return (lax.dot_general(hi, p, dn, preferred_element_type=F32)
                + lax.dot_general(lo, p, dn, preferred_element_type=F32))

    return place(jnp.cos(ang), pc), place(jnp.sin(ang), ps)


def _rms(x, g):
    return x * lax.rsqrt(jnp.mean(x * x, axis=-1, keepdims=True) + EPS) * g


def _swiglu_acc(n_bf, wg_ref, wu_ref, wd_ref, ck):
    acc = None
    for j in range(D_FF // ck):
        sl = slice(j * ck, (j + 1) * ck)
        g = jnp.dot(n_bf, wg_ref[:, sl].astype(BF16), preferred_element_type=F32)
        u = jnp.dot(n_bf, wu_ref[:, sl].astype(BF16), preferred_element_type=F32)
        hm = (g * jax.nn.sigmoid(g) * u).astype(BF16)
        d = jnp.dot(hm, wd_ref[sl, :].astype(BF16), preferred_element_type=F32)
        acc = d if acc is None else acc + d
    return acc


def _ffn1_kernel(x_ref, g_ref, wg_ref, wu_ref, wd_ref, o_ref, *, ck):
    x = x_ref[...]
    n_bf = _rms(x, g_ref[...]).astype(BF16)
    o_ref[...] = x + 0.5 * _swiglu_acc(n_bf, wg_ref, wu_ref, wd_ref, ck)


def _ffn1(x, g, wg, wu, wd, tm=512, ck=256):
    n = x.shape[0]
    row = pl.BlockSpec((tm, D_MODEL), lambda i: (i, 0))
    return pl.pallas_call(
        functools.partial(_ffn1_kernel, ck=ck),
        grid=(n // tm,),
        in_specs=[row, _const_spec((1, D_MODEL)), _const_spec((D_MODEL, D_FF)),
                  _const_spec((D_MODEL, D_FF)), _const_spec((D_FF, D_MODEL))],
        out_specs=row,
        out_shape=jax.ShapeDtypeStruct((n, D_MODEL), F32),
        compiler_params=_params(("parallel",)),
        name="ffn1",
    )(x, g, wg, wu, wd)


def _out_ffn2_kernel(h_ref, a_ref, r_ref, wo_ref, g_ref, wg_ref, wu_ref, wd_ref, gf_ref, o_ref,
                     *, ck):
    h = (h_ref[...]
         + jnp.dot(a_ref[...], wo_ref[:MLA_HEADS * VDIM, :], preferred_element_type=F32)
         + jnp.dot(r_ref[...], wo_ref[MLA_HEADS * VDIM:, :], preferred_element_type=F32))
    n_bf = _rms(h, g_ref[...]).astype(BF16)
    h = h + 0.5 * _swiglu_acc(n_bf, wg_ref, wu_ref, wd_ref, ck)
    o_ref[...] = _rms(h, gf_ref[...])


def _out_ffn2(h, a, r, wo, g, wg, wu, wd, gf, tm=512, ck=256):
    n = h.shape[0]
    row = pl.BlockSpec((tm, D_MODEL), lambda i: (i, 0))
    half = pl.BlockSpec((tm, D_RET), lambda i: (i, 0))
    return pl.pallas_call(
        functools.partial(_out_ffn2_kernel, ck=ck),
        grid=(n // tm,),
        in_specs=[row, half, half, _const_spec((MLA_HEADS * VDIM + D_RET, D_MODEL)),
                  _const_spec((1, D_MODEL)), _const_spec((D_MODEL, D_FF)),
                  _const_spec((D_MODEL, D_FF)), _const_spec((D_FF, D_MODEL)),
                  _const_spec((1, D_MODEL))],
        out_specs=row,
        out_shape=jax.ShapeDtypeStruct((n, D_MODEL), F32),
        compiler_params=_params(("parallel",)),
        name="out_ffn2",
    )(h, a, r, wo, g, wg, wu, wd, gf)


def _proj_kernel(h_ref, pos_ref, inv_ref, pc_ref, ps_ref, gm_ref, win_ref, gq_ref, wq_ref, gkv_ref,
                 wkv_ref, qm_ref, km_ref, vm_ref, rq_ref, rk_ref, rv_ref, rg_ref):
    n_bf = _rms(h_ref[...], gm_ref[...]).astype(BF16)

    def proj(off, width):
        return jnp.dot(n_bf, win_ref[:, off:off + width], preferred_element_type=F32)

    def ret_rope(x, out_ref):
        for u in range(D_RET // QUAD):
            x1 = x[:, u * QUAD:u * QUAD + LANES]
            x2 = x[:, u * QUAD + LANES:(u + 1) * QUAD]
            out_ref[:, u * QUAD:u * QUAD + LANES] = (x1 * cr - x2 * sr).astype(BF16)
            out_ref[:, u * QUAD + LANES:(u + 1) * QUAD] = (x1 * sr + x2 * cr).astype(BF16)

    cq_bf = _rms(proj(_OFF_CQ, Q_LORA), gq_ref[...]).astype(BF16)
    lat = proj(_OFF_CKV, 4 * LANES)
    ckv_bf = _rms(lat[:, :KV_LORA], gkv_ref[...]).astype(BF16)
    xq = proj(_OFF_RQ, D_RET)

    cos_t, sin_t = _rope_tables(pos_ref[...], inv_ref[...], pc_ref[...], ps_ref[...])
    cr, sr = cos_t[:, :LANES], sin_t[:, :LANES]
    nope_lane = lax.broadcasted_iota(jnp.int32, (1, LANES), 1) < NOPE
    cm, sm = cos_t[:, LANES:] + jnp.where(nope_lane, 1.0, 0.0), sin_t[:, LANES:]

    xk = proj(_OFF_RK, D_RET)
    ret_rope(xq, rq_ref)
    rv_ref[...] = proj(_OFF_RV, D_RET).astype(BF16)
    ret_rope(xk, rk_ref)
    g = proj(_OFF_RG, D_RET)
    rg_ref[...] = (g * jax.nn.sigmoid(g)).astype(BF16)

    qq = jnp.dot(cq_bf, wq_ref[...], preferred_element_type=F32)
    kv = jnp.dot(ckv_bf, wkv_ref[...], preferred_element_type=F32)
    nh = MLA_HEADS * LANES
    kpe = lat[:, LANES:2 * LANES] * cm + lat[:, 2 * LANES:3 * LANES] * sm
    cq_s, sq_s = cm * (MLA_SCALE * LOG2E), sm * (MLA_SCALE * LOG2E)
    for h in range(MLA_HEADS):
        sl = slice(h * LANES, (h + 1) * LANES)
        sw = slice(nh + h * LANES, nh + (h + 1) * LANES)
        qm_ref[sl, :] = (qq[:, sl] * cq_s + qq[:, sw] * sq_s).astype(BF16).T
        km_ref[:, sl] = (kv[:, sl] + kpe).astype(BF16)
    lane = lax.broadcasted_iota(jnp.int32, (1, nh), 1)
    ones_lane = jnp.where(lane % LANES == VDIM, 1.0, 0.0)
    vm_ref[...] = (kv[:, nh:] + ones_lane).astype(BF16).T


def _proj(h, pos_row, gm, win, gq, wq, gkv, wkv, tm=512):
    n = h.shape[0]
    row = lambda w: pl.BlockSpec((tm, w), lambda i: (i, 0))
    sds = lambda w: jax.ShapeDtypeStruct((n, w), BF16)
    col = pl.BlockSpec((MLA_HEADS * LANES, tm), lambda i: (0, i))
    sds_t = jax.ShapeDtypeStruct((MLA_HEADS * LANES, n), BF16)
    inv_col, pc, ps = _rope_constants()
    return pl.pallas_call(
        _proj_kernel,
        grid=(n // tm,),
        in_specs=[row(D_MODEL), pl.BlockSpec((1, tm), lambda i: (0, i)),
                  _const_spec((N_FREQ, 1)), _const_spec((N_FREQ, 2 * LANES)),
                  _const_spec((N_FREQ, 2 * LANES)), _const_spec((1, D_MODEL)),
                  _const_spec((D_MODEL, D_IN_AUG)), _const_spec((1, Q_LORA)),
                  _const_spec((Q_LORA, 2 * MLA_HEADS * LANES)),
                  _const_spec((1, KV_LORA)), _const_spec((KV_LORA, 2 * MLA_HEADS * LANES))],
        out_specs=[col, row(1024), col, row(512), row(512), row(512), row(512)],
        out_shape=[sds_t, sds(1024), sds_t, sds(512), sds(512), sds(512), sds(512)],
        compiler_params=_params(("parallel",)),
        name="mixer_proj",
    )(h, pos_row, inv_col, pc, ps, gm, win, gq, wq, gkv, wkv)


V_ROWS = 80


def _mla_kernel(qt_ref, k_ref, vt_ref, o_ref, m_ref, acc_ref, s0_ref, s1_ref, mb0_ref, mb1_ref,
                al0_ref, al1_ref, *, tq, tk, tqs, seq, unroll):
    nk = seq // tk
    slots = ((s0_ref, mb0_ref, al0_ref), (s1_ref, mb1_ref, al1_ref))
    m_ref[...] = jnp.full(m_ref.shape, -jnp.inf, F32)
    acc_ref[...] = jnp.zeros(acc_ref.shape, F32)

    def scores(c, slot, qs):
        s_ref, mb_ref, al_ref = slots[slot]
        off = c * tk if isinstance(c, int) else pl.multiple_of(c * tk, tk)
        for h in range(2):
            k = k_ref[0, pl.ds(off, tk), h * LANES:(h + 1) * LANES]
            s = jnp.dot(k, qt_ref[h * LANES:(h + 1) * LANES, qs], preferred_element_type=F32)
            s_ref[h, :, qs] = s
            m_old = m_ref[h, :, qs]
            m_new = jnp.maximum(m_old, jnp.max(s, axis=0, keepdims=True))
            m_ref[h, :, qs] = m_new
            mb_ref[h, :, qs] = m_new
            al_ref[h, :, qs] = jnp.exp2(m_old - m_new)

    def weighted_values(c, slot, qs):
        s_ref, mb_ref, al_ref = slots[slot]
        off = c * tk if isinstance(c, int) else pl.multiple_of(c * tk, tk)
        for h in range(2):
            p = jnp.exp2(s_ref[h, :, qs] - mb_ref[h, 0:1, qs])
            vt = vt_ref[h * LANES:h * LANES + V_ROWS, pl.ds(off, tk)]
            acc_ref[h, :, qs] = (al_ref[h, 0:1, qs] * acc_ref[h, :, qs]
                                 + jnp.dot(vt, p.astype(BF16), preferred_element_type=F32))

    q_slices = [slice(i * tqs, (i + 1) * tqs) for i in range(tq // tqs)]

    def chunk_group(c0, last):
        for j in range(unroll):
            for qs in q_slices:
                if not (last and j == unroll - 1):
                    scores(c0 + j + 1, (j + 1) % 2, qs)
                weighted_values(c0 + j, j % 2, qs)

    for qs in q_slices:
        scores(0, 0, qs)

    def body(i, carry):
        chunk_group(i * unroll, False)
        return carry

    lax.fori_loop(0, nk // unroll - 1, body, 0)
    chunk_group(nk - unroll, True)

    outs = []
    for h in range(2):
        acc = acc_ref[h]
        outs.append(acc[:VDIM, :] / acc[VDIM:VDIM + 1, :])
    o_ref[0] = jnp.concatenate(outs, axis=0).T.astype(BF16)


def _mla_attention(qt, k, vt, tq=2048, tk=512, tqs=256, unroll=2):
    b, s, _ = k.shape
    npair = MLA_HEADS // 2
    assert unroll % 2 == 0 and (s // tk) % unroll == 0
    nq = s // tq
    rowstate = pltpu.VMEM((2, 8, tq), F32)
    sbuf = pltpu.VMEM((2, tk, tq), F32)
    return pl.pallas_call(
        functools.partial(_mla_kernel, tq=tq, tk=tk, tqs=tqs, seq=s, unroll=unroll),
        grid=(b, npair, nq),
        in_specs=[pl.BlockSpec((2 * LANES, tq), lambda bi, p, i: (p, bi * nq + i)),
                  pl.BlockSpec((1, s, 2 * LANES), lambda bi, p, i: (bi, 0, p)),
                  pl.BlockSpec((2 * LANES, s), lambda bi, p, i: (p, bi))],
        out_specs=pl.BlockSpec((1, tq, LANES), lambda bi, p, i: (bi, i, p)),
        out_shape=jax.ShapeDtypeStruct((b, s, MLA_HEADS * VDIM), BF16),
        scratch_shapes=[rowstate, pltpu.VMEM((2, V_ROWS, tq), F32), sbuf, sbuf,
                        rowstate, rowstate, rowstate, rowstate],
        compiler_params=_params(("parallel", "parallel", "parallel")),
        name="mla_attention",
    )(qt, k, vt)


def _log_sigmoid(x):
    return jnp.minimum(x, 0.0) - jnp.log1p(jnp.exp(-jnp.abs(x)))


def _retention_kernel(q_ref, k_ref, v_ref, g_ref, dec_ref, o_ref, acc_lo_ref, acc_hi_ref,
                      stf_ref, stb_ref, dmf_ref, dmb_ref, tab_ref, ones_ref, *, ch, seq, unroll):
    nc = seq // ch
    half = nc // 2
    nh = QUAD // RET_DIM
    lane = lax.broadcasted_iota(jnp.int32, (ch, QUAD), 1)
    row = lax.broadcasted_iota(jnp.int32, (ch, QUAD), 0).astype(F32)
    khead = (lane >> 5) & (nh - 1)
    vhead = lane >> 6
    ri = lax.broadcasted_iota(jnp.int32, (ch, ch), 0)
    ci = lax.broadcasted_iota(jnp.int32, (ch, ch), 1)
    sd = lax.broadcasted_iota(jnp.int32, (QUAD, QUAD), 0)
    se = lax.broadcasted_iota(jnp.int32, (QUAD, QUAD), 1)
    state_mask = ((sd >> 5) & (nh - 1)) == (se >> 6)
    ones_ref[...] = jnp.where((sd >> 6) == (se >> 6), 1.0, 0.0).astype(BF16)

    lg = _log_sigmoid(dec_ref[...])
    tab_ref[0] = jnp.exp(lg[0:1] * (row + 1.0))
    tab_ref[1] = jnp.exp(lg[1:2] * (ch - 1.0 - row))
    tab_ref[2] = jnp.exp(lg[2:3] * (ch - row))
    tab_ref[3] = jnp.exp(lg[3:4] * row)
    c_dec = (jnp.exp(lg[0:1] * float(ch)), jnp.exp(lg[2:3] * float(ch)))
    rel_f = jnp.maximum(ri - ci, 0).astype(F32)
    rel_b = jnp.maximum(ci - ri, 0).astype(F32)
    for h in range(nh):
        dmf_ref[h] = jnp.where(ri >= ci, jnp.exp(lg[0:1, h * RET_DIM:h * RET_DIM + 1] * rel_f), 0.0)
        dmb_ref[h] = jnp.where(ci > ri, jnp.exp(lg[2:3, h * RET_DIM:h * RET_DIM + 1] * rel_b), 0.0)
    stf_ref[...] = jnp.zeros_like(stf_ref)
    stb_ref[...] = jnp.zeros_like(stb_ref)

    def chunks(items):
        loaded = []
        for c, _ in items:
            off = pl.multiple_of(c * ch, ch)
            loaded.append((q_ref[0, pl.ds(off, ch), :], k_ref[0, pl.ds(off, ch), :],
                           v_ref[0, pl.ds(off, ch), :]))
        scores = []
        for qc, kc, _ in loaded:
            zero = jnp.zeros_like(qc)
            scores.append([lax.dot_general(jnp.where(khead == h, qc, zero), kc,
                                           (((1,), (1,)), ((), ())), preferred_element_type=F32)
                           for h in range(nh)])
        outs = []
        for (_, fwd), (_, _, vc), ss in zip(items, loaded, scores):
            dm_ref = dmf_ref if fwd else dmb_ref
            out = None
            for h in range(nh):
                ih = jnp.dot((ss[h] * dm_ref[h]).astype(BF16), vc, preferred_element_type=F32)
                out = ih if out is None else jnp.where(vhead == h, ih, out)
            outs.append(out)
        upds = []
        for (_, fwd), (_, kc, vc) in zip(items, loaded):
            kd = (kc.astype(F32) * tab_ref[1 if fwd else 3]).astype(BF16)
            upds.append(lax.dot_general(kd, vc, (((0,), (0,)), ((), ())),
                                        preferred_element_type=F32))
        for i, ((_, fwd), (qc, _, _)) in enumerate(zip(items, loaded)):
            st_ref = stf_ref if fwd else stb_ref
            state = st_ref[...]
            cross = jnp.dot(qc, state.astype(BF16), preferred_element_type=F32)
            outs[i] = outs[i] + cross * tab_ref[0 if fwd else 2]
            st_ref[...] = state * c_dec[0 if fwd else 1] + jnp.where(state_mask, upds[i], 0.0)
        return outs

    def head_sum(x):
        hi = x.astype(BF16)
        lo = (x - hi.astype(F32)).astype(BF16)
        return (jnp.dot(hi, ones_ref[...], preferred_element_type=F32)
                + jnp.dot(lo, ones_ref[...], preferred_element_type=F32))

    def finish(c, tot):
        off = pl.multiple_of(c * ch, ch)
        d = tot - head_sum(tot) * (1.0 / RET_DIM)
        var = head_sum(d * d) * (1.0 / RET_DIM)
        gate = g_ref[0, pl.ds(off, ch), :].astype(F32)
        o_ref[0, pl.ds(off, ch), :] = (d * lax.rsqrt(var + EPS) * gate).astype(BF16)

    def rows(ref, c):
        return ref.at[pl.ds(pl.multiple_of(c * ch, ch), ch), :]

    def pairs(t):
        items = []
        for j in range(unroll):
            cf = t * unroll + j
            items += [(cf, True), (nc - 1 - cf, False)]
        return items

    def first(t, carry):
        items = pairs(t)
        for (c, fwd), out in zip(items, chunks(items)):
            if fwd:
                rows(acc_lo_ref, c)[...] = out
            else:
                rows(acc_hi_ref, c - half)[...] = out
        return carry

    lax.fori_loop(0, half // unroll, first, 0)

    def second(t, carry):
        items = pairs(t)
        for (c, fwd), out in zip(items, chunks(items)):
            other = rows(acc_hi_ref, c - half) if fwd else rows(acc_lo_ref, c)
            finish(c, out + other[...])
        return carry

    lax.fori_loop(half // unroll, nc // unroll, second, 0)


def _retention(q, k, v, g, dec, ch=256, unroll=2):
    b, s, _ = q.shape
    nquad = D_RET // QUAD
    assert (s // ch // 2) % unroll == 0
    blk = pl.BlockSpec((1, s, QUAD), lambda bi, u: (bi, 0, u))
    return pl.pallas_call(
        functools.partial(_retention_kernel, ch=ch, seq=s, unroll=unroll),
        grid=(b, nquad),
        in_specs=[blk, blk, blk, blk, pl.BlockSpec((4, QUAD), lambda bi, u: (0, u))],
        out_specs=blk,
        out_shape=jax.ShapeDtypeStruct((b, s, D_RET), BF16),
        scratch_shapes=[pltpu.VMEM((s // 2, QUAD), F32), pltpu.VMEM((s // 2, QUAD), F32),
                        pltpu.VMEM((QUAD, QUAD), F32), pltpu.VMEM((QUAD, QUAD), F32),
                        pltpu.VMEM((QUAD // RET_DIM, ch, ch), F32),
                        pltpu.VMEM((QUAD // RET_DIM, ch, ch), F32),
                        pltpu.VMEM((4, ch, QUAD), F32), pltpu.VMEM((QUAD, QUAD), BF16)],
        compiler_params=_params(("parallel", "parallel")),
        name="retention",
    )(q, k, v, g, dec)


def _ret_qk_perm():
    new = np.arange(D_RET)
    quad, j = new // QUAD, new % QUAD
    head = 4 * quad + (j % LANES) // 32
    dim = (j // LANES) * 32 + j % 32
    return head * RET_DIM + dim


def _ret_lane_heads():
    lane = np.arange(D_RET)
    v_head = lane // RET_DIM
    k_head = 4 * (lane // QUAD) + ((lane % QUAD) % LANES) // 32
    return v_head, k_head


def _gather_cols(w, idx, scale=None):
    idx = np.asarray(idx)
    out = jnp.take(w, jnp.asarray(np.maximum(idx, 0), jnp.int32), axis=1)
    out = jnp.where(jnp.asarray(idx >= 0)[None, :], out, 0.0)
    if scale is not None:
        out = out * jnp.asarray(scale, w.dtype)[None, :]
    return out.astype(BF16)


def _prep_w_in(w_in):
    o_kpe = Q_LORA + KV_LORA
    o_rq = o_kpe + ROPE
    o_rk, o_rv, o_rg = o_rq + D_RET, o_rq + 2 * D_RET, o_rq + 3 * D_RET
    idx = np.full(D_IN_AUG, -1)
    idx[_OFF_CQ:_OFF_CQ + o_kpe] = np.arange(o_kpe)
    idx[_OFF_CKV + LANES + NOPE:_OFF_CKV + LANES + NOPE + ROPE] = o_kpe + np.arange(ROPE)
    swapped = np.concatenate([np.arange(16, 32), np.arange(16)])
    idx[_OFF_CKV + 2 * LANES + NOPE:_OFF_CKV + 2 * LANES + NOPE + ROPE] = o_kpe + swapped
    perm = _ret_qk_perm()
    idx[_OFF_RQ:_OFF_RQ + D_RET] = o_rq + perm
    idx[_OFF_RK:_OFF_RK + D_RET] = o_rk + perm
    idx[_OFF_RV:_OFF_RV + D_RET] = o_rv + np.arange(D_RET)
    idx[_OFF_RG:_OFF_RG + D_RET] = o_rg + np.arange(D_RET)
    scale = np.ones(D_IN_AUG)
    scale[_OFF_RK:_OFF_RK + D_RET] = RET_DIM ** -0.5
    return _gather_cols(w_in, idx, scale)


def _prep_w_uq(w_uq):
    nh = MLA_HEADS * LANES
    idx = np.full(2 * nh, -1)
    for h in range(MLA_HEADS):
        src = h * (NOPE + ROPE)
        idx[h * LANES:h * LANES + NOPE + ROPE] = src + np.arange(NOPE + ROPE)
        idx[nh + h * LANES + NOPE:nh + h * LANES + NOPE + 16] = src + NOPE + 16 + np.arange(16)
        idx[nh + h * LANES + NOPE + 16:nh + h * LANES + NOPE + 32] = src + NOPE + np.arange(16)
    return _gather_cols(w_uq, idx)


def _prep_w_ukv(w_ukv):
    nh = MLA_HEADS * LANES
    idx = np.full(2 * nh, -1)
    for h in range(MLA_HEADS):
        src = h * (NOPE + VDIM)
        idx[h * LANES:h * LANES + NOPE] = src + np.arange(NOPE)
        idx[nh + h * LANES:nh + h * LANES + VDIM] = src + NOPE + np.arange(VDIM)
    return _gather_cols(w_ukv, idx)


def _prep_decay(dec_f, dec_b):
    v_head, k_head = _ret_lane_heads()
    f, b = dec_f.astype(F32), dec_b.astype(F32)
    return jnp.stack([f[v_head], f[k_head], b[v_head], b[k_head]])


def kernel(x, positions, ffn1_norm, ffn1_w_gate, ffn1_w_up, ffn1_w_down, mix_norm, w_in, q_norm, w_uq, kv_norm, w_ukv, ret_decay_fwd, ret_decay_bwd, w_o, ffn2_norm, ffn2_w_gate, ffn2_w_up, ffn2_w_down, final_norm):
    b, s, d = x.shape
    n = b * s
    assert ffn1_norm.shape[0] == 1, "specialised to DEPTH == 1 (the final norm is fused into the layer)"
    pos_row = positions.astype(F32).reshape(1, n)

    h = _ffn1(x.reshape(n, d), ffn1_norm[0][None, :], ffn1_w_gate[0], ffn1_w_up[0],
              ffn1_w_down[0])
    qm, km, vm, rq, rk, rv, rg = _proj(
        h, pos_row, mix_norm[0][None, :], _prep_w_in(w_in[0]), q_norm[0][None, :],
        _prep_w_uq(w_uq[0]), kv_norm[0][None, :], _prep_w_ukv(w_ukv[0]))
    a = _mla_attention(qm, km.reshape(b, s, -1), vm)
    r = _retention(rq.reshape(b, s, -1), rk.reshape(b, s, -1), rv.reshape(b, s, -1),
                   rg.reshape(b, s, -1), _prep_decay(ret_decay_fwd[0], ret_decay_bwd[0]))
    wo = w_o[0].astype(BF16)
    out = _out_ffn2(h, a.reshape(n, -1), r.reshape(n, -1), wo, ffn2_norm[0][None, :],
                    ffn2_w_gate[0], ffn2_w_up[0], ffn2_w_down[0], final_norm[None, :])
    return out.reshape(b, s, d)
```

```python
import functools
import math

import numpy as np
import jax
import jax.numpy as jnp
from jax import lax
from jax.experimental import pallas as pl
from jax.experimental.pallas import tpu as pltpu

F32 = jnp.float32
BF16 = jnp.bfloat16

D_MODEL = 1024
D_FF = 2816
MLA_HEADS = 8
Q_LORA = 256
KV_LORA = 128
NOPE = 64
ROPE = 32
VDIM = 64
RET_HEADS = 8
RET_DIM = 64
D_RET = RET_HEADS * RET_DIM
ROPE_THETA = 10000.0
EPS = 1e-6
MLA_SCALE = (NOPE + ROPE) ** -0.5
LOG2E = math.log2(math.e)

LANES = 128
QUAD = 256
VMEM_LIMIT = 56 * 1024 * 1024

_OFF_CQ = 0
_OFF_CKV = 256
_OFF_RQ = 768
_OFF_RK = 1280
_OFF_RV = 1792
_OFF_RG = 2304
D_IN_AUG = 2816


def _params(sem):
    return pltpu.CompilerParams(dimension_semantics=sem, vmem_limit_bytes=VMEM_LIMIT)


def _const_spec(shape):
    nd = len(shape)
    return pl.BlockSpec(shape, lambda *_: (0,) * nd, pipeline_mode=pl.Buffered(1))


N_FREQ = 64


def _rope_constants():
    row = np.arange(N_FREQ)
    inv = np.where(row < 32, ROPE_THETA ** (-(row % 32) / 32.0),
                   np.where(row < 48, ROPE_THETA ** (-((row - 32) % 16) / 16.0), 0.0))
    pc = np.zeros((N_FREQ, 2 * LANES))
    ps = np.zeros((N_FREQ, 2 * LANES))
    for l in range(LANES):
        pc[l % 32, l] = 1.0
        ps[l % 32, l] = 1.0
    for l in range(NOPE, NOPE + ROPE):
        r = 32 + (l - NOPE) % 16
        pc[r, LANES + l] = 1.0
        ps[r, LANES + l] = -1.0 if l < NOPE + 16 else 1.0
    return jnp.asarray(inv[:, None], F32), jnp.asarray(pc, BF16), jnp.asarray(ps, BF16)


def _rope_tables(pos_row, inv_col, pc, ps):
    ang = inv_col * pos_row
    dn = (((0,), (0,)), ((), ()))

    def place(x, p):
        hi = x.astype(BF16)
        lo = (x - hi.astype(F32)).astype(BF16)
        return (lax.dot_general(hi, p, dn, preferred_element_type=F32)
                + lax.dot_general(lo, p, dn, preferred_element_type=F32))

    return place(jnp.cos(ang), pc), place(jnp.sin(ang), ps)


def _rms(x, g):
    return x * lax.rsqrt(jnp.mean(x * x, axis=-1, keepdims=True) + EPS) * g


def _swiglu_acc(n_bf, wg_ref, wu_ref, wd_ref, ck):
    acc = None
    for j in range(D_FF // ck):
        sl = slice(j * ck, (j + 1) * ck)
        g = jnp.dot(n_bf, wg_ref[:, sl].astype(BF16), preferred_element_type=F32)
        u = jnp.dot(n_bf, wu_ref[:, sl].astype(BF16), preferred_element_type=F32)
        hm = (g * jax.nn.sigmoid(g) * u).astype(BF16)
        d = jnp.dot(hm, wd_ref[sl, :].astype(BF16), preferred_element_type=F32)
        acc = d if acc is None else acc + d
    return acc


def _ffn1_kernel(x_ref, g_ref, wg_ref, wu_ref, wd_ref, o_ref, *, ck):
    x = x_ref[...]
    n_bf = _rms(x, g_ref[...]).astype(BF16)
    o_ref[...] = x + 0.5 * _swiglu_acc(n_bf, wg_ref, wu_ref, wd_ref, ck)


def _ffn1(x, g, wg, wu, wd, tm=512, ck=256):
    n = x.shape[0]
    row = pl.BlockSpec((tm, D_MODEL), lambda i: (i, 0))
    return pl.pallas_call(
        functools.partial(_ffn1_kernel, ck=ck),
        grid=(n // tm,),
        in_specs=[row, _const_spec((1, D_MODEL)), _const_spec((D_MODEL, D_FF)),
                  _const_spec((D_MODEL, D_FF)), _const_spec((D_FF, D_MODEL))],
        out_specs=row,
        out_shape=jax.ShapeDtypeStruct((n, D_MODEL), F32),
        compiler_params=_params(("parallel",)),
        name="ffn1",
    )(x, g, wg, wu, wd)


def _out_ffn2_kernel(h_ref, a_ref, r_ref, wo_ref, g_ref, wg_ref, wu_ref, wd_ref, gf_ref, o_ref,
                     *, ck):
    h = (h_ref[...]
         + jnp.dot(a_ref[...], wo_ref[:MLA_HEADS * VDIM, :], preferred_element_type=F32)
         + jnp.dot(r_ref[...], wo_ref[MLA_HEADS * VDIM:, :], preferred_element_type=F32))
    n_bf = _rms(h, g_ref[...]).astype(BF16)
    h = h + 0.5 * _swiglu_acc(n_bf, wg_ref, wu_ref, wd_ref, ck)
    o_ref[...] = _rms(h, gf_ref[...])


def _out_ffn2(h, a, r, wo, g, wg, wu, wd, gf, tm=512, ck=256):
    n = h.shape[0]
    row = pl.BlockSpec((tm, D_MODEL), lambda i: (i, 0))
    half = pl.BlockSpec((tm, D_RET), lambda i: (i, 0))
    return pl.pallas_call(
        functools.partial(_out_ffn2_kernel, ck=ck),
        grid=(n // tm,),
        in_specs=[row, half, half, _const_spec((MLA_HEADS * VDIM + D_RET, D_MODEL)),
                  _const_spec((1, D_MODEL)), _const_spec((D_MODEL, D_FF)),
                  _const_spec((D_MODEL, D_FF)), _const_spec((D_FF, D_MODEL)),
                  _const_spec((1, D_MODEL))],
        out_specs=row,
        out_shape=jax.ShapeDtypeStruct((n, D_MODEL), F32),
        compiler_params=_params(("parallel",)),
        name="out_ffn2",
    )(h, a, r, wo, g, wg, wu, wd, gf)


def _proj_kernel(h_ref, pos_ref, inv_ref, pc_ref, ps_ref, gm_ref, win_ref, gq_ref, wq_ref, gkv_ref,
                 wkv_ref, qm_ref, km_ref, vm_ref, rq_ref, rk_ref, rv_ref, rg_ref):
    n_bf = _rms(h_ref[...], gm_ref[...]).astype(BF16)

    def proj(off, width):
        return jnp.dot(n_bf, win_ref[:, off:off + width], preferred_element_type=F32)

    def ret_rope(x, out_ref):
        for u in range(D_RET // QUAD):
            x1 = x[:, u * QUAD:u * QUAD + LANES]
            x2 = x[:, u * QUAD + LANES:(u + 1) * QUAD]
            out_ref[:, u * QUAD:u * QUAD + LANES] = (x1 * cr - x2 * sr).astype(BF16)
            out_ref[:, u * QUAD + LANES:(u + 1) * QUAD] = (x1 * sr + x2 * cr).astype(BF16)

    cq_bf = _rms(proj(_OFF_CQ, Q_LORA), gq_ref[...]).astype(BF16)
    lat = proj(_OFF_CKV, 4 * LANES)
    ckv_bf = _rms(lat[:, :KV_LORA], gkv_ref[...]).astype(BF16)
    xq = proj(_OFF_RQ, D_RET)

    cos_t, sin_t = _rope_tables(pos_ref[...], inv_ref[...], pc_ref[...], ps_ref[...])
    cr, sr = cos_t[:, :LANES], sin_t[:, :LANES]
    nope_lane = lax.broadcasted_iota(jnp.int32, (1, LANES), 1) < NOPE
    cm, sm = cos_t[:, LANES:] + jnp.where(nope_lane, 1.0, 0.0), sin_t[:, LANES:]

    xk = proj(_OFF_RK, D_RET)
    ret_rope(xq, rq_ref)
    rv_ref[...] = proj(_OFF_RV, D_RET).astype(BF16)
    ret_rope(xk, rk_ref)
    g = proj(_OFF_RG, D_RET)
    rg_ref[...] = (g * jax.nn.sigmoid(g)).astype(BF16)

    qq = jnp.dot(cq_bf, wq_ref[...], preferred_element_type=F32)
    kv = jnp.dot(ckv_bf, wkv_ref[...], preferred_element_type=F32)
    nh = MLA_HEADS * LANES
    kpe = lat[:, LANES:2 * LANES] * cm + lat[:, 2 * LANES:3 * LANES] * sm
    cq_s, sq_s = cm * (MLA_SCALE * LOG2E), sm * (MLA_SCALE * LOG2E)
    for h in range(MLA_HEADS):
        sl = slice(h * LANES, (h + 1) * LANES)
        sw = slice(nh + h * LANES, nh + (h + 1) * LANES)
        qm_ref[sl, :] = (qq[:, sl] * cq_s + qq[:, sw] * sq_s).astype(BF16).T
        km_ref[:, sl] = (kv[:, sl] + kpe).astype(BF16)
    lane = lax.broadcasted_iota(jnp.int32, (1, nh), 1)
    ones_lane = jnp.where(lane % LANES == VDIM, 1.0, 0.0)
    vm_ref[...] = (kv[:, nh:] + ones_lane).astype(BF16).T


def _proj(h, pos_row, gm, win, gq, wq, gkv, wkv, tm=512):
    n = h.shape[0]
    row = lambda w: pl.BlockSpec((tm, w), lambda i: (i, 0))
    sds = lambda w: jax.ShapeDtypeStruct((n, w), BF16)
    col = pl.BlockSpec((MLA_HEADS * LANES, tm), lambda i: (0, i))
    sds_t = jax.ShapeDtypeStruct((MLA_HEADS * LANES, n), BF16)
    inv_col, pc, ps = _rope_constants()
    return pl.pallas_call(
        _proj_kernel,
        grid=(n // tm,),
        in_specs=[row(D_MODEL), pl.BlockSpec((1, tm), lambda i: (0, i)),
                  _const_spec((N_FREQ, 1)), _const_spec((N_FREQ, 2 * LANES)),
                  _const_spec((N_FREQ, 2 * LANES)), _const_spec((1, D_MODEL)),
                  _const_spec((D_MODEL, D_IN_AUG)), _const_spec((1, Q_LORA)),
                  _const_spec((Q_LORA, 2 * MLA_HEADS * LANES)),
                  _const_spec((1, KV_LORA)), _const_spec((KV_LORA, 2 * MLA_HEADS * LANES))],
        out_specs=[col, row(1024), col, row(512), row(512), row(512), row(512)],
        out_shape=[sds_t, sds(1024), sds_t, sds(512), sds(512), sds(512), sds(512)],
        compiler_params=_params(("parallel",)),
        name="mixer_proj",
    )(h, pos_row, inv_col, pc, ps, gm, win, gq, wq, gkv, wkv)


V_ROWS = 80


def _mla_kernel(qt_ref, k_ref, vt_ref, o_ref, m_ref, acc_ref, s0_ref, s1_ref, mb0_ref, mb1_ref,
                al0_ref, al1_ref, *, tq, tk, tqs, seq, unroll):
    nk = seq // tk
    slots = ((s0_ref, mb0_ref, al0_ref), (s1_ref, mb1_ref, al1_ref))
    m_ref[...] = jnp.full(m_ref.shape, -jnp.inf, F32)
    acc_ref[...] = jnp.zeros(acc_ref.shape, F32)

    def scores(c, slot, qs):
        s_ref, mb_ref, al_ref = slots[slot]
        off = c * tk if isinstance(c, int) else pl.multiple_of(c * tk, tk)
        for h in range(2):
            k = k_ref[0, pl.ds(off, tk), h * LANES:(h + 1) * LANES]
            s = jnp.dot(k, qt_ref[h * LANES:(h + 1) * LANES, qs], preferred_element_type=F32)
            s_ref[h, :, qs] = s
            m_old = m_ref[h, :, qs]
            m_new = jnp.maximum(m_old, jnp.max(s, axis=0, keepdims=True))
            m_ref[h, :, qs] = m_new
            mb_ref[h, :, qs] = m_new
            al_ref[h, :, qs] = jnp.exp2(m_old - m_new)

    def weighted_values(c, slot, qs):
        s_ref, mb_ref, al_ref = slots[slot]
        off = c * tk if isinstance(c, int) else pl.multiple_of(c * tk, tk)
        for h in range(2):
            p = jnp.exp2(s_ref[h, :, qs] - mb_ref[h, 0:1, qs])
            vt = vt_ref[h * LANES:h * LANES + V_ROWS, pl.ds(off, tk)]
            acc_ref[h, :, qs] = (al_ref[h, 0:1, qs] * acc_ref[h, :, qs]
                                 + jnp.dot(vt, p.astype(BF16), preferred_element_type=F32))

    q_slices = [slice(i * tqs, (i + 1) * tqs) for i in range(tq // tqs)]

    def chunk_group(c0, last):
        for j in range(unroll):
            for qs in q_slices:
                if not (last and j == unroll - 1):
                    scores(c0 + j + 1, (j + 1) % 2, qs)
                weighted_values(c0 + j, j % 2, qs)

    for qs in q_slices:
        scores(0, 0, qs)

    def body(i, carry):
        chunk_group(i * unroll, False)
        return carry

    lax.fori_loop(0, nk // unroll - 1, body, 0)
    chunk_group(nk - unroll, True)

    outs = []
    for h in range(2):
        acc = acc_ref[h]
        outs.append(acc[:VDIM, :] / acc[VDIM:VDIM + 1, :])
    o_ref[0] = jnp.concatenate(outs, axis=0).T.astype(BF16)


def _mla_attention(qt, k, vt, tq=2048, tk=512, tqs=256, unroll=2):
    b, s, _ = k.shape
    npair = MLA_HEADS // 2
    assert unroll % 2 == 0 and (s // tk) % unroll == 0
    nq = s // tq
    rowstate = pltpu.VMEM((2, 8, tq), F32)
    sbuf = pltpu.VMEM((2, tk, tq), F32)
    return pl.pallas_call(
        functools.partial(_mla_kernel, tq=tq, tk=tk, tqs=tqs, seq=s, unroll=unroll),
        grid=(b, npair, nq),
        in_specs=[pl.BlockSpec((2 * LANES, tq), lambda bi, p, i: (p, bi * nq + i)),
                  pl.BlockSpec((1, s, 2 * LANES), lambda bi, p, i: (bi, 0, p)),
                  pl.BlockSpec((2 * LANES, s), lambda bi, p, i: (p, bi))],
        out_specs=pl.BlockSpec((1, tq, LANES), lambda bi, p, i: (bi, i, p)),
        out_shape=jax.ShapeDtypeStruct((b, s, MLA_HEADS * VDIM), BF16),
        scratch_shapes=[rowstate, pltpu.VMEM((2, V_ROWS, tq), F32), sbuf, sbuf,
                        rowstate, rowstate, rowstate, rowstate],
        compiler_params=_params(("parallel", "parallel", "parallel")),
        name="mla_attention",
    )(qt, k, vt)


def _log_sigmoid(x):
    return jnp.minimum(x, 0.0) - jnp.log1p(jnp.exp(-jnp.abs(x)))


def _retention_kernel(q_ref, k_ref, v_ref, g_ref, dec_ref, o_ref, acc_lo_ref, acc_hi_ref,
                      stf_ref, stb_ref, dmf_ref, dmb_ref, tab_ref, ones_ref, *, ch, seq, unroll):
    nc = seq // ch
    half = nc // 2
    nh = QUAD // RET_DIM
    lane = lax.broadcasted_iota(jnp.int32, (ch, QUAD), 1)
    row = lax.broadcasted_iota(jnp.int32, (ch, QUAD), 0).astype(F32)
    khead = (lane >> 5) & (nh - 1)
    vhead = lane >> 6
    ri = lax.broadcasted_iota(jnp.int32, (ch, ch), 0)
    ci = lax.broadcasted_iota(jnp.int32, (ch, ch), 1)
    sd = lax.broadcasted_iota(jnp.int32, (QUAD, QUAD), 0)
    se = lax.broadcasted_iota(jnp.int32, (QUAD, QUAD), 1)
    state_mask = ((sd >> 5) & (nh - 1)) == (se >> 6)
    ones_ref[...] = jnp.where((sd >> 6) == (se >> 6), 1.0, 0.0).astype(BF16)

    lg = _log_sigmoid(dec_ref[...])
    tab_ref[0] = jnp.exp(lg[0:1] * (row + 1.0))
    tab_ref[1] = jnp.exp(lg[1:2] * (ch - 1.0 - row))
    tab_ref[2] = jnp.exp(lg[2:3] * (ch - row))
    tab_ref[3] = jnp.exp(lg[3:4] * row)
    c_dec = (jnp.exp(lg[0:1] * float(ch)), jnp.exp(lg[2:3] * float(ch)))
    rel_f = jnp.maximum(ri - ci, 0).astype(F32)
    rel_b = jnp.maximum(ci - ri, 0).astype(F32)
    for h in range(nh):
        dmf_ref[h] = jnp.where(ri >= ci, jnp.exp(lg[0:1, h * RET_DIM:h * RET_DIM + 1] * rel_f), 0.0)
        dmb_ref[h] = jnp.where(ci > ri, jnp.exp(lg[2:3, h * RET_DIM:h * RET_DIM + 1] * rel_b), 0.0)
    stf_ref[...] = jnp.zeros_like(stf_ref)
    stb_ref[...] = jnp.zeros_like(stb_ref)

    def chunks(items):
        loaded = []
        for c, _ in items:
            off = pl.multiple_of(c * ch, ch)
            loaded.append((q_ref[0, pl.ds(off, ch), :], k_ref[0, pl.ds(off, ch), :],
                           v_ref[0, pl.ds(off, ch), :]))
        scores = []
        for qc, kc, _ in loaded:
            zero = jnp.zeros_like(qc)
            scores.append([lax.dot_general(jnp.where(khead == h, qc, zero), kc,
                                           (((1,), (1,)), ((), ())), preferred_element_type=F32)
                           for h in range(nh)])
        outs = []
        for (_, fwd), (_, _, vc), ss in zip(items, loaded, scores):
            dm_ref = dmf_ref if fwd else dmb_ref
            out = None
            for h in range(nh):
                ih = jnp.dot((ss[h] * dm_ref[h]).astype(BF16), vc, preferred_element_type=F32)
                out = ih if out is None else jnp.where(vhead == h, ih, out)
            outs.append(out)
        upds = []
        for (_, fwd), (_, kc, vc) in zip(items, loaded):
            kd = (kc.astype(F32) * tab_ref[1 if fwd else 3]).astype(BF16)
            upds.append(lax.dot_general(kd, vc, (((0,), (0,)), ((), ())),
                                        preferred_element_type=F32))
        for i, ((_, fwd), (qc, _, _)) in enumerate(zip(items, loaded)):
            st_ref = stf_ref if fwd else stb_ref
            state = st_ref[...]
            cross = jnp.dot(qc, state.astype(BF16), preferred_element_type=F32)
            outs[i] = outs[i] + cross * tab_ref[0 if fwd else 2]
            st_ref[...] = state * c_dec[0 if fwd else 1] + jnp.where(state_mask, upds[i], 0.0)
        return outs

    def head_sum(x):
        hi = x.astype(BF16)
        lo = (x - hi.astype(F32)).astype(BF16)
        return (jnp.dot(hi, ones_ref[...], preferred_element_type=F32)
                + jnp.dot(lo, ones_ref[...], preferred_element_type=F32))

    def finish(c, tot):
        off = pl.multiple_of(c * ch, ch)
        d = tot - head_sum(tot) * (1.0 / RET_DIM)
        var = head_sum(d * d) * (1.0 / RET_DIM)
        gate = g_ref[0, pl.ds(off, ch), :].astype(F32)
        o_ref[0, pl.ds(off, ch), :] = (d * lax.rsqrt(var + EPS) * gate).astype(BF16)

    def rows(ref, c):
        return ref.at[pl.ds(pl.multiple_of(c * ch, ch), ch), :]

    def pairs(t):
        items = []
        for j in range(unroll):
            cf = t * unroll + j
            items += [(cf, True), (nc - 1 - cf, False)]
        return items

    def first(t, carry):
        items = pairs(t)
        for (c, fwd), out in zip(items, chunks(items)):
            if fwd:
                rows(acc_lo_ref, c)[...] = out
            else:
                rows(acc_hi_ref, c - half)[...] = out
        return carry

    lax.fori_loop(0, half // unroll, first, 0)

    def second(t, carry):
        items = pairs(t)
        for (c, fwd), out in zip(items, chunks(items)):
            other = rows(acc_hi_ref, c - half) if fwd else rows(acc_lo_ref, c)
            finish(c, out + other[...])
        return carry

    lax.fori_loop(half // unroll, nc // unroll, second, 0)


def _retention(q, k, v, g, dec, ch=256, unroll=2):
    b, s, _ = q.shape
    nquad = D_RET // QUAD
    assert (s // ch // 2) % unroll == 0
    blk = pl.BlockSpec((1, s, QUAD), lambda bi, u: (bi, 0, u))
    return pl.pallas_call(
        functools.partial(_retention_kernel, ch=ch, seq=s, unroll=unroll),
        grid=(b, nquad),
        in_specs=[blk, blk, blk, blk, pl.BlockSpec((4, QUAD), lambda bi, u: (0, u))],
        out_specs=blk,
        out_shape=jax.ShapeDtypeStruct((b, s, D_RET), BF16),
        scratch_shapes=[pltpu.VMEM((s // 2, QUAD), F32), pltpu.VMEM((s // 2, QUAD), F32),
                        pltpu.VMEM((QUAD, QUAD), F32), pltpu.VMEM((QUAD, QUAD), F32),
                        pltpu.VMEM((QUAD // RET_DIM, ch, ch), F32),
                        pltpu.VMEM((QUAD // RET_DIM, ch, ch), F32),
                        pltpu.VMEM((4, ch, QUAD), F32), pltpu.VMEM((QUAD, QUAD), BF16)],
        compiler_params=_params(("parallel", "parallel")),
        name="retention",
    )(q, k, v, g, dec)


def _ret_qk_perm():
    new = np.arange(D_RET)
    quad, j = new // QUAD, new % QUAD
    head = 4 * quad + (j % LANES) // 32
    dim = (j // LANES) * 32 + j % 32
    return head * RET_DIM + dim


def _prep_w_in(w_in):
    c_q, c_kv, k_pe, r_q, r_k, r_v, r_g = jnp.split(
        w_in, np.cumsum([Q_LORA, KV_LORA, ROPE, D_RET, D_RET, D_RET])[:], axis=1)
    k = w_in.shape[0]
    z = lambda w: jnp.zeros((k, w), w_in.dtype)
    kpe = jnp.concatenate([z(64), k_pe, z(32)], axis=1)
    kpe_sw = jnp.concatenate([z(64), k_pe[:, 16:], k_pe[:, :16], z(32)], axis=1)
    perm = _ret_qk_perm()
    return jnp.concatenate([c_q, c_kv, kpe, kpe_sw, z(LANES), r_q[:, perm],
                            r_k[:, perm] * (RET_DIM ** -0.5), r_v, r_g], axis=1).astype(BF16)


def _prep_w_uq(w_uq):
    k = w_uq.shape[0]
    w3 = w_uq.reshape(k, MLA_HEADS, NOPE + ROPE)
    nope, x1, x2 = w3[:, :, :NOPE], w3[:, :, NOPE:NOPE + 16], w3[:, :, NOPE + 16:]
    z32 = jnp.zeros((k, MLA_HEADS, 32), w_uq.dtype)
    wq = jnp.concatenate([nope, x1, x2, z32], axis=2).reshape(k, MLA_HEADS * LANES)
    wqs = jnp.concatenate([jnp.zeros_like(nope), x2, x1, z32], axis=2).reshape(k, MLA_HEADS * LANES)
    return jnp.concatenate([wq, wqs], axis=1).astype(BF16)


def _prep_w_ukv(w_ukv):
    k = w_ukv.shape[0]
    w3 = w_ukv.reshape(k, MLA_HEADS, NOPE + VDIM)
    kn = jnp.concatenate([w3[:, :, :NOPE], jnp.zeros((k, MLA_HEADS, 64), w_ukv.dtype)], axis=2)
    v = jnp.concatenate([w3[:, :, NOPE:], jnp.zeros((k, MLA_HEADS, 64), w_ukv.dtype)], axis=2)
    return jnp.concatenate([kn.reshape(k, MLA_HEADS * LANES), v.reshape(k, MLA_HEADS * LANES)],
                           axis=1).astype(BF16)


def _prep_decay(dec_f, dec_b):
    nquad, per_quad = D_RET // QUAD, QUAD // RET_DIM
    rows = []
    for d in (dec_f.astype(F32), dec_b.astype(F32)):
        rows.append(jnp.broadcast_to(d[:, None], (RET_HEADS, RET_DIM)).reshape(D_RET))
        rows.append(jnp.broadcast_to(d.reshape(nquad, 1, per_quad, 1),
                                     (nquad, 2, per_quad, 32)).reshape(D_RET))
    return jnp.stack(rows)


def kernel(x, positions, ffn1_norm, ffn1_w_gate, ffn1_w_up, ffn1_w_down, mix_norm, w_in, q_norm, w_uq, kv_norm, w_ukv, ret_decay_fwd, ret_decay_bwd, w_o, ffn2_norm, ffn2_w_gate, ffn2_w_up, ffn2_w_down, final_norm):
    b, s, d = x.shape
    n = b * s
    assert ffn1_norm.shape[0] == 1, "specialised to DEPTH == 1 (the final norm is fused into the layer)"
    pos_row = positions.astype(F32).reshape(1, n)

    h = _ffn1(x.reshape(n, d), ffn1_norm[0][None, :], ffn1_w_gate[0], ffn1_w_up[0],
              ffn1_w_down[0])
    qm, km, vm, rq, rk, rv, rg = _proj(
        h, pos_row, mix_norm[0][None, :], _prep_w_in(w_in[0]), q_norm[0][None, :],
        _prep_w_uq(w_uq[0]), kv_norm[0][None, :], _prep_w_ukv(w_ukv[0]))
    a = _mla_attention(qm, km.reshape(b, s, -1), vm)
    r = _retention(rq.reshape(b, s, -1), rk.reshape(b, s, -1), rv.reshape(b, s, -1),
                   rg.reshape(b, s, -1), _prep_decay(ret_decay_fwd[0], ret_decay_bwd[0]))
    wo = w_o[0].astype(BF16)
    out = _out_ffn2(h, a.reshape(n, -1), r.reshape(n, -1), wo, ffn2_norm[0][None, :],
                    ffn2_w_gate[0], ffn2_w_up[0], ffn2_w_down[0], final_norm[None, :])
    return out.reshape(b, s, d)
```

```python
import functools
import math

import numpy as np
import jax
import jax.numpy as jnp
from jax import lax
from jax.experimental import pallas as pl
from jax.experimental.pallas import tpu as pltpu

F32 = jnp.float32
BF16 = jnp.bfloat16

D_MODEL = 1024
D_FF = 2816
MLA_HEADS = 8
Q_LORA = 256
KV_LORA = 128
NOPE = 64
ROPE = 32
VDIM = 64
RET_HEADS = 8
RET_DIM = 64
D_RET = RET_HEADS * RET_DIM
ROPE_THETA = 10000.0
EPS = 1e-6
MLA_SCALE = (NOPE + ROPE) ** -0.5
LOG2E = math.log2(math.e)

LANES = 128
QUAD = 256
VMEM_LIMIT = 56 * 1024 * 1024

_OFF_CQ = 0
_OFF_CKV = 256
_OFF_RQ = 768
_OFF_RK = 1280
_OFF_RV = 1792
_OFF_RG = 2304
D_IN_AUG = 2816


def _params(sem):
    return pltpu.CompilerParams(dimension_semantics=sem, vmem_limit_bytes=VMEM_LIMIT)


def _const_spec(shape):
    nd = len(shape)
    return pl.BlockSpec(shape, lambda *_: (0,) * nd, pipeline_mode=pl.Buffered(1))


N_FREQ = 64


def _rope_constants():
    row = np.arange(N_FREQ)
    inv = np.where(row < 32, ROPE_THETA ** (-(row % 32) / 32.0),
                   np.where(row < 48, ROPE_THETA ** (-((row - 32) % 16) / 16.0), 0.0))
    pc = np.zeros((N_FREQ, 2 * LANES))
    ps = np.zeros((N_FREQ, 2 * LANES))
    for l in range(LANES):
        pc[l % 32, l] = 1.0
        ps[l % 32, l] = 1.0
    for l in range(NOPE, NOPE + ROPE):
        r = 32 + (l - NOPE) % 16
        pc[r, LANES + l] = 1.0
        ps[r, LANES + l] = -1.0 if l < NOPE + 16 else 1.0
    return jnp.asarray(inv[:, None], F32), jnp.asarray(pc, BF16), jnp.asarray(ps, BF16)


def _rope_tables(pos_row, inv_col, pc, ps):
    ang = inv_col * pos_row
    dn = (((0,), (0,)), ((), ()))

    def place(x, p):
        hi = x.astype(BF16)
        lo = (x - hi.astype(F32)).astype(BF16)
        return (lax.dot_general(hi, p, dn, preferred_element_type=F32)
                + lax.dot_general(lo, p, dn, preferred_element_type=F32))

    return place(jnp.cos(ang), pc), place(jnp.sin(ang), ps)


def _rms(x, g):
    return x * lax.rsqrt(jnp.mean(x * x, axis=-1, keepdims=True) + EPS) * g


def _swiglu_acc(n_bf, wg_ref, wu_ref, wd_ref, ck):
    acc = None
    for j in range(D_FF // ck):
        sl = slice(j * ck, (j + 1) * ck)
        g = jnp.dot(n_bf, wg_ref[:, sl], preferred_element_type=F32)
        u = jnp.dot(n_bf, wu_ref[:, sl], preferred_element_type=F32)
        hm = (g * jax.nn.sigmoid(g) * u).astype(BF16)
        d = jnp.dot(hm, wd_ref[sl, :], preferred_element_type=F32)
        acc = d if acc is None else acc + d
    return acc


def _stage_weights(i, wg_ref, wu_ref, wd_ref, wg_bf, wu_bf, wd_bf, ck):
    off = pl.multiple_of(i * ck, ck)
    wg_bf[:, pl.ds(off, ck)] = wg_ref[...].astype(BF16)
    wu_bf[:, pl.ds(off, ck)] = wu_ref[...].astype(BF16)
    wd_bf[pl.ds(off, ck), :] = wd_ref[...].astype(BF16)


def _ffn_specs(tm, ck):
    nstage = D_FF // ck
    tile = lambda w: pl.BlockSpec((tm, w), lambda i: (jnp.maximum(i - nstage, 0), 0))
    col_chunk = pl.BlockSpec((D_MODEL, ck), lambda i: (0, jnp.minimum(i, nstage - 1)))
    row_chunk = pl.BlockSpec((ck, D_MODEL), lambda i: (jnp.minimum(i, nstage - 1), 0))
    scratch = [pltpu.VMEM((D_MODEL, D_FF), BF16), pltpu.VMEM((D_MODEL, D_FF), BF16),
               pltpu.VMEM((D_FF, D_MODEL), BF16)]
    return nstage, tile, [col_chunk, col_chunk, row_chunk], scratch


def _ffn1_kernel(x_ref, g_ref, wg_ref, wu_ref, wd_ref, o_ref, wg_bf, wu_bf, wd_bf, *, ck):
    i = pl.program_id(0)
    nstage = D_FF // ck

    @pl.when(i < nstage)
    def _():
        _stage_weights(i, wg_ref, wu_ref, wd_ref, wg_bf, wu_bf, wd_bf, ck)

    @pl.when(i >= nstage)
    def _():
        x = x_ref[...]
        n_bf = _rms(x, g_ref[...]).astype(BF16)
        o_ref[...] = x + 0.5 * _swiglu_acc(n_bf, wg_bf, wu_bf, wd_bf, ck)


def _ffn1(x, g, wg, wu, wd, tm=1024, ck=256):
    n = x.shape[0]
    nstage, tile, wspecs, scratch = _ffn_specs(tm, ck)
    return pl.pallas_call(
        functools.partial(_ffn1_kernel, ck=ck),
        grid=(nstage + n // tm,),
        in_specs=[tile(D_MODEL), _const_spec((1, D_MODEL))] + wspecs,
        out_specs=tile(D_MODEL),
        out_shape=jax.ShapeDtypeStruct((n, D_MODEL), F32),
        scratch_shapes=scratch,
        compiler_params=_params(("arbitrary",)),
        name="ffn1",
    )(x, g, wg, wu, wd)


def _out_ffn2_kernel(h_ref, a_ref, r_ref, wo_ref, g_ref, wg_ref, wu_ref, wd_ref, gf_ref, o_ref,
                     wg_bf, wu_bf, wd_bf, *, ck):
    i = pl.program_id(0)
    nstage = D_FF // ck

    @pl.when(i < nstage)
    def _():
        _stage_weights(i, wg_ref, wu_ref, wd_ref, wg_bf, wu_bf, wd_bf, ck)

    @pl.when(i >= nstage)
    def _():
        h = (h_ref[...]
             + jnp.dot(a_ref[...], wo_ref[:MLA_HEADS * VDIM, :], preferred_element_type=F32)
             + jnp.dot(r_ref[...], wo_ref[MLA_HEADS * VDIM:, :], preferred_element_type=F32))
        n_bf = _rms(h, g_ref[...]).astype(BF16)
        h = h + 0.5 * _swiglu_acc(n_bf, wg_bf, wu_bf, wd_bf, ck)
        o_ref[...] = _rms(h, gf_ref[...])


def _out_ffn2(h, a, r, wo, g, wg, wu, wd, gf, tm=1024, ck=256):
    n = h.shape[0]
    nstage, tile, wspecs, scratch = _ffn_specs(tm, ck)
    return pl.pallas_call(
        functools.partial(_out_ffn2_kernel, ck=ck),
        grid=(nstage + n // tm,),
        in_specs=[tile(D_MODEL), tile(D_RET), tile(D_RET),
                  _const_spec((MLA_HEADS * VDIM + D_RET, D_MODEL)), _const_spec((1, D_MODEL))]
                 + wspecs + [_const_spec((1, D_MODEL))],
        out_specs=tile(D_MODEL),
        out_shape=jax.ShapeDtypeStruct((n, D_MODEL), F32),
        scratch_shapes=scratch,
        compiler_params=_params(("arbitrary",)),
        name="out_ffn2",
    )(h, a, r, wo, g, wg, wu, wd, gf)


def _proj_kernel(h_ref, pos_ref, inv_ref, pc_ref, ps_ref, gm_ref, win_ref, gq_ref, wq_ref, gkv_ref,
                 wkv_ref, qm_ref, km_ref, vm_ref, rq_ref, rk_ref, rv_ref, rg_ref):
    n_bf = _rms(h_ref[...], gm_ref[...]).astype(BF16)

    def proj(off, width):
        return jnp.dot(n_bf, win_ref[:, off:off + width], preferred_element_type=F32)

    def ret_rope(x, out_ref):
        for u in range(D_RET // QUAD):
            x1 = x[:, u * QUAD:u * QUAD + LANES]
            x2 = x[:, u * QUAD + LANES:(u + 1) * QUAD]
            out_ref[:, u * QUAD:u * QUAD + LANES] = (x1 * cr - x2 * sr).astype(BF16)
            out_ref[:, u * QUAD + LANES:(u + 1) * QUAD] = (x1 * sr + x2 * cr).astype(BF16)

    cq_bf = _rms(proj(_OFF_CQ, Q_LORA), gq_ref[...]).astype(BF16)
    lat = proj(_OFF_CKV, 4 * LANES)
    ckv_bf = _rms(lat[:, :KV_LORA], gkv_ref[...]).astype(BF16)
    xq = proj(_OFF_RQ, D_RET)

    cos_t, sin_t = _rope_tables(pos_ref[...], inv_ref[...], pc_ref[...], ps_ref[...])
    cr, sr = cos_t[:, :LANES], sin_t[:, :LANES]
    nope_lane = lax.broadcasted_iota(jnp.int32, (1, LANES), 1) < NOPE
    cm, sm = cos_t[:, LANES:] + jnp.where(nope_lane, 1.0, 0.0), sin_t[:, LANES:]

    xk = proj(_OFF_RK, D_RET)
    ret_rope(xq, rq_ref)
    rv_ref[...] = proj(_OFF_RV, D_RET).astype(BF16)
    ret_rope(xk, rk_ref)
    g = proj(_OFF_RG, D_RET)
    rg_ref[...] = (g * jax.nn.sigmoid(g)).astype(BF16)

    qq = jnp.dot(cq_bf, wq_ref[...], preferred_element_type=F32)
    kv = jnp.dot(ckv_bf, wkv_ref[...], preferred_element_type=F32)
    nh = MLA_HEADS * LANES
    kpe = lat[:, LANES:2 * LANES] * cm + lat[:, 2 * LANES:3 * LANES] * sm
    cq_s, sq_s = cm * (MLA_SCALE * LOG2E), sm * (MLA_SCALE * LOG2E)
    for h in range(MLA_HEADS):
        sl = slice(h * LANES, (h + 1) * LANES)
        sw = slice(nh + h * LANES, nh + (h + 1) * LANES)
        qm_ref[sl, :] = (qq[:, sl] * cq_s + qq[:, sw] * sq_s).astype(BF16).T
        km_ref[:, sl] = (kv[:, sl] + kpe).astype(BF16)
    lane = lax.broadcasted_iota(jnp.int32, (1, nh), 1)
    ones_lane = jnp.where(lane % LANES == VDIM, 1.0, 0.0)
    vm_ref[...] = (kv[:, nh:] + ones_lane).astype(BF16).T


def _proj(h, pos_row, gm, win, gq, wq, gkv, wkv, tm=512):
    n = h.shape[0]
    row = lambda w: pl.BlockSpec((tm, w), lambda i: (i, 0))
    sds = lambda w: jax.ShapeDtypeStruct((n, w), BF16)
    col = pl.BlockSpec((MLA_HEADS * LANES, tm), lambda i: (0, i))
    sds_t = jax.ShapeDtypeStruct((MLA_HEADS * LANES, n), BF16)
    inv_col, pc, ps = _rope_constants()
    return pl.pallas_call(
        _proj_kernel,
        grid=(n // tm,),
        in_specs=[row(D_MODEL), pl.BlockSpec((1, tm), lambda i: (0, i)),
                  _const_spec((N_FREQ, 1)), _const_spec((N_FREQ, 2 * LANES)),
                  _const_spec((N_FREQ, 2 * LANES)), _const_spec((1, D_MODEL)),
                  _const_spec((D_MODEL, D_IN_AUG)), _const_spec((1, Q_LORA)),
                  _const_spec((Q_LORA, 2 * MLA_HEADS * LANES)),
                  _const_spec((1, KV_LORA)), _const_spec((KV_LORA, 2 * MLA_HEADS * LANES))],
        out_specs=[col, row(1024), col, row(512), row(512), row(512), row(512)],
        out_shape=[sds_t, sds(1024), sds_t, sds(512), sds(512), sds(512), sds(512)],
        compiler_params=_params(("parallel",)),
        name="mixer_proj",
    )(h, pos_row, inv_col, pc, ps, gm, win, gq, wq, gkv, wkv)


V_ROWS = 80


def _mla_kernel(qt_ref, k_ref, vt_ref, o_ref, m_ref, acc_ref, s0_ref, s1_ref, mb0_ref, mb1_ref,
                al0_ref, al1_ref, *, tq, tk, tqs, seq, unroll):
    nk = seq // tk
    slots = ((s0_ref, mb0_ref, al0_ref), (s1_ref, mb1_ref, al1_ref))
    m_ref[...] = jnp.full(m_ref.shape, -jnp.inf, F32)
    acc_ref[...] = jnp.zeros(acc_ref.shape, F32)

    def scores(c, slot, qs):
        s_ref, mb_ref, al_ref = slots[slot]
        off = c * tk if isinstance(c, int) else pl.multiple_of(c * tk, tk)
        for h in range(2):
            k = k_ref[0, pl.ds(off, tk), h * LANES:(h + 1) * LANES]
            s = jnp.dot(k, qt_ref[h * LANES:(h + 1) * LANES, qs], preferred_element_type=F32)
            s_ref[h, :, qs] = s
            m_old = m_ref[h, :, qs]
            m_new = jnp.maximum(m_old, jnp.max(s, axis=0, keepdims=True))
            m_ref[h, :, qs] = m_new
            mb_ref[h, :, qs] = m_new
            al_ref[h, :, qs] = jnp.exp2(m_old - m_new)

    def weighted_values(c, slot, qs):
        s_ref, mb_ref, al_ref = slots[slot]
        off = c * tk if isinstance(c, int) else pl.multiple_of(c * tk, tk)
        for h in range(2):
            p = jnp.exp2(s_ref[h, :, qs] - mb_ref[h, 0:1, qs])
            vt = vt_ref[h * LANES:h * LANES + V_ROWS, pl.ds(off, tk)]
            acc_ref[h, :, qs] = (al_ref[h, 0:1, qs] * acc_ref[h, :, qs]
                                 + jnp.dot(vt, p.astype(BF16), preferred_element_type=F32))

    q_slices = [slice(i * tqs, (i + 1) * tqs) for i in range(tq // tqs)]

    def chunk_group(c0, last):
        for j in range(unroll):
            for qs in q_slices:
                if not (last and j == unroll - 1):
                    scores(c0 + j + 1, (j + 1) % 2, qs)
                weighted_values(c0 + j, j % 2, qs)

    for qs in q_slices:
        scores(0, 0, qs)

    def body(i, carry):
        chunk_group(i * unroll, False)
        return carry

    lax.fori_loop(0, nk // unroll - 1, body, 0)
    chunk_group(nk - unroll, True)

    outs = []
    for h in range(2):
        acc = acc_ref[h]
        outs.append(acc[:VDIM, :] / acc[VDIM:VDIM + 1, :])
    o_ref[0] = jnp.concatenate(outs, axis=0).T.astype(BF16)


def _mla_attention(qt, k, vt, tq=2048, tk=512, tqs=256, unroll=2):
    b, s, _ = k.shape
    npair = MLA_HEADS // 2
    assert unroll % 2 == 0 and (s // tk) % unroll == 0
    nq = s // tq
    rowstate = pltpu.VMEM((2, 8, tq), F32)
    sbuf = pltpu.VMEM((2, tk, tq), F32)
    return pl.pallas_call(
        functools.partial(_mla_kernel, tq=tq, tk=tk, tqs=tqs, seq=s, unroll=unroll),
        grid=(b, npair, nq),
        in_specs=[pl.BlockSpec((2 * LANES, tq), lambda bi, p, i: (p, bi * nq + i)),
                  pl.BlockSpec((1, s, 2 * LANES), lambda bi, p, i: (bi, 0, p)),
                  pl.BlockSpec((2 * LANES, s), lambda bi, p, i: (p, bi))],
        out_specs=pl.BlockSpec((1, tq, LANES), lambda bi, p, i: (bi, i, p)),
        out_shape=jax.ShapeDtypeStruct((b, s, MLA_HEADS * VDIM), BF16),
        scratch_shapes=[rowstate, pltpu.VMEM((2, V_ROWS, tq), F32), sbuf, sbuf,
                        rowstate, rowstate, rowstate, rowstate],
        compiler_params=_params(("parallel", "parallel", "parallel")),
        name="mla_attention",
    )(qt, k, vt)


def _log_sigmoid(x):
    return jnp.minimum(x, 0.0) - jnp.log1p(jnp.exp(-jnp.abs(x)))


def _retention_kernel(q_ref, k_ref, v_ref, g_ref, dec_ref, o_ref, acc_lo_ref, acc_hi_ref,
                      stf_ref, stb_ref, dmf_ref, dmb_ref, tab_ref, ones_ref, *, ch, seq, unroll):
    nc = seq // ch
    half = nc // 2
    nh = QUAD // RET_DIM
    lane = lax.broadcasted_iota(jnp.int32, (ch, QUAD), 1)
    row = lax.broadcasted_iota(jnp.int32, (ch, QUAD), 0).astype(F32)
    khead = (lane >> 5) & (nh - 1)
    vhead = lane >> 6
    ri = lax.broadcasted_iota(jnp.int32, (ch, ch), 0)
    ci = lax.broadcasted_iota(jnp.int32, (ch, ch), 1)
    sd = lax.broadcasted_iota(jnp.int32, (QUAD, QUAD), 0)
    se = lax.broadcasted_iota(jnp.int32, (QUAD, QUAD), 1)
    state_mask = ((sd >> 5) & (nh - 1)) == (se >> 6)
    ones_ref[...] = jnp.where((sd >> 6) == (se >> 6), 1.0, 0.0).astype(BF16)

    lg = _log_sigmoid(dec_ref[...])
    tab_ref[0] = jnp.exp(lg[0:1] * (row + 1.0))
    tab_ref[1] = jnp.exp(lg[1:2] * (ch - 1.0 - row))
    tab_ref[2] = jnp.exp(lg[2:3] * (ch - row))
    tab_ref[3] = jnp.exp(lg[3:4] * row)
    c_dec = (jnp.exp(lg[0:1] * float(ch)), jnp.exp(lg[2:3] * float(ch)))
    rel_f = jnp.maximum(ri - ci, 0).astype(F32)
    rel_b = jnp.maximum(ci - ri, 0).astype(F32)
    for h in range(nh):
        dmf_ref[h] = jnp.where(ri >= ci, jnp.exp(lg[0:1, h * RET_DIM:h * RET_DIM + 1] * rel_f), 0.0)
        dmb_ref[h] = jnp.where(ci > ri, jnp.exp(lg[2:3, h * RET_DIM:h * RET_DIM + 1] * rel_b), 0.0)
    stf_ref[...] = jnp.zeros_like(stf_ref)
    stb_ref[...] = jnp.zeros_like(stb_ref)

    def chunks(items):
        loaded = []
        for c, _ in items:
            off = pl.multiple_of(c * ch, ch)
            loaded.append((q_ref[0, pl.ds(off, ch), :], k_ref[0, pl.ds(off, ch), :],
                           v_ref[0, pl.ds(off, ch), :]))
        scores = []
        for qc, kc, _ in loaded:
            zero = jnp.zeros_like(qc)
            scores.append([lax.dot_general(jnp.where(khead == h, qc, zero), kc,
                                           (((1,), (1,)), ((), ())), preferred_element_type=F32)
                           for h in range(nh)])
        outs = []
        for (_, fwd), (_, _, vc), ss in zip(items, loaded, scores):
            dm_ref = dmf_ref if fwd else dmb_ref
            out = None
            for h in range(nh):
                ih = jnp.dot((ss[h] * dm_ref[h]).astype(BF16), vc, preferred_element_type=F32)
                out = ih if out is None else jnp.where(vhead == h, ih, out)
            outs.append(out)
        upds = []
        for (_, fwd), (_, kc, vc) in zip(items, loaded):
            kd = (kc.astype(F32) * tab_ref[1 if fwd else 3]).astype(BF16)
            upds.append(lax.dot_general(kd, vc, (((0,), (0,)), ((), ())),
                                        preferred_element_type=F32))
        for i, ((_, fwd), (qc, _, _)) in enumerate(zip(items, loaded)):
            st_ref = stf_ref if fwd else stb_ref
            state = st_ref[...]
            cross = jnp.dot(qc, state.astype(BF16), preferred_element_type=F32)
            outs[i] = outs[i] + cross * tab_ref[0 if fwd else 2]
            st_ref[...] = state * c_dec[0 if fwd else 1] + jnp.where(state_mask, upds[i], 0.0)
        return outs

    def head_sum(x):
        hi = x.astype(BF16)
        lo = (x - hi.astype(F32)).astype(BF16)
        return (jnp.dot(hi, ones_ref[...], preferred_element_type=F32)
                + jnp.dot(lo, ones_ref[...], preferred_element_type=F32))

    def finish(c, tot):
        off = pl.multiple_of(c * ch, ch)
        d = tot - head_sum(tot) * (1.0 / RET_DIM)
        var = head_sum(d * d) * (1.0 / RET_DIM)
        gate = g_ref[0, pl.ds(off, ch), :].astype(F32)
        o_ref[0, pl.ds(off, ch), :] = (d * lax.rsqrt(var + EPS) * gate).astype(BF16)

    def rows(ref, c):
        return ref.at[pl.ds(pl.multiple_of(c * ch, ch), ch), :]

    def pairs(t):
        items = []
        for j in range(unroll):
            cf = t * unroll + j
            items += [(cf, True), (nc - 1 - cf, False)]
        return items

    def first(t, carry):
        items = pairs(t)
        for (c, fwd), out in zip(items, chunks(items)):
            if fwd:
                rows(acc_lo_ref, c)[...] = out
            else:
                rows(acc_hi_ref, c - half)[...] = out
        return carry

    lax.fori_loop(0, half // unroll, first, 0)

    def second(t, carry):
        items = pairs(t)
        for (c, fwd), out in zip(items, chunks(items)):
            other = rows(acc_hi_ref, c - half) if fwd else rows(acc_lo_ref, c)
            finish(c, out + other[...])
        return carry

    lax.fori_loop(half // unroll, nc // unroll, second, 0)


def _retention(q, k, v, g, dec, ch=256, unroll=2):
    b, s, _ = q.shape
    nquad = D_RET // QUAD
    assert (s // ch // 2) % unroll == 0
    blk = pl.BlockSpec((1, s, QUAD), lambda bi, u: (bi, 0, u))
    return pl.pallas_call(
        functools.partial(_retention_kernel, ch=ch, seq=s, unroll=unroll),
        grid=(b, nquad),
        in_specs=[blk, blk, blk, blk, pl.BlockSpec((4, QUAD), lambda bi, u: (0, u))],
        out_specs=blk,
        out_shape=jax.ShapeDtypeStruct((b, s, D_RET), BF16),
        scratch_shapes=[pltpu.VMEM((s // 2, QUAD), F32), pltpu.VMEM((s // 2, QUAD), F32),
                        pltpu.VMEM((QUAD, QUAD), F32), pltpu.VMEM((QUAD, QUAD), F32),
                        pltpu.VMEM((QUAD // RET_DIM, ch, ch), F32),
                        pltpu.VMEM((QUAD // RET_DIM, ch, ch), F32),
                        pltpu.VMEM((4, ch, QUAD), F32), pltpu.VMEM((QUAD, QUAD), BF16)],
        compiler_params=_params(("parallel", "parallel")),
        name="retention",
    )(q, k, v, g, dec)


def _ret_qk_perm():
    new = np.arange(D_RET)
    quad, j = new // QUAD, new % QUAD
    head = 4 * quad + (j % LANES) // 32
    dim = (j // LANES) * 32 + j % 32
    return head * RET_DIM + dim


def _prep_w_in(w_in):
    c_q, c_kv, k_pe, r_q, r_k, r_v, r_g = jnp.split(
        w_in, np.cumsum([Q_LORA, KV_LORA, ROPE, D_RET, D_RET, D_RET])[:], axis=1)
    k = w_in.shape[0]
    z = lambda w: jnp.zeros((k, w), w_in.dtype)
    kpe = jnp.concatenate([z(64), k_pe, z(32)], axis=1)
    kpe_sw = jnp.concatenate([z(64), k_pe[:, 16:], k_pe[:, :16], z(32)], axis=1)
    perm = _ret_qk_perm()
    return jnp.concatenate([c_q, c_kv, kpe, kpe_sw, z(LANES), r_q[:, perm],
                            r_k[:, perm] * (RET_DIM ** -0.5), r_v, r_g], axis=1).astype(BF16)


def _prep_w_uq(w_uq):
    k = w_uq.shape[0]
    w3 = w_uq.reshape(k, MLA_HEADS, NOPE + ROPE)
    nope, x1, x2 = w3[:, :, :NOPE], w3[:, :, NOPE:NOPE + 16], w3[:, :, NOPE + 16:]
    z32 = jnp.zeros((k, MLA_HEADS, 32), w_uq.dtype)
    wq = jnp.concatenate([nope, x1, x2, z32], axis=2).reshape(k, MLA_HEADS * LANES)
    wqs = jnp.concatenate([jnp.zeros_like(nope), x2, x1, z32], axis=2).reshape(k, MLA_HEADS * LANES)
    return jnp.concatenate([wq, wqs], axis=1).astype(BF16)


def _prep_w_ukv(w_ukv):
    k = w_ukv.shape[0]
    w3 = w_ukv.reshape(k, MLA_HEADS, NOPE + VDIM)
    kn = jnp.concatenate([w3[:, :, :NOPE], jnp.zeros((k, MLA_HEADS, 64), w_ukv.dtype)], axis=2)
    v = jnp.concatenate([w3[:, :, NOPE:], jnp.zeros((k, MLA_HEADS, 64), w_ukv.dtype)], axis=2)
    return jnp.concatenate([kn.reshape(k, MLA_HEADS * LANES), v.reshape(k, MLA_HEADS * LANES)],
                           axis=1).astype(BF16)


def _prep_decay(dec_f, dec_b):
    nquad, per_quad = D_RET // QUAD, QUAD // RET_DIM
    rows = []
    for d in (dec_f.astype(F32), dec_b.astype(F32)):
        rows.append(jnp.broadcast_to(d[:, None], (RET_HEADS, RET_DIM)).reshape(D_RET))
        rows.append(jnp.broadcast_to(d.reshape(nquad, 1, per_quad, 1),
                                     (nquad, 2, per_quad, 32)).reshape(D_RET))
    return jnp.stack(rows)


def kernel(x, positions, ffn1_norm, ffn1_w_gate, ffn1_w_up, ffn1_w_down, mix_norm, w_in, q_norm, w_uq, kv_norm, w_ukv, ret_decay_fwd, ret_decay_bwd, w_o, ffn2_norm, ffn2_w_gate, ffn2_w_up, ffn2_w_down, final_norm):
    b, s, d = x.shape
    n = b * s
    assert ffn1_norm.shape[0] == 1, "specialised to DEPTH == 1 (the final norm is fused into the layer)"
    pos_row = positions.astype(F32).reshape(1, n)

    h = _ffn1(x.reshape(n, d), ffn1_norm[0][None, :], ffn1_w_gate[0], ffn1_w_up[0],
              ffn1_w_down[0])
    qm, km, vm, rq, rk, rv, rg = _proj(
        h, pos_row, mix_norm[0][None, :], _prep_w_in(w_in[0]), q_norm[0][None, :],
        _prep_w_uq(w_uq[0]), kv_norm[0][None, :], _prep_w_ukv(w_ukv[0]))
    a = _mla_attention(qm, km.reshape(b, s, -1), vm)
    r = _retention(rq.reshape(b, s, -1), rk.reshape(b, s, -1), rv.reshape(b, s, -1),
                   rg.reshape(b, s, -1), _prep_decay(ret_decay_fwd[0], ret_decay_bwd[0]))
    wo = w_o[0].astype(BF16)
    out = _out_ffn2(h, a.reshape(n, -1), r.reshape(n, -1), wo, ffn2_norm[0][None, :],
                    ffn2_w_gate[0], ffn2_w_up[0], ffn2_w_down[0], final_norm[None, :])
    return out.reshape(b, s, d)
```

```python
import functools
import math

import numpy as np
import jax
import jax.numpy as jnp
from jax import lax
from jax.experimental import pallas as pl
from jax.experimental.pallas import tpu as pltpu

F32 = jnp.float32
BF16 = jnp.bfloat16

D_MODEL = 1024
D_FF = 2816
MLA_HEADS = 8
Q_LORA = 256
KV_LORA = 128
NOPE = 64
ROPE = 32
VDIM = 64
RET_HEADS = 8
RET_DIM = 64
D_RET = RET_HEADS * RET_DIM
ROPE_THETA = 10000.0
EPS = 1e-6
MLA_SCALE = (NOPE + ROPE) ** -0.5
LOG2E = math.log2(math.e)

LANES = 128
QUAD = 256
VMEM_LIMIT = 56 * 1024 * 1024

_OFF_CQ = 0
_OFF_CKV = 256
_OFF_RQ = 768
_OFF_RK = 1280
_OFF_RV = 1792
_OFF_RG = 2304
D_IN_AUG = 2816


def _params(sem):
    return pltpu.CompilerParams(dimension_semantics=sem, vmem_limit_bytes=VMEM_LIMIT)


def _const_spec(shape):
    nd = len(shape)
    return pl.BlockSpec(shape, lambda *_: (0,) * nd, pipeline_mode=pl.Buffered(1))


N_FREQ = 64


def _rope_constants():
    row = np.arange(N_FREQ)
    inv = np.where(row < 32, ROPE_THETA ** (-(row % 32) / 32.0),
                   np.where(row < 48, ROPE_THETA ** (-((row - 32) % 16) / 16.0), 0.0))
    pc = np.zeros((N_FREQ, 2 * LANES))
    ps = np.zeros((N_FREQ, 2 * LANES))
    for l in range(LANES):
        pc[l % 32, l] = 1.0
        ps[l % 32, l] = 1.0
    for l in range(NOPE, NOPE + ROPE):
        r = 32 + (l - NOPE) % 16
        pc[r, LANES + l] = 1.0
        ps[r, LANES + l] = -1.0 if l < NOPE + 16 else 1.0
    return jnp.asarray(inv[:, None], F32), jnp.asarray(pc, BF16), jnp.asarray(ps, BF16)


def _rope_tables(pos_row, inv_col, pc, ps):
    ang = inv_col * pos_row
    dn = (((0,), (0,)), ((), ()))

    def place(x, p):
        hi = x.astype(BF16)
        lo = (x - hi.astype(F32)).astype(BF16)
        return (lax.dot_general(hi, p, dn, preferred_element_type=F32)
                + lax.dot_general(lo, p, dn, preferred_element_type=F32))

    return place(jnp.cos(ang), pc), place(jnp.sin(ang), ps)


def _rms(x, g):
    return x * lax.rsqrt(jnp.mean(x * x, axis=-1, keepdims=True) + EPS) * g


def _swiglu_acc(n_bf, wg_ref, wu_ref, wd_ref, ck):
    acc = None
    for j in range(D_FF // ck):
        sl = slice(j * ck, (j + 1) * ck)
        g = jnp.dot(n_bf, wg_ref[:, sl].astype(BF16), preferred_element_type=F32)
        u = jnp.dot(n_bf, wu_ref[:, sl].astype(BF16), preferred_element_type=F32)
        hm = (g * jax.nn.sigmoid(g) * u).astype(BF16)
        d = jnp.dot(hm, wd_ref[sl, :].astype(BF16), preferred_element_type=F32)
        acc = d if acc is None else acc + d
    return acc


def _ffn1_kernel(x_ref, g_ref, wg_ref, wu_ref, wd_ref, o_ref, *, ck):
    x = x_ref[...]
    n_bf = _rms(x, g_ref[...]).astype(BF16)
    o_ref[...] = x + 0.5 * _swiglu_acc(n_bf, wg_ref, wu_ref, wd_ref, ck)


def _ffn1(x, g, wg, wu, wd, tm=512, ck=256):
    n = x.shape[0]
    row = pl.BlockSpec((tm, D_MODEL), lambda i: (i, 0))
    return pl.pallas_call(
        functools.partial(_ffn1_kernel, ck=ck),
        grid=(n // tm,),
        in_specs=[row, _const_spec((1, D_MODEL)), _const_spec((D_MODEL, D_FF)),
                  _const_spec((D_MODEL, D_FF)), _const_spec((D_FF, D_MODEL))],
        out_specs=row,
        out_shape=jax.ShapeDtypeStruct((n, D_MODEL), F32),
        compiler_params=_params(("parallel",)),
        name="ffn1",
    )(x, g, wg, wu, wd)


def _out_ffn2_kernel(h_ref, a_ref, r_ref, wo_ref, g_ref, wg_ref, wu_ref, wd_ref, gf_ref, o_ref,
                     *, ck):
    h = (h_ref[...]
         + jnp.dot(a_ref[...], wo_ref[:MLA_HEADS * VDIM, :], preferred_element_type=F32)
         + jnp.dot(r_ref[...], wo_ref[MLA_HEADS * VDIM:, :], preferred_element_type=F32))
    n_bf = _rms(h, g_ref[...]).astype(BF16)
    h = h + 0.5 * _swiglu_acc(n_bf, wg_ref, wu_ref, wd_ref, ck)
    o_ref[...] = _rms(h, gf_ref[...])


def _out_ffn2(h, a, r, wo, g, wg, wu, wd, gf, tm=512, ck=256):
    n = h.shape[0]
    row = pl.BlockSpec((tm, D_MODEL), lambda i: (i, 0))
    half = pl.BlockSpec((tm, D_RET), lambda i: (i, 0))
    return pl.pallas_call(
        functools.partial(_out_ffn2_kernel, ck=ck),
        grid=(n // tm,),
        in_specs=[row, half, half, _const_spec((MLA_HEADS * VDIM + D_RET, D_MODEL)),
                  _const_spec((1, D_MODEL)), _const_spec((D_MODEL, D_FF)),
                  _const_spec((D_MODEL, D_FF)), _const_spec((D_FF, D_MODEL)),
                  _const_spec((1, D_MODEL))],
        out_specs=row,
        out_shape=jax.ShapeDtypeStruct((n, D_MODEL), F32),
        compiler_params=_params(("parallel",)),
        name="out_ffn2",
    )(h, a, r, wo, g, wg, wu, wd, gf)


def _proj_kernel(h_ref, pos_ref, inv_ref, pc_ref, ps_ref, gm_ref, win_ref, gq_ref, wq_ref, gkv_ref,
                 wkv_ref, qm_ref, km_ref, vm_ref, rq_ref, rk_ref, rv_ref, rg_ref):
    n_bf = _rms(h_ref[...], gm_ref[...]).astype(BF16)

    def proj(off, width):
        return jnp.dot(n_bf, win_ref[:, off:off + width], preferred_element_type=F32)

    def ret_rope(x, out_ref):
        for u in range(D_RET // QUAD):
            x1 = x[:, u * QUAD:u * QUAD + LANES]
            x2 = x[:, u * QUAD + LANES:(u + 1) * QUAD]
            out_ref[:, u * QUAD:u * QUAD + LANES] = (x1 * cr - x2 * sr).astype(BF16)
            out_ref[:, u * QUAD + LANES:(u + 1) * QUAD] = (x1 * sr + x2 * cr).astype(BF16)

    cq_bf = _rms(proj(_OFF_CQ, Q_LORA), gq_ref[...]).astype(BF16)
    lat = proj(_OFF_CKV, 4 * LANES)
    ckv_bf = _rms(lat[:, :KV_LORA], gkv_ref[...]).astype(BF16)
    xq = proj(_OFF_RQ, D_RET)

    cos_t, sin_t = _rope_tables(pos_ref[...], inv_ref[...], pc_ref[...], ps_ref[...])
    cr, sr = cos_t[:, :LANES], sin_t[:, :LANES]
    nope_lane = lax.broadcasted_iota(jnp.int32, (1, LANES), 1) < NOPE
    cm, sm = cos_t[:, LANES:] + jnp.where(nope_lane, 1.0, 0.0), sin_t[:, LANES:]

    xk = proj(_OFF_RK, D_RET)
    ret_rope(xq, rq_ref)
    rv_ref[...] = proj(_OFF_RV, D_RET).astype(BF16)
    ret_rope(xk, rk_ref)
    g = proj(_OFF_RG, D_RET)
    rg_ref[...] = (g * jax.nn.sigmoid(g)).astype(BF16)

    qq = jnp.dot(cq_bf, wq_ref[...], preferred_element_type=F32)
    kv = jnp.dot(ckv_bf, wkv_ref[...], preferred_element_type=F32)
    nh = MLA_HEADS * LANES
    kpe = lat[:, LANES:2 * LANES] * cm + lat[:, 2 * LANES:3 * LANES] * sm
    cq_s, sq_s = cm * (MLA_SCALE * LOG2E), sm * (MLA_SCALE * LOG2E)
    for h in range(MLA_HEADS):
        sl = slice(h * LANES, (h + 1) * LANES)
        sw = slice(nh + h * LANES, nh + (h + 1) * LANES)
        qm_ref[sl, :] = (qq[:, sl] * cq_s + qq[:, sw] * sq_s).astype(BF16).T
        km_ref[:, sl] = (kv[:, sl] + kpe).astype(BF16)
    lane = lax.broadcasted_iota(jnp.int32, (1, nh), 1)
    ones_lane = jnp.where(lane % LANES == VDIM, 1.0, 0.0)
    vm_ref[...] = (kv[:, nh:] + ones_lane).astype(BF16).T


def _proj(h, pos_row, gm, win, gq, wq, gkv, wkv, tm=512):
    n = h.shape[0]
    row = lambda w: pl.BlockSpec((tm, w), lambda i: (i, 0))
    sds = lambda w: jax.ShapeDtypeStruct((n, w), BF16)
    col = pl.BlockSpec((MLA_HEADS * LANES, tm), lambda i: (0, i))
    sds_t = jax.ShapeDtypeStruct((MLA_HEADS * LANES, n), BF16)
    inv_col, pc, ps = _rope_constants()
    return pl.pallas_call(
        _proj_kernel,
        grid=(n // tm,),
        in_specs=[row(D_MODEL), pl.BlockSpec((1, tm), lambda i: (0, i)),
                  _const_spec((N_FREQ, 1)), _const_spec((N_FREQ, 2 * LANES)),
                  _const_spec((N_FREQ, 2 * LANES)), _const_spec((1, D_MODEL)),
                  _const_spec((D_MODEL, D_IN_AUG)), _const_spec((1, Q_LORA)),
                  _const_spec((Q_LORA, 2 * MLA_HEADS * LANES)),
                  _const_spec((1, KV_LORA)), _const_spec((KV_LORA, 2 * MLA_HEADS * LANES))],
        out_specs=[col, row(1024), col, row(512), row(512), row(512), row(512)],
        out_shape=[sds_t, sds(1024), sds_t, sds(512), sds(512), sds(512), sds(512)],
        compiler_params=_params(("parallel",)),
        name="mixer_proj",
    )(h, pos_row, inv_col, pc, ps, gm, win, gq, wq, gkv, wkv)


V_ROWS = 80


def _mla_kernel(qt_ref, k_ref, vt_ref, o_ref, m_ref, acc_ref, s0_ref, s1_ref, mb0_ref, mb1_ref,
                al0_ref, al1_ref, *, tq, tk, tqs, seq, unroll):
    nk = seq // tk
    slots = ((s0_ref, mb0_ref, al0_ref), (s1_ref, mb1_ref, al1_ref))
    m_ref[...] = jnp.full(m_ref.shape, -jnp.inf, F32)
    acc_ref[...] = jnp.zeros(acc_ref.shape, F32)

    def scores(c, slot, qs):
        s_ref, mb_ref, al_ref = slots[slot]
        off = c * tk if isinstance(c, int) else pl.multiple_of(c * tk, tk)
        for h in range(2):
            k = k_ref[0, pl.ds(off, tk), h * LANES:(h + 1) * LANES]
            s = jnp.dot(k, qt_ref[h * LANES:(h + 1) * LANES, qs], preferred_element_type=F32)
            s_ref[h, :, qs] = s
            m_old = m_ref[h, :, qs]
            m_new = jnp.maximum(m_old, jnp.max(s, axis=0, keepdims=True))
            m_ref[h, :, qs] = m_new
            mb_ref[h, :, qs] = m_new
            al_ref[h, :, qs] = jnp.exp2(m_old - m_new)

    def weighted_values(c, slot, qs):
        s_ref, mb_ref, al_ref = slots[slot]
        off = c * tk if isinstance(c, int) else pl.multiple_of(c * tk, tk)
        for h in range(2):
            p = jnp.exp2(s_ref[h, :, qs] - mb_ref[h, 0:1, qs])
            vt = vt_ref[h * LANES:h * LANES + V_ROWS, pl.ds(off, tk)]
            acc_ref[h, :, qs] = (al_ref[h, 0:1, qs] * acc_ref[h, :, qs]
                                 + jnp.dot(vt, p.astype(BF16), preferred_element_type=F32))

    q_slices = [slice(i * tqs, (i + 1) * tqs) for i in range(tq // tqs)]

    def chunk_group(c0, last):
        for j in range(unroll):
            for qs in q_slices:
                if not (last and j == unroll - 1):
                    scores(c0 + j + 1, (j + 1) % 2, qs)
                weighted_values(c0 + j, j % 2, qs)

    for qs in q_slices:
        scores(0, 0, qs)

    def body(i, carry):
        chunk_group(i * unroll, False)
        return carry

    lax.fori_loop(0, nk // unroll - 1, body, 0)
    chunk_group(nk - unroll, True)

    outs = []
    for h in range(2):
        acc = acc_ref[h]
        outs.append(acc[:VDIM, :] / acc[VDIM:VDIM + 1, :])
    o_ref[0] = jnp.concatenate(outs, axis=0).T.astype(BF16)


def _mla_attention(qt, k, vt, tq=2048, tk=512, tqs=256, unroll=2):
    b, s, _ = k.shape
    npair = MLA_HEADS // 2
    assert unroll % 2 == 0 and (s // tk) % unroll == 0
    nq = s // tq
    rowstate = pltpu.VMEM((2, 8, tq), F32)
    sbuf = pltpu.VMEM((2, tk, tq), F32)
    return pl.pallas_call(
        functools.partial(_mla_kernel, tq=tq, tk=tk, tqs=tqs, seq=s, unroll=unroll),
        grid=(b, npair, nq),
        in_specs=[pl.BlockSpec((2 * LANES, tq), lambda bi, p, i: (p, bi * nq + i)),
                  pl.BlockSpec((1, s, 2 * LANES), lambda bi, p, i: (bi, 0, p)),
                  pl.BlockSpec((2 * LANES, s), lambda bi, p, i: (p, bi))],
        out_specs=pl.BlockSpec((1, tq, LANES), lambda bi, p, i: (bi, i, p)),
        out_shape=jax.ShapeDtypeStruct((b, s, MLA_HEADS * VDIM), BF16),
        scratch_shapes=[rowstate, pltpu.VMEM((2, V_ROWS, tq), F32), sbuf, sbuf,
                        rowstate, rowstate, rowstate, rowstate],
        compiler_params=_params(("parallel", "parallel", "parallel")),
        name="mla_attention",
    )(qt, k, vt)


def _log_sigmoid(x):
    return jnp.minimum(x, 0.0) - jnp.log1p(jnp.exp(-jnp.abs(x)))


def _retention_kernel(q_ref, k_ref, v_ref, g_ref, dec_ref, o_ref, acc_lo_ref, acc_hi_ref,
                      stf_ref, stb_ref, dmf_ref, dmb_ref, tab_ref, ones_ref, *, ch, seq, unroll):
    nc = seq // ch
    half = nc // 2
    nh = QUAD // RET_DIM
    lane = lax.broadcasted_iota(jnp.int32, (ch, QUAD), 1)
    row = lax.broadcasted_iota(jnp.int32, (ch, QUAD), 0).astype(F32)
    khead = (lane >> 5) & (nh - 1)
    vhead = lane >> 6
    ri = lax.broadcasted_iota(jnp.int32, (ch, ch), 0)
    ci = lax.broadcasted_iota(jnp.int32, (ch, ch), 1)
    sd = lax.broadcasted_iota(jnp.int32, (QUAD, QUAD), 0)
    se = lax.broadcasted_iota(jnp.int32, (QUAD, QUAD), 1)
    state_mask = ((sd >> 5) & (nh - 1)) == (se >> 6)
    ones_ref[...] = jnp.where((sd >> 6) == (se >> 6), 1.0, 0.0).astype(BF16)

    lg = _log_sigmoid(dec_ref[...])
    tab_ref[0] = jnp.exp(lg[0:1] * (row + 1.0))
    tab_ref[1] = jnp.exp(lg[1:2] * (ch - 1.0 - row))
    tab_ref[2] = jnp.exp(lg[2:3] * (ch - row))
    tab_ref[3] = jnp.exp(lg[3:4] * row)
    c_dec = (jnp.exp(lg[0:1] * float(ch)), jnp.exp(lg[2:3] * float(ch)))
    rel_f = jnp.maximum(ri - ci, 0).astype(F32)
    rel_b = jnp.maximum(ci - ri, 0).astype(F32)
    for h in range(nh):
        dmf_ref[h] = jnp.where(ri >= ci, jnp.exp(lg[0:1, h * RET_DIM:h * RET_DIM + 1] * rel_f), 0.0)
        dmb_ref[h] = jnp.where(ci > ri, jnp.exp(lg[2:3, h * RET_DIM:h * RET_DIM + 1] * rel_b), 0.0)
    stf_ref[...] = jnp.zeros_like(stf_ref)
    stb_ref[...] = jnp.zeros_like(stb_ref)

    def chunks(items):
        loaded = []
        for c, _ in items:
            off = pl.multiple_of(c * ch, ch)
            loaded.append((q_ref[0, pl.ds(off, ch), :], k_ref[0, pl.ds(off, ch), :],
                           v_ref[0, pl.ds(off, ch), :]))
        scores = []
        for qc, kc, _ in loaded:
            zero = jnp.zeros_like(qc)
            q4 = jnp.concatenate([jnp.where(khead == h, qc, zero) for h in range(nh)], axis=0)
            scores.append(lax.dot_general(q4, kc, (((1,), (1,)), ((), ())),
                                          preferred_element_type=F32))
        outs = []
        for (_, fwd), (_, _, vc), s4 in zip(items, loaded, scores):
            dm_ref = dmf_ref if fwd else dmb_ref
            p4 = (s4 * dm_ref[...].reshape(nh * ch, ch)).astype(BF16)
            i4 = jnp.dot(p4, vc, preferred_element_type=F32)
            out = i4[:ch]
            for h in range(1, nh):
                out = jnp.where(vhead == h, i4[h * ch:(h + 1) * ch], out)
            outs.append(out)
        upds = []
        for (_, fwd), (_, kc, vc) in zip(items, loaded):
            kd = (kc.astype(F32) * tab_ref[1 if fwd else 3]).astype(BF16)
            upds.append(lax.dot_general(kd, vc, (((0,), (0,)), ((), ())),
                                        preferred_element_type=F32))
        for i, ((_, fwd), (qc, _, _)) in enumerate(zip(items, loaded)):
            st_ref = stf_ref if fwd else stb_ref
            state = st_ref[...]
            cross = jnp.dot(qc, state.astype(BF16), preferred_element_type=F32)
            outs[i] = outs[i] + cross * tab_ref[0 if fwd else 2]
            st_ref[...] = state * c_dec[0 if fwd else 1] + jnp.where(state_mask, upds[i], 0.0)
        return outs

    def head_sum(x):
        hi = x.astype(BF16)
        lo = (x - hi.astype(F32)).astype(BF16)
        return (jnp.dot(hi, ones_ref[...], preferred_element_type=F32)
                + jnp.dot(lo, ones_ref[...], preferred_element_type=F32))

    def finish(c, tot):
        off = pl.multiple_of(c * ch, ch)
        d = tot - head_sum(tot) * (1.0 / RET_DIM)
        var = head_sum(d * d) * (1.0 / RET_DIM)
        gate = g_ref[0, pl.ds(off, ch), :].astype(F32)
        o_ref[0, pl.ds(off, ch), :] = (d * lax.rsqrt(var + EPS) * gate).astype(BF16)

    def rows(ref, c):
        return ref.at[pl.ds(pl.multiple_of(c * ch, ch), ch), :]

    def pairs(t):
        items = []
        for j in range(unroll):
            cf = t * unroll + j
            items += [(cf, True), (nc - 1 - cf, False)]
        return items

    def first(t, carry):
        items = pairs(t)
        for (c, fwd), out in zip(items, chunks(items)):
            if fwd:
                rows(acc_lo_ref, c)[...] = out
            else:
                rows(acc_hi_ref, c - half)[...] = out
        return carry

    lax.fori_loop(0, half // unroll, first, 0)

    def second(t, carry):
        items = pairs(t)
        for (c, fwd), out in zip(items, chunks(items)):
            other = rows(acc_hi_ref, c - half) if fwd else rows(acc_lo_ref, c)
            finish(c, out + other[...])
        return carry

    lax.fori_loop(half // unroll, nc // unroll, second, 0)


def _retention(q, k, v, g, dec, ch=256, unroll=2):
    b, s, _ = q.shape
    nquad = D_RET // QUAD
    assert (s // ch // 2) % unroll == 0
    blk = pl.BlockSpec((1, s, QUAD), lambda bi, u: (bi, 0, u))
    return pl.pallas_call(
        functools.partial(_retention_kernel, ch=ch, seq=s, unroll=unroll),
        grid=(b, nquad),
        in_specs=[blk, blk, blk, blk, pl.BlockSpec((4, QUAD), lambda bi, u: (0, u))],
        out_specs=blk,
        out_shape=jax.ShapeDtypeStruct((b, s, D_RET), BF16),
        scratch_shapes=[pltpu.VMEM((s // 2, QUAD), F32), pltpu.VMEM((s // 2, QUAD), F32),
                        pltpu.VMEM((QUAD, QUAD), F32), pltpu.VMEM((QUAD, QUAD), F32),
                        pltpu.VMEM((QUAD // RET_DIM, ch, ch), F32),
                        pltpu.VMEM((QUAD // RET_DIM, ch, ch), F32),
                        pltpu.VMEM((4, ch, QUAD), F32), pltpu.VMEM((QUAD, QUAD), BF16)],
        compiler_params=_params(("parallel", "parallel")),
        name="retention",
    )(q, k, v, g, dec)


def _ret_qk_perm():
    new = np.arange(D_RET)
    quad, j = new // QUAD, new % QUAD
    head = 4 * quad + (j % LANES) // 32
    dim = (j // LANES) * 32 + j % 32
    return head * RET_DIM + dim


def _prep_w_in(w_in):
    c_q, c_kv, k_pe, r_q, r_k, r_v, r_g = jnp.split(
        w_in, np.cumsum([Q_LORA, KV_LORA, ROPE, D_RET, D_RET, D_RET])[:], axis=1)
    k = w_in.shape[0]
    z = lambda w: jnp.zeros((k, w), w_in.dtype)
    kpe = jnp.concatenate([z(64), k_pe, z(32)], axis=1)
    kpe_sw = jnp.concatenate([z(64), k_pe[:, 16:], k_pe[:, :16], z(32)], axis=1)
    perm = _ret_qk_perm()
    return jnp.concatenate([c_q, c_kv, kpe, kpe_sw, z(LANES), r_q[:, perm],
                            r_k[:, perm] * (RET_DIM ** -0.5), r_v, r_g], axis=1).astype(BF16)


def _prep_w_uq(w_uq):
    k = w_uq.shape[0]
    w3 = w_uq.reshape(k, MLA_HEADS, NOPE + ROPE)
    nope, x1, x2 = w3[:, :, :NOPE], w3[:, :, NOPE:NOPE + 16], w3[:, :, NOPE + 16:]
    z32 = jnp.zeros((k, MLA_HEADS, 32), w_uq.dtype)
    wq = jnp.concatenate([nope, x1, x2, z32], axis=2).reshape(k, MLA_HEADS * LANES)
    wqs = jnp.concatenate([jnp.zeros_like(nope), x2, x1, z32], axis=2).reshape(k, MLA_HEADS * LANES)
    return jnp.concatenate([wq, wqs], axis=1).astype(BF16)


def _prep_w_ukv(w_ukv):
    k = w_ukv.shape[0]
    w3 = w_ukv.reshape(k, MLA_HEADS, NOPE + VDIM)
    kn = jnp.concatenate([w3[:, :, :NOPE], jnp.zeros((k, MLA_HEADS, 64), w_ukv.dtype)], axis=2)
    v = jnp.concatenate([w3[:, :, NOPE:], jnp.zeros((k, MLA_HEADS, 64), w_ukv.dtype)], axis=2)
    return jnp.concatenate([kn.reshape(k, MLA_HEADS * LANES), v.reshape(k, MLA_HEADS * LANES)],
                           axis=1).astype(BF16)


def _prep_decay(dec_f, dec_b):
    nquad, per_quad = D_RET // QUAD, QUAD // RET_DIM
    rows = []
    for d in (dec_f.astype(F32), dec_b.astype(F32)):
        rows.append(jnp.broadcast_to(d[:, None], (RET_HEADS, RET_DIM)).reshape(D_RET))
        rows.append(jnp.broadcast_to(d.reshape(nquad, 1, per_quad, 1),
                                     (nquad, 2, per_quad, 32)).reshape(D_RET))
    return jnp.stack(rows)


def kernel(x, positions, ffn1_norm, ffn1_w_gate, ffn1_w_up, ffn1_w_down, mix_norm, w_in, q_norm, w_uq, kv_norm, w_ukv, ret_decay_fwd, ret_decay_bwd, w_o, ffn2_norm, ffn2_w_gate, ffn2_w_up, ffn2_w_down, final_norm):
    b, s, d = x.shape
    n = b * s
    assert ffn1_norm.shape[0] == 1, "specialised to DEPTH == 1 (the final norm is fused into the layer)"
    pos_row = positions.astype(F32).reshape(1, n)

    h = _ffn1(x.reshape(n, d), ffn1_norm[0][None, :], ffn1_w_gate[0], ffn1_w_up[0],
              ffn1_w_down[0])
    qm, km, vm, rq, rk, rv, rg = _proj(
        h, pos_row, mix_norm[0][None, :], _prep_w_in(w_in[0]), q_norm[0][None, :],
        _prep_w_uq(w_uq[0]), kv_norm[0][None, :], _prep_w_ukv(w_ukv[0]))
    a = _mla_attention(qm, km.reshape(b, s, -1), vm)
    r = _retention(rq.reshape(b, s, -1), rk.reshape(b, s, -1), rv.reshape(b, s, -1),
                   rg.reshape(b, s, -1), _prep_decay(ret_decay_fwd[0], ret_decay_bwd[0]))
    wo = w_o[0].astype(BF16)
    out = _out_ffn2(h, a.reshape(n, -1), r.reshape(n, -1), wo, ffn2_norm[0][None, :],
                    ffn2_w_gate[0], ffn2_w_up[0], ffn2_w_down[0], final_norm[None, :])
    return out.reshape(b, s, d)
```

```python
import functools
import math

import numpy as np
import jax
import jax.numpy as jnp
from jax import lax
from jax.experimental import pallas as pl
from jax.experimental.pallas import tpu as pltpu

F32 = jnp.float32
BF16 = jnp.bfloat16

D_MODEL = 1024
D_FF = 2816
MLA_HEADS = 8
Q_LORA = 256
KV_LORA = 128
NOPE = 64
ROPE = 32
VDIM = 64
RET_HEADS = 8
RET_DIM = 64
D_RET = RET_HEADS * RET_DIM
ROPE_THETA = 10000.0
EPS = 1e-6
MLA_SCALE = (NOPE + ROPE) ** -0.5
LOG2E = math.log2(math.e)

LANES = 128
QUAD = 256
VMEM_LIMIT = 56 * 1024 * 1024

_OFF_CQ = 0
_OFF_CKV = 256
_OFF_RQ = 768
_OFF_RK = 1280
_OFF_RV = 1792
_OFF_RG = 2304
D_IN_AUG = 2816


def _params(sem):
    return pltpu.CompilerParams(dimension_semantics=sem, vmem_limit_bytes=VMEM_LIMIT)


def _const_spec(shape):
    nd = len(shape)
    return pl.BlockSpec(shape, lambda *_: (0,) * nd, pipeline_mode=pl.Buffered(1))


N_FREQ = 64


def _rope_constants():
    row = np.arange(N_FREQ)
    inv = np.where(row < 32, ROPE_THETA ** (-(row % 32) / 32.0),
                   np.where(row < 48, ROPE_THETA ** (-((row - 32) % 16) / 16.0), 0.0))
    pc = np.zeros((N_FREQ, 2 * LANES))
    ps = np.zeros((N_FREQ, 2 * LANES))
    for l in range(LANES):
        pc[l % 32, l] = 1.0
        ps[l % 32, l] = 1.0
    for l in range(NOPE, NOPE + ROPE):
        r = 32 + (l - NOPE) % 16
        pc[r, LANES + l] = 1.0
        ps[r, LANES + l] = -1.0 if l < NOPE + 16 else 1.0
    return jnp.asarray(inv[:, None], F32), jnp.asarray(pc, BF16), jnp.asarray(ps, BF16)


def _rope_tables(pos_row, inv_col, pc, ps):
    ang = inv_col * pos_row
    dn = (((0,), (0,)), ((), ()))

    def place(x, p):
        hi = x.astype(BF16)
        lo = (x - hi.astype(F32)).astype(BF16)
        return (lax.dot_general(hi, p, dn, preferred_element_type=F32)
                + lax.dot_general(lo, p, dn, preferred_element_type=F32))

    return place(jnp.cos(ang), pc), place(jnp.sin(ang), ps)


def _rms(x, g):
    return x * lax.rsqrt(jnp.mean(x * x, axis=-1, keepdims=True) + EPS) * g


def _swiglu_acc(n_bf, wg_ref, wu_ref, wd_ref, ck):
    acc = None
    for j in range(D_FF // ck):
        sl = slice(j * ck, (j + 1) * ck)
        g = jnp.dot(n_bf, wg_ref[:, sl].astype(BF16), preferred_element_type=F32)
        u = jnp.dot(n_bf, wu_ref[:, sl].astype(BF16), preferred_element_type=F32)
        hm = (g * jax.nn.sigmoid(g) * u).astype(BF16)
        d = jnp.dot(hm, wd_ref[sl, :].astype(BF16), preferred_element_type=F32)
        acc = d if acc is None else acc + d
    return acc


def _ffn1_kernel(x_ref, g_ref, wg_ref, wu_ref, wd_ref, o_ref, *, ck):
    x = x_ref[...]
    n_bf = _rms(x, g_ref[...]).astype(BF16)
    o_ref[...] = x + 0.5 * _swiglu_acc(n_bf, wg_ref, wu_ref, wd_ref, ck)


def _ffn1(x, g, wg, wu, wd, tm=512, ck=256):
    n = x.shape[0]
    row = pl.BlockSpec((tm, D_MODEL), lambda i: (i, 0))
    return pl.pallas_call(
        functools.partial(_ffn1_kernel, ck=ck),
        grid=(n // tm,),
        in_specs=[row, _const_spec((1, D_MODEL)), _const_spec((D_MODEL, D_FF)),
                  _const_spec((D_MODEL, D_FF)), _const_spec((D_FF, D_MODEL))],
        out_specs=row,
        out_shape=jax.ShapeDtypeStruct((n, D_MODEL), F32),
        compiler_params=_params(("parallel",)),
        name="ffn1",
    )(x, g, wg, wu, wd)


def _out_ffn2_kernel(h_ref, a_ref, r_ref, wo_ref, g_ref, wg_ref, wu_ref, wd_ref, gf_ref, o_ref,
                     *, ck):
    h = (h_ref[...]
         + jnp.dot(a_ref[...], wo_ref[:MLA_HEADS * VDIM, :], preferred_element_type=F32)
         + jnp.dot(r_ref[...], wo_ref[MLA_HEADS * VDIM:, :], preferred_element_type=F32))
    n_bf = _rms(h, g_ref[...]).astype(BF16)
    h = h + 0.5 * _swiglu_acc(n_bf, wg_ref, wu_ref, wd_ref, ck)
    o_ref[...] = _rms(h, gf_ref[...])


def _out_ffn2(h, a, r, wo, g, wg, wu, wd, gf, tm=512, ck=256):
    n = h.shape[0]
    row = pl.BlockSpec((tm, D_MODEL), lambda i: (i, 0))
    half = pl.BlockSpec((tm, D_RET), lambda i: (i, 0))
    return pl.pallas_call(
        functools.partial(_out_ffn2_kernel, ck=ck),
        grid=(n // tm,),
        in_specs=[row, half, half, _const_spec((MLA_HEADS * VDIM + D_RET, D_MODEL)),
                  _const_spec((1, D_MODEL)), _const_spec((D_MODEL, D_FF)),
                  _const_spec((D_MODEL, D_FF)), _const_spec((D_FF, D_MODEL)),
                  _const_spec((1, D_MODEL))],
        out_specs=row,
        out_shape=jax.ShapeDtypeStruct((n, D_MODEL), F32),
        compiler_params=_params(("parallel",)),
        name="out_ffn2",
    )(h, a, r, wo, g, wg, wu, wd, gf)


def _stage_w_in(w_ref, out_ref, rows=128):
    o_kpe = Q_LORA + KV_LORA
    o_rq = o_kpe + ROPE
    for r0 in range(0, D_MODEL, rows):
        x = w_ref[r0:r0 + rows, :]

        def put(off, val):
            out_ref[r0:r0 + rows, off:off + val.shape[1]] = val.astype(BF16)

        zeros = lambda w: jnp.zeros((rows, w), F32)
        put(_OFF_CQ, x[:, :o_kpe])
        kpe = x[:, o_kpe:o_rq]
        put(_OFF_CKV + LANES, jnp.concatenate([zeros(NOPE), kpe, zeros(32)], axis=1))
        put(_OFF_CKV + 2 * LANES,
            jnp.concatenate([zeros(NOPE), kpe[:, 16:], kpe[:, :16], zeros(32)], axis=1))
        put(_OFF_CKV + 3 * LANES, zeros(LANES))
        for off_out, off_in, scale in ((_OFF_RQ, o_rq, 1.0), (_OFF_RK, o_rq + D_RET, RET_DIM ** -0.5)):
            for u in range(D_RET // QUAD):
                for part in range(2):
                    src = [off_in + (4 * u + j) * RET_DIM + part * 32 for j in range(4)]
                    blk = jnp.concatenate([x[:, s:s + 32] for s in src], axis=1)
                    put(off_out + u * QUAD + part * LANES, blk * scale if scale != 1.0 else blk)
        put(_OFF_RV, x[:, o_rq + 2 * D_RET:o_rq + 4 * D_RET])


def _stage_w_uq(w_ref, out_ref):
    x = w_ref[...]
    rows = x.shape[0]
    zeros = lambda w: jnp.zeros((rows, w), F32)
    nh = MLA_HEADS * LANES
    for h in range(MLA_HEADS):
        head = x[:, h * (NOPE + ROPE):(h + 1) * (NOPE + ROPE)]
        out_ref[:, h * LANES:(h + 1) * LANES] = jnp.concatenate(
            [head, zeros(32)], axis=1).astype(BF16)
        out_ref[:, nh + h * LANES:nh + (h + 1) * LANES] = jnp.concatenate(
            [zeros(NOPE), head[:, NOPE + 16:], head[:, NOPE:NOPE + 16], zeros(32)],
            axis=1).astype(BF16)


def _stage_w_ukv(w_ref, out_ref):
    x = w_ref[...]
    zeros = jnp.zeros((x.shape[0], LANES - NOPE), F32)
    nh = MLA_HEADS * LANES
    for h in range(MLA_HEADS):
        src = h * (NOPE + VDIM)
        out_ref[:, h * LANES:(h + 1) * LANES] = jnp.concatenate(
            [x[:, src:src + NOPE], zeros], axis=1).astype(BF16)
        out_ref[:, nh + h * LANES:nh + (h + 1) * LANES] = jnp.concatenate(
            [x[:, src + NOPE:src + NOPE + VDIM], zeros], axis=1).astype(BF16)


def _proj_kernel(h_ref, pos_ref, inv_ref, pc_ref, ps_ref, gm_ref, w_in_ref, gq_ref, w_uq_ref,
                 gkv_ref, w_ukv_ref, qm_ref, km_ref, vm_ref, rq_ref, rk_ref, rv_ref, rg_ref,
                 win_ref, wq_ref, wkv_ref):
    @pl.when(pl.program_id(0) == 0)
    def _():
        _stage_w_in(w_in_ref, win_ref)
        _stage_w_uq(w_uq_ref, wq_ref)
        _stage_w_ukv(w_ukv_ref, wkv_ref)

    n_bf = _rms(h_ref[...], gm_ref[...]).astype(BF16)

    def proj(off, width):
        return jnp.dot(n_bf, win_ref[:, off:off + width], preferred_element_type=F32)

    def ret_rope(x, out_ref):
        for u in range(D_RET // QUAD):
            x1 = x[:, u * QUAD:u * QUAD + LANES]
            x2 = x[:, u * QUAD + LANES:(u + 1) * QUAD]
            out_ref[:, u * QUAD:u * QUAD + LANES] = (x1 * cr - x2 * sr).astype(BF16)
            out_ref[:, u * QUAD + LANES:(u + 1) * QUAD] = (x1 * sr + x2 * cr).astype(BF16)

    cq_bf = _rms(proj(_OFF_CQ, Q_LORA), gq_ref[...]).astype(BF16)
    lat = proj(_OFF_CKV, 4 * LANES)
    ckv_bf = _rms(lat[:, :KV_LORA], gkv_ref[...]).astype(BF16)
    xq = proj(_OFF_RQ, D_RET)

    cos_t, sin_t = _rope_tables(pos_ref[...], inv_ref[...], pc_ref[...], ps_ref[...])
    cr, sr = cos_t[:, :LANES], sin_t[:, :LANES]
    nope_lane = lax.broadcasted_iota(jnp.int32, (1, LANES), 1) < NOPE
    cm, sm = cos_t[:, LANES:] + jnp.where(nope_lane, 1.0, 0.0), sin_t[:, LANES:]

    xk = proj(_OFF_RK, D_RET)
    ret_rope(xq, rq_ref)
    rv_ref[...] = proj(_OFF_RV, D_RET).astype(BF16)
    ret_rope(xk, rk_ref)
    g = proj(_OFF_RG, D_RET)
    rg_ref[...] = (g * jax.nn.sigmoid(g)).astype(BF16)

    qq = jnp.dot(cq_bf, wq_ref[...], preferred_element_type=F32)
    kv = jnp.dot(ckv_bf, wkv_ref[...], preferred_element_type=F32)
    nh = MLA_HEADS * LANES
    kpe = lat[:, LANES:2 * LANES] * cm + lat[:, 2 * LANES:3 * LANES] * sm
    cq_s, sq_s = cm * (MLA_SCALE * LOG2E), sm * (MLA_SCALE * LOG2E)
    for h in range(MLA_HEADS):
        sl = slice(h * LANES, (h + 1) * LANES)
        sw = slice(nh + h * LANES, nh + (h + 1) * LANES)
        qm_ref[sl, :] = (qq[:, sl] * cq_s + qq[:, sw] * sq_s).astype(BF16).T
        km_ref[:, sl] = (kv[:, sl] + kpe).astype(BF16)
    lane = lax.broadcasted_iota(jnp.int32, (1, nh), 1)
    ones_lane = jnp.where(lane % LANES == VDIM, 1.0, 0.0)
    vm_ref[...] = (kv[:, nh:] + ones_lane).astype(BF16).T


def _proj(h, pos_row, gm, w_in, gq, w_uq, gkv, w_ukv, tm=512):
    n = h.shape[0]
    row = lambda w: pl.BlockSpec((tm, w), lambda i: (i, 0))
    sds = lambda w: jax.ShapeDtypeStruct((n, w), BF16)
    col = pl.BlockSpec((MLA_HEADS * LANES, tm), lambda i: (0, i))
    sds_t = jax.ShapeDtypeStruct((MLA_HEADS * LANES, n), BF16)
    inv_col, pc, ps = _rope_constants()
    return pl.pallas_call(
        _proj_kernel,
        grid=(n // tm,),
        in_specs=[row(D_MODEL), pl.BlockSpec((1, tm), lambda i: (0, i)),
                  _const_spec((N_FREQ, 1)), _const_spec((N_FREQ, 2 * LANES)),
                  _const_spec((N_FREQ, 2 * LANES)), _const_spec((1, D_MODEL)),
                  _const_spec(w_in.shape), _const_spec((1, Q_LORA)), _const_spec(w_uq.shape),
                  _const_spec((1, KV_LORA)), _const_spec(w_ukv.shape)],
        out_specs=[col, row(1024), col, row(512), row(512), row(512), row(512)],
        out_shape=[sds_t, sds(1024), sds_t, sds(512), sds(512), sds(512), sds(512)],
        scratch_shapes=[pltpu.VMEM((D_MODEL, D_IN_AUG), BF16),
                        pltpu.VMEM((Q_LORA, 2 * MLA_HEADS * LANES), BF16),
                        pltpu.VMEM((KV_LORA, 2 * MLA_HEADS * LANES), BF16)],
        compiler_params=_params(("arbitrary",)),
        name="mixer_proj",
    )(h, pos_row, inv_col, pc, ps, gm, w_in, gq, w_uq, gkv, w_ukv)


V_ROWS = 80


def _mla_kernel(qt_ref, k_ref, vt_ref, o_ref, m_ref, acc_ref, s0_ref, s1_ref, mb0_ref, mb1_ref,
                al0_ref, al1_ref, *, tq, tk, tqs, seq, unroll):
    nk = seq // tk
    slots = ((s0_ref, mb0_ref, al0_ref), (s1_ref, mb1_ref, al1_ref))
    m_ref[...] = jnp.full(m_ref.shape, -jnp.inf, F32)
    acc_ref[...] = jnp.zeros(acc_ref.shape, F32)

    def scores(c, slot, qs):
        s_ref, mb_ref, al_ref = slots[slot]
        off = c * tk if isinstance(c, int) else pl.multiple_of(c * tk, tk)
        for h in range(2):
            k = k_ref[0, pl.ds(off, tk), h * LANES:(h + 1) * LANES]
            s = jnp.dot(k, qt_ref[h * LANES:(h + 1) * LANES, qs], preferred_element_type=F32)
            s_ref[h, :, qs] = s
            m_old = m_ref[h, :, qs]
            m_new = jnp.maximum(m_old, jnp.max(s, axis=0, keepdims=True))
            m_ref[h, :, qs] = m_new
            mb_ref[h, :, qs] = m_new
            al_ref[h, :, qs] = jnp.exp2(m_old - m_new)

    def weighted_values(c, slot, qs):
        s_ref, mb_ref, al_ref = slots[slot]
        off = c * tk if isinstance(c, int) else pl.multiple_of(c * tk, tk)
        for h in range(2):
            p = jnp.exp2(s_ref[h, :, qs] - mb_ref[h, 0:1, qs])
            vt = vt_ref[h * LANES:h * LANES + V_ROWS, pl.ds(off, tk)]
            acc_ref[h, :, qs] = (al_ref[h, 0:1, qs] * acc_ref[h, :, qs]
                                 + jnp.dot(vt, p.astype(BF16), preferred_element_type=F32))

    q_slices = [slice(i * tqs, (i + 1) * tqs) for i in range(tq // tqs)]

    def chunk_group(c0, last):
        for j in range(unroll):
            for qs in q_slices:
                if not (last and j == unroll - 1):
                    scores(c0 + j + 1, (j + 1) % 2, qs)
                weighted_values(c0 + j, j % 2, qs)

    for qs in q_slices:
        scores(0, 0, qs)

    def body(i, carry):
        chunk_group(i * unroll, False)
        return carry

    lax.fori_loop(0, nk // unroll - 1, body, 0)
    chunk_group(nk - unroll, True)

    outs = []
    for h in range(2):
        acc = acc_ref[h]
        outs.append(acc[:VDIM, :] / acc[VDIM:VDIM + 1, :])
    o_ref[0] = jnp.concatenate(outs, axis=0).T.astype(BF16)


def _mla_attention(qt, k, vt, tq=2048, tk=512, tqs=256, unroll=2):
    b, s, _ = k.shape
    npair = MLA_HEADS // 2
    assert unroll % 2 == 0 and (s // tk) % unroll == 0
    nq = s // tq
    rowstate = pltpu.VMEM((2, 8, tq), F32)
    sbuf = pltpu.VMEM((2, tk, tq), F32)
    return pl.pallas_call(
        functools.partial(_mla_kernel, tq=tq, tk=tk, tqs=tqs, seq=s, unroll=unroll),
        grid=(b, npair, nq),
        in_specs=[pl.BlockSpec((2 * LANES, tq), lambda bi, p, i: (p, bi * nq + i)),
                  pl.BlockSpec((1, s, 2 * LANES), lambda bi, p, i: (bi, 0, p)),
                  pl.BlockSpec((2 * LANES, s), lambda bi, p, i: (p, bi))],
        out_specs=pl.BlockSpec((1, tq, LANES), lambda bi, p, i: (bi, i, p)),
        out_shape=jax.ShapeDtypeStruct((b, s, MLA_HEADS * VDIM), BF16),
        scratch_shapes=[rowstate, pltpu.VMEM((2, V_ROWS, tq), F32), sbuf, sbuf,
                        rowstate, rowstate, rowstate, rowstate],
        compiler_params=_params(("parallel", "parallel", "parallel")),
        name="mla_attention",
    )(qt, k, vt)


def _log_sigmoid(x):
    return jnp.minimum(x, 0.0) - jnp.log1p(jnp.exp(-jnp.abs(x)))


def _retention_kernel(q_ref, k_ref, v_ref, g_ref, dec_ref, o_ref, acc_lo_ref, acc_hi_ref,
                      stf_ref, stb_ref, dmf_ref, dmb_ref, tab_ref, ones_ref, *, ch, seq, unroll):
    nc = seq // ch
    half = nc // 2
    nh = QUAD // RET_DIM
    lane = lax.broadcasted_iota(jnp.int32, (ch, QUAD), 1)
    row = lax.broadcasted_iota(jnp.int32, (ch, QUAD), 0).astype(F32)
    khead = (lane >> 5) & (nh - 1)
    vhead = lane >> 6
    ri = lax.broadcasted_iota(jnp.int32, (ch, ch), 0)
    ci = lax.broadcasted_iota(jnp.int32, (ch, ch), 1)
    sd = lax.broadcasted_iota(jnp.int32, (QUAD, QUAD), 0)
    se = lax.broadcasted_iota(jnp.int32, (QUAD, QUAD), 1)
    state_mask = ((sd >> 5) & (nh - 1)) == (se >> 6)
    ones_ref[...] = jnp.where((sd >> 6) == (se >> 6), 1.0, 0.0).astype(BF16)

    lg = _log_sigmoid(dec_ref[...])
    tab_ref[0] = jnp.exp(lg[0:1] * (row + 1.0))
    tab_ref[1] = jnp.exp(lg[1:2] * (ch - 1.0 - row))
    tab_ref[2] = jnp.exp(lg[2:3] * (ch - row))
    tab_ref[3] = jnp.exp(lg[3:4] * row)
    c_dec = (jnp.exp(lg[0:1] * float(ch)), jnp.exp(lg[2:3] * float(ch)))
    rel_f = jnp.maximum(ri - ci, 0).astype(F32)
    rel_b = jnp.maximum(ci - ri, 0).astype(F32)
    for h in range(nh):
        dmf_ref[h] = jnp.where(ri >= ci, jnp.exp(lg[0:1, h * RET_DIM:h * RET_DIM + 1] * rel_f), 0.0)
        dmb_ref[h] = jnp.where(ci > ri, jnp.exp(lg[2:3, h * RET_DIM:h * RET_DIM + 1] * rel_b), 0.0)
    stf_ref[...] = jnp.zeros_like(stf_ref)
    stb_ref[...] = jnp.zeros_like(stb_ref)

    def chunks(items):
        loaded = []
        for c, _ in items:
            off = pl.multiple_of(c * ch, ch)
            loaded.append((q_ref[0, pl.ds(off, ch), :], k_ref[0, pl.ds(off, ch), :],
                           v_ref[0, pl.ds(off, ch), :]))
        scores = []
        for qc, kc, _ in loaded:
            zero = jnp.zeros_like(qc)
            scores.append([lax.dot_general(jnp.where(khead == h, qc, zero), kc,
                                           (((1,), (1,)), ((), ())), preferred_element_type=F32)
                           for h in range(nh)])
        outs = []
        for (_, fwd), (_, _, vc), ss in zip(items, loaded, scores):
            dm_ref = dmf_ref if fwd else dmb_ref
            out = None
            for h in range(nh):
                ih = jnp.dot((ss[h] * dm_ref[h]).astype(BF16), vc, preferred_element_type=F32)
                out = ih if out is None else jnp.where(vhead == h, ih, out)
            outs.append(out)
        upds = []
        for (_, fwd), (_, kc, vc) in zip(items, loaded):
            kd = (kc.astype(F32) * tab_ref[1 if fwd else 3]).astype(BF16)
            upds.append(lax.dot_general(kd, vc, (((0,), (0,)), ((), ())),
                                        preferred_element_type=F32))
        for i, ((_, fwd), (qc, _, _)) in enumerate(zip(items, loaded)):
            st_ref = stf_ref if fwd else stb_ref
            state = st_ref[...]
            cross = jnp.dot(qc, state.astype(BF16), preferred_element_type=F32)
            outs[i] = outs[i] + cross * tab_ref[0 if fwd else 2]
            st_ref[...] = state * c_dec[0 if fwd else 1] + jnp.where(state_mask, upds[i], 0.0)
        return outs

    def head_sum(x):
        hi = x.astype(BF16)
        lo = (x - hi.astype(F32)).astype(BF16)
        return (jnp.dot(hi, ones_ref[...], preferred_element_type=F32)
                + jnp.dot(lo, ones_ref[...], preferred_element_type=F32))

    def finish(c, tot):
        off = pl.multiple_of(c * ch, ch)
        d = tot - head_sum(tot) * (1.0 / RET_DIM)
        var = head_sum(d * d) * (1.0 / RET_DIM)
        gate = g_ref[0, pl.ds(off, ch), :].astype(F32)
        o_ref[0, pl.ds(off, ch), :] = (d * lax.rsqrt(var + EPS) * gate).astype(BF16)

    def rows(ref, c):
        return ref.at[pl.ds(pl.multiple_of(c * ch, ch), ch), :]

    def pairs(t):
        items = []
        for j in range(unroll):
            cf = t * unroll + j
            items += [(cf, True), (nc - 1 - cf, False)]
        return items

    def first(t, carry):
        items = pairs(t)
        for (c, fwd), out in zip(items, chunks(items)):
            if fwd:
                rows(acc_lo_ref, c)[...] = out
            else:
                rows(acc_hi_ref, c - half)[...] = out
        return carry

    lax.fori_loop(0, half // unroll, first, 0)

    def second(t, carry):
        items = pairs(t)
        for (c, fwd), out in zip(items, chunks(items)):
            other = rows(acc_hi_ref, c - half) if fwd else rows(acc_lo_ref, c)
            finish(c, out + other[...])
        return carry

    lax.fori_loop(half // unroll, nc // unroll, second, 0)


def _retention(q, k, v, g, dec, ch=256, unroll=2):
    b, s, _ = q.shape
    nquad = D_RET // QUAD
    assert (s // ch // 2) % unroll == 0
    blk = pl.BlockSpec((1, s, QUAD), lambda bi, u: (bi, 0, u))
    return pl.pallas_call(
        functools.partial(_retention_kernel, ch=ch, seq=s, unroll=unroll),
        grid=(b, nquad),
        in_specs=[blk, blk, blk, blk, pl.BlockSpec((4, QUAD), lambda bi, u: (0, u))],
        out_specs=blk,
        out_shape=jax.ShapeDtypeStruct((b, s, D_RET), BF16),
        scratch_shapes=[pltpu.VMEM((s // 2, QUAD), F32), pltpu.VMEM((s // 2, QUAD), F32),
                        pltpu.VMEM((QUAD, QUAD), F32), pltpu.VMEM((QUAD, QUAD), F32),
                        pltpu.VMEM((QUAD // RET_DIM, ch, ch), F32),
                        pltpu.VMEM((QUAD // RET_DIM, ch, ch), F32),
                        pltpu.VMEM((4, ch, QUAD), F32), pltpu.VMEM((QUAD, QUAD), BF16)],
        compiler_params=_params(("parallel", "parallel")),
        name="retention",
    )(q, k, v, g, dec)


def _prep_decay(dec_f, dec_b):
    nquad, per_quad = D_RET // QUAD, QUAD // RET_DIM
    rows = []
    for d in (dec_f.astype(F32), dec_b.astype(F32)):
        rows.append(jnp.broadcast_to(d[:, None], (RET_HEADS, RET_DIM)).reshape(D_RET))
        rows.append(jnp.broadcast_to(d.reshape(nquad, 1, per_quad, 1),
                                     (nquad, 2, per_quad, 32)).reshape(D_RET))
    return jnp.stack(rows)


def kernel(x, positions, ffn1_norm, ffn1_w_gate, ffn1_w_up, ffn1_w_down, mix_norm, w_in, q_norm, w_uq, kv_norm, w_ukv, ret_decay_fwd, ret_decay_bwd, w_o, ffn2_norm, ffn2_w_gate, ffn2_w_up, ffn2_w_down, final_norm):
    b, s, d = x.shape
    n = b * s
    assert ffn1_norm.shape[0] == 1, "specialised to DEPTH == 1 (the final norm is fused into the layer)"
    pos_row = positions.astype(F32).reshape(1, n)

    h = _ffn1(x.reshape(n, d), ffn1_norm[0][None, :], ffn1_w_gate[0], ffn1_w_up[0],
              ffn1_w_down[0])
    qm, km, vm, rq, rk, rv, rg = _proj(
        h, pos_row, mix_norm[0][None, :], w_in[0], q_norm[0][None, :],
        w_uq[0], kv_norm[0][None, :], w_ukv[0])
    a = _mla_attention(qm, km.reshape(b, s, -1), vm)
    r = _retention(rq.reshape(b, s, -1), rk.reshape(b, s, -1), rv.reshape(b, s, -1),
                   rg.reshape(b, s, -1), _prep_decay(ret_decay_fwd[0], ret_decay_bwd[0]))
    wo = w_o[0].astype(BF16)
    out = _out_ffn2(h, a.reshape(n, -1), r.reshape(n, -1), wo, ffn2_norm[0][None, :],
                    ffn2_w_gate[0], ffn2_w_up[0], ffn2_w_down[0], final_norm[None, :])
    return out.reshape(b, s, d)
```

```python
import functools
import math

import numpy as np
import jax
import jax.numpy as jnp
from jax import lax
from jax.experimental import pallas as pl
from jax.experimental.pallas import tpu as pltpu

F32 = jnp.float32
BF16 = jnp.bfloat16

D_MODEL = 1024
D_FF = 2816
MLA_HEADS = 8
Q_LORA = 256
KV_LORA = 128
NOPE = 64
ROPE = 32
VDIM = 64
RET_HEADS = 8
RET_DIM = 64
D_RET = RET_HEADS * RET_DIM
ROPE_THETA = 10000.0
EPS = 1e-6
MLA_SCALE = (NOPE + ROPE) ** -0.5
LOG2E = math.log2(math.e)

LANES = 128
QUAD = 256
VMEM_LIMIT = 56 * 1024 * 1024

_OFF_CQ = 0
_OFF_CKV = 256
_OFF_RQ = 768
_OFF_RK = 1280
_OFF_RV = 1792
_OFF_RG = 2304
D_IN_AUG = 2816


def _params(sem):
    return pltpu.CompilerParams(dimension_semantics=sem, vmem_limit_bytes=VMEM_LIMIT)


def _const_spec(shape):
    nd = len(shape)
    return pl.BlockSpec(shape, lambda *_: (0,) * nd, pipeline_mode=pl.Buffered(1))


N_FREQ = 64


def _rope_constants():
    row = np.arange(N_FREQ)
    inv = np.where(row < 32, ROPE_THETA ** (-(row % 32) / 32.0),
                   np.where(row < 48, ROPE_THETA ** (-((row - 32) % 16) / 16.0), 0.0))
    pc = np.zeros((N_FREQ, 2 * LANES))
    ps = np.zeros((N_FREQ, 2 * LANES))
    for l in range(LANES):
        pc[l % 32, l] = 1.0
        ps[l % 32, l] = 1.0
    for l in range(NOPE, NOPE + ROPE):
        r = 32 + (l - NOPE) % 16
        pc[r, LANES + l] = 1.0
        ps[r, LANES + l] = -1.0 if l < NOPE + 16 else 1.0
    return jnp.asarray(inv[:, None], F32), jnp.asarray(pc, BF16), jnp.asarray(ps, BF16)


def _rope_tables(pos_row, inv_col, pc, ps):
    ang = inv_col * pos_row
    dn = (((0,), (0,)), ((), ()))

    def place(x, p):
        hi = x.astype(BF16)
        lo = (x - hi.astype(F32)).astype(BF16)
        return (lax.dot_general(hi, p, dn, preferred_element_type=F32)
                + lax.dot_general(lo, p, dn, preferred_element_type=F32))

    return place(jnp.cos(ang), pc), place(jnp.sin(ang), ps)


def _rms(x, g):
    return x * lax.rsqrt(jnp.mean(x * x, axis=-1, keepdims=True) + EPS) * g


def _swiglu_acc(n_bf, wg_ref, wu_ref, wd_ref, ck):
    acc = None
    for j in range(D_FF // ck):
        sl = slice(j * ck, (j + 1) * ck)
        g = jnp.dot(n_bf, wg_ref[:, sl].astype(BF16), preferred_element_type=F32)
        u = jnp.dot(n_bf, wu_ref[:, sl].astype(BF16), preferred_element_type=F32)
        hm = (g * jax.nn.sigmoid(g) * u).astype(BF16)
        d = jnp.dot(hm, wd_ref[sl, :].astype(BF16), preferred_element_type=F32)
        acc = d if acc is None else acc + d
    return acc


def _ffn1_kernel(x_ref, g_ref, wg_ref, wu_ref, wd_ref, o_ref, *, ck):
    x = x_ref[...]
    n_bf = _rms(x, g_ref[...]).astype(BF16)
    o_ref[...] = x + 0.5 * _swiglu_acc(n_bf, wg_ref, wu_ref, wd_ref, ck)


def _ffn1(x, g, wg, wu, wd, tm=512, ck=256):
    n = x.shape[0]
    row = pl.BlockSpec((tm, D_MODEL), lambda i: (i, 0))
    return pl.pallas_call(
        functools.partial(_ffn1_kernel, ck=ck),
        grid=(n // tm,),
        in_specs=[row, _const_spec((1, D_MODEL)), _const_spec((D_MODEL, D_FF)),
                  _const_spec((D_MODEL, D_FF)), _const_spec((D_FF, D_MODEL))],
        out_specs=row,
        out_shape=jax.ShapeDtypeStruct((n, D_MODEL), F32),
        compiler_params=_params(("parallel",)),
        name="ffn1",
    )(x, g, wg, wu, wd)


def _out_ffn2_kernel(h_ref, a_ref, r_ref, wo_ref, g_ref, wg_ref, wu_ref, wd_ref, gf_ref, o_ref,
                     *, ck):
    h = (h_ref[...]
         + jnp.dot(a_ref[...], wo_ref[:MLA_HEADS * VDIM, :], preferred_element_type=F32)
         + jnp.dot(r_ref[...], wo_ref[MLA_HEADS * VDIM:, :], preferred_element_type=F32))
    n_bf = _rms(h, g_ref[...]).astype(BF16)
    h = h + 0.5 * _swiglu_acc(n_bf, wg_ref, wu_ref, wd_ref, ck)
    o_ref[...] = _rms(h, gf_ref[...])


def _out_ffn2(h, a, r, wo, g, wg, wu, wd, gf, tm=512, ck=256):
    n = h.shape[0]
    row = pl.BlockSpec((tm, D_MODEL), lambda i: (i, 0))
    half = pl.BlockSpec((tm, D_RET), lambda i: (i, 0))
    return pl.pallas_call(
        functools.partial(_out_ffn2_kernel, ck=ck),
        grid=(n // tm,),
        in_specs=[row, half, half, _const_spec((MLA_HEADS * VDIM + D_RET, D_MODEL)),
                  _const_spec((1, D_MODEL)), _const_spec((D_MODEL, D_FF)),
                  _const_spec((D_MODEL, D_FF)), _const_spec((D_FF, D_MODEL)),
                  _const_spec((1, D_MODEL))],
        out_specs=row,
        out_shape=jax.ShapeDtypeStruct((n, D_MODEL), F32),
        compiler_params=_params(("parallel",)),
        name="out_ffn2",
    )(h, a, r, wo, g, wg, wu, wd, gf)


def _stage_w_in(w_ref, out_ref, rows=128):
    o_kpe = Q_LORA + KV_LORA
    o_rq = o_kpe + ROPE
    for r0 in range(0, D_MODEL, rows):
        x = w_ref[r0:r0 + rows, :]

        def put(off, val):
            out_ref[r0:r0 + rows, off:off + val.shape[1]] = val.astype(BF16)

        zeros = lambda w: jnp.zeros((rows, w), F32)
        put(_OFF_CQ, x[:, :o_kpe])
        kpe = x[:, o_kpe:o_rq]
        put(_OFF_CKV + LANES, jnp.concatenate([zeros(NOPE), kpe, zeros(32)], axis=1))
        put(_OFF_CKV + 2 * LANES,
            jnp.concatenate([zeros(NOPE), kpe[:, 16:], kpe[:, :16], zeros(32)], axis=1))
        put(_OFF_CKV + 3 * LANES, zeros(LANES))
        for off_out, off_in, scale in ((_OFF_RQ, o_rq, 1.0), (_OFF_RK, o_rq + D_RET, RET_DIM ** -0.5)):
            for u in range(D_RET // QUAD):
                for part in range(2):
                    src = [off_in + (4 * u + j) * RET_DIM + part * 32 for j in range(4)]
                    blk = jnp.concatenate([x[:, s:s + 32] for s in src], axis=1)
                    put(off_out + u * QUAD + part * LANES, blk * scale if scale != 1.0 else blk)
        put(_OFF_RV, x[:, o_rq + 2 * D_RET:o_rq + 4 * D_RET])


def _stage_w_uq(w_ref, out_ref):
    x = w_ref[...]
    rows = x.shape[0]
    zeros = lambda w: jnp.zeros((rows, w), F32)
    nh = MLA_HEADS * LANES
    for h in range(MLA_HEADS):
        head = x[:, h * (NOPE + ROPE):(h + 1) * (NOPE + ROPE)]
        out_ref[:, h * LANES:(h + 1) * LANES] = jnp.concatenate(
            [head, zeros(32)], axis=1).astype(BF16)
        out_ref[:, nh + h * LANES:nh + (h + 1) * LANES] = jnp.concatenate(
            [zeros(NOPE), head[:, NOPE + 16:], head[:, NOPE:NOPE + 16], zeros(32)],
            axis=1).astype(BF16)


def _stage_w_ukv(w_ref, out_ref):
    x = w_ref[...]
    zeros = jnp.zeros((x.shape[0], LANES - NOPE), F32)
    nh = MLA_HEADS * LANES
    for h in range(MLA_HEADS):
        src = h * (NOPE + VDIM)
        out_ref[:, h * LANES:(h + 1) * LANES] = jnp.concatenate(
            [x[:, src:src + NOPE], zeros], axis=1).astype(BF16)
        out_ref[:, nh + h * LANES:nh + (h + 1) * LANES] = jnp.concatenate(
            [x[:, src + NOPE:src + NOPE + VDIM], zeros], axis=1).astype(BF16)


def _proj_kernel(h_ref, pos_ref, inv_ref, pc_ref, ps_ref, gm_ref, w_in_ref, gq_ref, w_uq_ref,
                 gkv_ref, w_ukv_ref, qm_ref, km_ref, vm_ref, rq_ref, rk_ref, rv_ref, rg_ref,
                 win_ref, wq_ref, wkv_ref):
    @pl.when(pl.program_id(0) == 0)
    def _():
        _stage_w_in(w_in_ref, win_ref)
        _stage_w_uq(w_uq_ref, wq_ref)
        _stage_w_ukv(w_ukv_ref, wkv_ref)

    n_bf = _rms(h_ref[...], gm_ref[...]).astype(BF16)

    def proj(off, width):
        return jnp.dot(n_bf, win_ref[:, off:off + width], preferred_element_type=F32)

    def ret_rope(x, out_ref):
        for u in range(D_RET // QUAD):
            x1 = x[:, u * QUAD:u * QUAD + LANES]
            x2 = x[:, u * QUAD + LANES:(u + 1) * QUAD]
            out_ref[:, u * QUAD:u * QUAD + LANES] = (x1 * cr - x2 * sr).astype(BF16)
            out_ref[:, u * QUAD + LANES:(u + 1) * QUAD] = (x1 * sr + x2 * cr).astype(BF16)

    cq_bf = _rms(proj(_OFF_CQ, Q_LORA), gq_ref[...]).astype(BF16)
    lat = proj(_OFF_CKV, 4 * LANES)
    ckv_bf = _rms(lat[:, :KV_LORA], gkv_ref[...]).astype(BF16)
    xq = proj(_OFF_RQ, D_RET)

    cos_t, sin_t = _rope_tables(pos_ref[...], inv_ref[...], pc_ref[...], ps_ref[...])
    cr, sr = cos_t[:, :LANES], sin_t[:, :LANES]
    nope_lane = lax.broadcasted_iota(jnp.int32, (1, LANES), 1) < NOPE
    cm, sm = cos_t[:, LANES:] + jnp.where(nope_lane, 1.0, 0.0), sin_t[:, LANES:]

    xk = proj(_OFF_RK, D_RET)
    ret_rope(xq, rq_ref)
    rv_ref[...] = proj(_OFF_RV, D_RET).astype(BF16)
    ret_rope(xk, rk_ref)
    g = proj(_OFF_RG, D_RET)
    rg_ref[...] = (g * jax.nn.sigmoid(g)).astype(BF16)

    qq = jnp.dot(cq_bf, wq_ref[...], preferred_element_type=F32)
    kv = jnp.dot(ckv_bf, wkv_ref[...], preferred_element_type=F32)
    nh = MLA_HEADS * LANES
    kpe = lat[:, LANES:2 * LANES] * cm + lat[:, 2 * LANES:3 * LANES] * sm
    cq_s, sq_s = cm * (MLA_SCALE * LOG2E), sm * (MLA_SCALE * LOG2E)
    for h in range(MLA_HEADS):
        sl = slice(h * LANES, (h + 1) * LANES)
        sw = slice(nh + h * LANES, nh + (h + 1) * LANES)
        qm_ref[sl, :] = (qq[:, sl] * cq_s + qq[:, sw] * sq_s).astype(BF16).T
        km_ref[:, sl] = (kv[:, sl] + kpe).astype(BF16)
    lane = lax.broadcasted_iota(jnp.int32, (1, nh), 1)
    ones_lane = jnp.where(lane % LANES == VDIM, 1.0, 0.0)
    vm_ref[...] = (kv[:, nh:] + ones_lane).astype(BF16).T


def _proj(h, pos_row, gm, w_in, gq, w_uq, gkv, w_ukv, tm=512):
    n = h.shape[0]
    row = lambda w: pl.BlockSpec((tm, w), lambda i: (i, 0))
    sds = lambda w: jax.ShapeDtypeStruct((n, w), BF16)
    col = pl.BlockSpec((MLA_HEADS * LANES, tm), lambda i: (0, i))
    sds_t = jax.ShapeDtypeStruct((MLA_HEADS * LANES, n), BF16)
    inv_col, pc, ps = _rope_constants()
    return pl.pallas_call(
        _proj_kernel,
        grid=(n // tm,),
        in_specs=[row(D_MODEL), pl.BlockSpec((1, tm), lambda i: (0, i)),
                  _const_spec((N_FREQ, 1)), _const_spec((N_FREQ, 2 * LANES)),
                  _const_spec((N_FREQ, 2 * LANES)), _const_spec((1, D_MODEL)),
                  _const_spec(w_in.shape), _const_spec((1, Q_LORA)), _const_spec(w_uq.shape),
                  _const_spec((1, KV_LORA)), _const_spec(w_ukv.shape)],
        out_specs=[col, row(1024), col, row(512), row(512), row(512), row(512)],
        out_shape=[sds_t, sds(1024), sds_t, sds(512), sds(512), sds(512), sds(512)],
        scratch_shapes=[pltpu.VMEM((D_MODEL, D_IN_AUG), BF16),
                        pltpu.VMEM((Q_LORA, 2 * MLA_HEADS * LANES), BF16),
                        pltpu.VMEM((KV_LORA, 2 * MLA_HEADS * LANES), BF16)],
        compiler_params=_params(("arbitrary",)),
        name="mixer_proj",
    )(h, pos_row, inv_col, pc, ps, gm, w_in, gq, w_uq, gkv, w_ukv)


V_ROWS = 80


def _mla_kernel(qt_ref, k0_ref, k1_ref, vt_ref, o_ref, m_ref, acc_ref, s0_ref, s1_ref, mb0_ref,
                mb1_ref, al0_ref, al1_ref, *, tq, tk, tqs, seq, unroll):
    nk = seq // tk
    slots = ((s0_ref, mb0_ref, al0_ref), (s1_ref, mb1_ref, al1_ref))
    m_ref[...] = jnp.full(m_ref.shape, -jnp.inf, F32)
    acc_ref[...] = jnp.zeros(acc_ref.shape, F32)

    def scores(c, slot, qs):
        s_ref, mb_ref, al_ref = slots[slot]
        off = c * tk if isinstance(c, int) else pl.multiple_of(c * tk, tk)
        for h in range(2):
            k = (k0_ref, k1_ref)[h][0, pl.ds(off, tk), :]
            s = jnp.dot(k, qt_ref[h * LANES:(h + 1) * LANES, qs], preferred_element_type=F32)
            s_ref[h, :, qs] = s
            m_old = m_ref[h, :, qs]
            m_new = jnp.maximum(m_old, jnp.max(s, axis=0, keepdims=True))
            m_ref[h, :, qs] = m_new
            mb_ref[h, :, qs] = m_new
            al_ref[h, :, qs] = jnp.exp2(m_old - m_new)

    def weighted_values(c, slot, qs):
        s_ref, mb_ref, al_ref = slots[slot]
        off = c * tk if isinstance(c, int) else pl.multiple_of(c * tk, tk)
        for h in range(2):
            p = jnp.exp2(s_ref[h, :, qs] - mb_ref[h, 0:1, qs])
            vt = vt_ref[h * LANES:h * LANES + V_ROWS, pl.ds(off, tk)]
            acc_ref[h, :, qs] = (al_ref[h, 0:1, qs] * acc_ref[h, :, qs]
                                 + jnp.dot(vt, p.astype(BF16), preferred_element_type=F32))

    q_slices = [slice(i * tqs, (i + 1) * tqs) for i in range(tq // tqs)]

    def chunk_group(c0, last):
        for j in range(unroll):
            for qs in q_slices:
                if not (last and j == unroll - 1):
                    scores(c0 + j + 1, (j + 1) % 2, qs)
                weighted_values(c0 + j, j % 2, qs)

    for qs in q_slices:
        scores(0, 0, qs)

    def body(i, carry):
        chunk_group(i * unroll, False)
        return carry

    lax.fori_loop(0, nk // unroll - 1, body, 0)
    chunk_group(nk - unroll, True)

    outs = []
    for h in range(2):
        acc = acc_ref[h]
        outs.append(acc[:VDIM, :] / acc[VDIM:VDIM + 1, :])
    o_ref[0] = jnp.concatenate(outs, axis=0).T.astype(BF16)


def _mla_attention(qt, k, vt, tq=2048, tk=512, tqs=256, unroll=2):
    b, s, _ = k.shape
    npair = MLA_HEADS // 2
    assert unroll % 2 == 0 and (s // tk) % unroll == 0
    nq = s // tq
    rowstate = pltpu.VMEM((2, 8, tq), F32)
    sbuf = pltpu.VMEM((2, tk, tq), F32)
    return pl.pallas_call(
        functools.partial(_mla_kernel, tq=tq, tk=tk, tqs=tqs, seq=s, unroll=unroll),
        grid=(b, npair, nq),
        in_specs=[pl.BlockSpec((2 * LANES, tq), lambda bi, p, i: (p, bi * nq + i)),
                  pl.BlockSpec((1, s, LANES), lambda bi, p, i: (bi, 0, 2 * p)),
                  pl.BlockSpec((1, s, LANES), lambda bi, p, i: (bi, 0, 2 * p + 1)),
                  pl.BlockSpec((2 * LANES, s), lambda bi, p, i: (p, bi))],
        out_specs=pl.BlockSpec((1, tq, LANES), lambda bi, p, i: (bi, i, p)),
        out_shape=jax.ShapeDtypeStruct((b, s, MLA_HEADS * VDIM), BF16),
        scratch_shapes=[rowstate, pltpu.VMEM((2, V_ROWS, tq), F32), sbuf, sbuf,
                        rowstate, rowstate, rowstate, rowstate],
        compiler_params=_params(("parallel", "parallel", "parallel")),
        name="mla_attention",
    )(qt, k, k, vt)


def _log_sigmoid(x):
    return jnp.minimum(x, 0.0) - jnp.log1p(jnp.exp(-jnp.abs(x)))


def _retention_kernel(q_ref, k_ref, v_ref, g_ref, dec_ref, o_ref, acc_lo_ref, acc_hi_ref,
                      stf_ref, stb_ref, dmf_ref, dmb_ref, tab_ref, ones_ref, *, ch, seq, unroll):
    nc = seq // ch
    half = nc // 2
    nh = QUAD // RET_DIM
    lane = lax.broadcasted_iota(jnp.int32, (ch, QUAD), 1)
    row = lax.broadcasted_iota(jnp.int32, (ch, QUAD), 0).astype(F32)
    khead = (lane >> 5) & (nh - 1)
    vhead = lane >> 6
    ri = lax.broadcasted_iota(jnp.int32, (ch, ch), 0)
    ci = lax.broadcasted_iota(jnp.int32, (ch, ch), 1)
    sd = lax.broadcasted_iota(jnp.int32, (QUAD, QUAD), 0)
    se = lax.broadcasted_iota(jnp.int32, (QUAD, QUAD), 1)
    state_mask = ((sd >> 5) & (nh - 1)) == (se >> 6)
    ones_ref[...] = jnp.where((sd >> 6) == (se >> 6), 1.0, 0.0).astype(BF16)

    lg = _log_sigmoid(dec_ref[...])
    tab_ref[0] = jnp.exp(lg[0:1] * (row + 1.0))
    tab_ref[1] = jnp.exp(lg[1:2] * (ch - 1.0 - row))
    tab_ref[2] = jnp.exp(lg[2:3] * (ch - row))
    tab_ref[3] = jnp.exp(lg[3:4] * row)
    c_dec = (jnp.exp(lg[0:1] * float(ch)), jnp.exp(lg[2:3] * float(ch)))
    rel_f = jnp.maximum(ri - ci, 0).astype(F32)
    rel_b = jnp.maximum(ci - ri, 0).astype(F32)
    for h in range(nh):
        dmf_ref[h] = jnp.where(ri >= ci, jnp.exp(lg[0:1, h * RET_DIM:h * RET_DIM + 1] * rel_f), 0.0)
        dmb_ref[h] = jnp.where(ci > ri, jnp.exp(lg[2:3, h * RET_DIM:h * RET_DIM + 1] * rel_b), 0.0)
    stf_ref[...] = jnp.zeros_like(stf_ref)
    stb_ref[...] = jnp.zeros_like(stb_ref)

    def chunks(items):
        loaded = []
        for c, _ in items:
            off = pl.multiple_of(c * ch, ch)
            loaded.append((q_ref[0, pl.ds(off, ch), :], k_ref[0, pl.ds(off, ch), :],
                           v_ref[0, pl.ds(off, ch), :]))
        scores = []
        for qc, kc, _ in loaded:
            zero = jnp.zeros_like(qc)
            scores.append([lax.dot_general(jnp.where(khead == h, qc, zero), kc,
                                           (((1,), (1,)), ((), ())), preferred_element_type=F32)
                           for h in range(nh)])
        outs = []
        for (_, fwd), (_, _, vc), ss in zip(items, loaded, scores):
            dm_ref = dmf_ref if fwd else dmb_ref
            out = None
            for h in range(nh):
                ih = jnp.dot((ss[h] * dm_ref[h]).astype(BF16), vc, preferred_element_type=F32)
                out = ih if out is None else jnp.where(vhead == h, ih, out)
            outs.append(out)
        upds = []
        for (_, fwd), (_, kc, vc) in zip(items, loaded):
            kd = (kc.astype(F32) * tab_ref[1 if fwd else 3]).astype(BF16)
            upds.append(lax.dot_general(kd, vc, (((0,), (0,)), ((), ())),
                                        preferred_element_type=F32))
        for i, ((_, fwd), (qc, _, _)) in enumerate(zip(items, loaded)):
            st_ref = stf_ref if fwd else stb_ref
            state = st_ref[...]
            cross = jnp.dot(qc, state.astype(BF16), preferred_element_type=F32)
            outs[i] = outs[i] + cross * tab_ref[0 if fwd else 2]
            st_ref[...] = state * c_dec[0 if fwd else 1] + jnp.where(state_mask, upds[i], 0.0)
        return outs

    def head_sum(x):
        hi = x.astype(BF16)
        lo = (x - hi.astype(F32)).astype(BF16)
        return (jnp.dot(hi, ones_ref[...], preferred_element_type=F32)
                + jnp.dot(lo, ones_ref[...], preferred_element_type=F32))

    def finish(c, tot):
        off = pl.multiple_of(c * ch, ch)
        d = tot - head_sum(tot) * (1.0 / RET_DIM)
        var = head_sum(d * d) * (1.0 / RET_DIM)
        gate = g_ref[0, pl.ds(off, ch), :].astype(F32)
        o_ref[0, pl.ds(off, ch), :] = (d * lax.rsqrt(var + EPS) * gate).astype(BF16)

    def rows(ref, c):
        return ref.at[pl.ds(pl.multiple_of(c * ch, ch), ch), :]

    def pairs(t):
        items = []
        for j in range(unroll):
            cf = t * unroll + j
            items += [(cf, True), (nc - 1 - cf, False)]
        return items

    def first(t, carry):
        items = pairs(t)
        for (c, fwd), out in zip(items, chunks(items)):
            if fwd:
                rows(acc_lo_ref, c)[...] = out
            else:
                rows(acc_hi_ref, c - half)[...] = out
        return carry

    lax.fori_loop(0, half // unroll, first, 0)

    def second(t, carry):
        items = pairs(t)
        for (c, fwd), out in zip(items, chunks(items)):
            other = rows(acc_hi_ref, c - half) if fwd else rows(acc_lo_ref, c)
            finish(c, out + other[...])
        return carry

    lax.fori_loop(half // unroll, nc // unroll, second, 0)


def _retention(q, k, v, g, dec, ch=256, unroll=2):
    b, s, _ = q.shape
    nquad = D_RET // QUAD
    assert (s // ch // 2) % unroll == 0
    blk = pl.BlockSpec((1, s, QUAD), lambda bi, u: (bi, 0, u))
    return pl.pallas_call(
        functools.partial(_retention_kernel, ch=ch, seq=s, unroll=unroll),
        grid=(b, nquad),
        in_specs=[blk, blk, blk, blk, pl.BlockSpec((4, QUAD), lambda bi, u: (0, u))],
        out_specs=blk,
        out_shape=jax.ShapeDtypeStruct((b, s, D_RET), BF16),
        scratch_shapes=[pltpu.VMEM((s // 2, QUAD), F32), pltpu.VMEM((s // 2, QUAD), F32),
                        pltpu.VMEM((QUAD, QUAD), F32), pltpu.VMEM((QUAD, QUAD), F32),
                        pltpu.VMEM((QUAD // RET_DIM, ch, ch), F32),
                        pltpu.VMEM((QUAD // RET_DIM, ch, ch), F32),
                        pltpu.VMEM((4, ch, QUAD), F32), pltpu.VMEM((QUAD, QUAD), BF16)],
        compiler_params=_params(("parallel", "parallel")),
        name="retention",
    )(q, k, v, g, dec)


def _prep_decay(dec_f, dec_b):
    nquad, per_quad = D_RET // QUAD, QUAD // RET_DIM
    rows = []
    for d in (dec_f.astype(F32), dec_b.astype(F32)):
        rows.append(jnp.broadcast_to(d[:, None], (RET_HEADS, RET_DIM)).reshape(D_RET))
        rows.append(jnp.broadcast_to(d.reshape(nquad, 1, per_quad, 1),
                                     (nquad, 2, per_quad, 32)).reshape(D_RET))
    return jnp.stack(rows)


def kernel(x, positions, ffn1_norm, ffn1_w_gate, ffn1_w_up, ffn1_w_down, mix_norm, w_in, q_norm, w_uq, kv_norm, w_ukv, ret_decay_fwd, ret_decay_bwd, w_o, ffn2_norm, ffn2_w_gate, ffn2_w_up, ffn2_w_down, final_norm):
    b, s, d = x.shape
    n = b * s
    assert ffn1_norm.shape[0] == 1, "specialised to DEPTH == 1 (the final norm is fused into the layer)"
    pos_row = positions.astype(F32).reshape(1, n)

    h = _ffn1(x.reshape(n, d), ffn1_norm[0][None, :], ffn1_w_gate[0], ffn1_w_up[0],
              ffn1_w_down[0])
    qm, km, vm, rq, rk, rv, rg = _proj(
        h, pos_row, mix_norm[0][None, :], w_in[0], q_norm[0][None, :],
        w_uq[0], kv_norm[0][None, :], w_ukv[0])
    a = _mla_attention(qm, km.reshape(b, s, -1), vm)
    r = _retention(rq.reshape(b, s, -1), rk.reshape(b, s, -1), rv.reshape(b, s, -1),
                   rg.reshape(b, s, -1), _prep_decay(ret_decay_fwd[0], ret_decay_bwd[0]))
    wo = w_o[0].astype(BF16)
    out = _out_ffn2(h, a.reshape(n, -1), r.reshape(n, -1), wo, ffn2_norm[0][None, :],
                    ffn2_w_gate[0], ffn2_w_up[0], ffn2_w_down[0], final_norm[None, :])
    return out.reshape(b, s, d)
```

```python
import functools
import math

import numpy as np
import jax
import jax.numpy as jnp
from jax import lax
from jax.experimental import pallas as pl
from jax.experimental.pallas import tpu as pltpu

F32 = jnp.float32
BF16 = jnp.bfloat16

D_MODEL = 1024
D_FF = 2816
MLA_HEADS = 8
Q_LORA = 256
KV_LORA = 128
NOPE = 64
ROPE = 32
VDIM = 64
RET_HEADS = 8
RET_DIM = 64
D_RET = RET_HEADS * RET_DIM
ROPE_THETA = 10000.0
EPS = 1e-6
MLA_SCALE = (NOPE + ROPE) ** -0.5
LOG2E = math.log2(math.e)

LANES = 128
QUAD = 256
VMEM_LIMIT = 56 * 1024 * 1024

_OFF_CQ = 0
_OFF_CKV = 256
_OFF_RQ = 768
_OFF_RK = 1280
_OFF_RV = 1792
_OFF_RG = 2304
D_IN_AUG = 2816


def _params(sem):
    return pltpu.CompilerParams(dimension_semantics=sem, vmem_limit_bytes=VMEM_LIMIT)


def _const_spec(shape):
    nd = len(shape)
    return pl.BlockSpec(shape, lambda *_: (0,) * nd, pipeline_mode=pl.Buffered(1))


N_FREQ = 48


def _rope_constants():
    row = np.arange(N_FREQ)
    inv = np.where(row < 32, ROPE_THETA ** (-(row % 32) / 32.0),
                   np.where(row < 48, ROPE_THETA ** (-((row - 32) % 16) / 16.0), 0.0))
    pc = np.zeros((N_FREQ, 2 * LANES))
    ps = np.zeros((N_FREQ, 2 * LANES))
    for l in range(LANES):
        pc[l % 32, l] = 1.0
        ps[l % 32, l] = 1.0
    for l in range(NOPE, NOPE + ROPE):
        r = 32 + (l - NOPE) % 16
        pc[r, LANES + l] = 1.0
        ps[r, LANES + l] = -1.0 if l < NOPE + 16 else 1.0
    return jnp.asarray(inv[:, None], F32), jnp.asarray(pc, BF16), jnp.asarray(ps, BF16)


def _rope_tables(pos_row, inv_col, pc, ps):
    ang = inv_col * pos_row
    dn = (((0,), (0,)), ((), ()))

    def place(x, p):
        hi = x.astype(BF16)
        lo = (x - hi.astype(F32)).astype(BF16)
        return (lax.dot_general(hi, p, dn, preferred_element_type=F32)
                + lax.dot_general(lo, p, dn, preferred_element_type=F32))

    return place(jnp.cos(ang), pc), place(jnp.sin(ang), ps)


def _rms(x, g):
    return x * lax.rsqrt(jnp.mean(x * x, axis=-1, keepdims=True) + EPS) * g


def _swiglu_acc(n_bf, wg_ref, wu_ref, wd_ref, ck):
    acc = None
    for j in range(D_FF // ck):
        sl = slice(j * ck, (j + 1) * ck)
        g = jnp.dot(n_bf, wg_ref[:, sl].astype(BF16), preferred_element_type=F32)
        u = jnp.dot(n_bf, wu_ref[:, sl].astype(BF16), preferred_element_type=F32)
        hm = (g * jax.nn.sigmoid(g) * u).astype(BF16)
        d = jnp.dot(hm, wd_ref[sl, :].astype(BF16), preferred_element_type=F32)
        acc = d if acc is None else acc + d
    return acc


def _ffn1_kernel(x_ref, g_ref, wg_ref, wu_ref, wd_ref, o_ref, *, ck):
    x = x_ref[...]
    n_bf = _rms(x, g_ref[...]).astype(BF16)
    o_ref[...] = x + 0.5 * _swiglu_acc(n_bf, wg_ref, wu_ref, wd_ref, ck)


def _ffn1(x, g, wg, wu, wd, tm=512, ck=256):
    n = x.shape[0]
    row = pl.BlockSpec((tm, D_MODEL), lambda i: (i, 0))
    return pl.pallas_call(
        functools.partial(_ffn1_kernel, ck=ck),
        grid=(n // tm,),
        in_specs=[row, _const_spec((1, D_MODEL)), _const_spec((D_MODEL, D_FF)),
                  _const_spec((D_MODEL, D_FF)), _const_spec((D_FF, D_MODEL))],
        out_specs=row,
        out_shape=jax.ShapeDtypeStruct((n, D_MODEL), F32),
        compiler_params=_params(("parallel",)),
        name="ffn1",
    )(x, g, wg, wu, wd)


def _out_ffn2_kernel(h_ref, a_ref, r_ref, wo_ref, g_ref, wg_ref, wu_ref, wd_ref, gf_ref, o_ref,
                     *, ck):
    h = (h_ref[...]
         + jnp.dot(a_ref[...], wo_ref[:MLA_HEADS * VDIM, :], preferred_element_type=F32)
         + jnp.dot(r_ref[...], wo_ref[MLA_HEADS * VDIM:, :], preferred_element_type=F32))
    n_bf = _rms(h, g_ref[...]).astype(BF16)
    h = h + 0.5 * _swiglu_acc(n_bf, wg_ref, wu_ref, wd_ref, ck)
    o_ref[...] = _rms(h, gf_ref[...])


def _out_ffn2(h, a, r, wo, g, wg, wu, wd, gf, tm=512, ck=256):
    n = h.shape[0]
    row = pl.BlockSpec((tm, D_MODEL), lambda i: (i, 0))
    half = pl.BlockSpec((tm, D_RET), lambda i: (i, 0))
    return pl.pallas_call(
        functools.partial(_out_ffn2_kernel, ck=ck),
        grid=(n // tm,),
        in_specs=[row, half, half, _const_spec((MLA_HEADS * VDIM + D_RET, D_MODEL)),
                  _const_spec((1, D_MODEL)), _const_spec((D_MODEL, D_FF)),
                  _const_spec((D_MODEL, D_FF)), _const_spec((D_FF, D_MODEL)),
                  _const_spec((1, D_MODEL))],
        out_specs=row,
        out_shape=jax.ShapeDtypeStruct((n, D_MODEL), F32),
        compiler_params=_params(("parallel",)),
        name="out_ffn2",
    )(h, a, r, wo, g, wg, wu, wd, gf)


def _stage_w_in(w_ref, out_ref, rows=128):
    o_kpe = Q_LORA + KV_LORA
    o_rq = o_kpe + ROPE
    for r0 in range(0, D_MODEL, rows):
        x = w_ref[r0:r0 + rows, :]

        def put(off, val):
            out_ref[r0:r0 + rows, off:off + val.shape[1]] = val.astype(BF16)

        zeros = lambda w: jnp.zeros((rows, w), F32)
        put(_OFF_CQ, x[:, :o_kpe])
        kpe = x[:, o_kpe:o_rq]
        put(_OFF_CKV + LANES, jnp.concatenate([zeros(NOPE), kpe, zeros(32)], axis=1))
        put(_OFF_CKV + 2 * LANES,
            jnp.concatenate([zeros(NOPE), kpe[:, 16:], kpe[:, :16], zeros(32)], axis=1))
        put(_OFF_CKV + 3 * LANES, zeros(LANES))
        for off_out, off_in, scale in ((_OFF_RQ, o_rq, 1.0), (_OFF_RK, o_rq + D_RET, RET_DIM ** -0.5)):
            for u in range(D_RET // QUAD):
                for part in range(2):
                    src = [off_in + (4 * u + j) * RET_DIM + part * 32 for j in range(4)]
                    blk = jnp.concatenate([x[:, s:s + 32] for s in src], axis=1)
                    put(off_out + u * QUAD + part * LANES, blk * scale if scale != 1.0 else blk)
        put(_OFF_RV, x[:, o_rq + 2 * D_RET:o_rq + 4 * D_RET])


def _stage_w_uq(w_ref, out_ref):
    x = w_ref[...]
    rows = x.shape[0]
    zeros = lambda w: jnp.zeros((rows, w), F32)
    nh = MLA_HEADS * LANES
    for h in range(MLA_HEADS):
        head = x[:, h * (NOPE + ROPE):(h + 1) * (NOPE + ROPE)]
        out_ref[:, h * LANES:(h + 1) * LANES] = jnp.concatenate(
            [head, zeros(32)], axis=1).astype(BF16)
        out_ref[:, nh + h * LANES:nh + (h + 1) * LANES] = jnp.concatenate(
            [zeros(NOPE), head[:, NOPE + 16:], head[:, NOPE:NOPE + 16], zeros(32)],
            axis=1).astype(BF16)


def _stage_w_ukv(w_ref, out_ref):
    x = w_ref[...]
    zeros = jnp.zeros((x.shape[0], LANES - NOPE), F32)
    nh = MLA_HEADS * LANES
    for h in range(MLA_HEADS):
        src = h * (NOPE + VDIM)
        out_ref[:, h * LANES:(h + 1) * LANES] = jnp.concatenate(
            [x[:, src:src + NOPE], zeros], axis=1).astype(BF16)
        out_ref[:, nh + h * LANES:nh + (h + 1) * LANES] = jnp.concatenate(
            [x[:, src + NOPE:src + NOPE + VDIM], zeros], axis=1).astype(BF16)


def _proj_kernel(h_ref, pos_ref, inv_ref, pc_ref, ps_ref, gm_ref, w_in_ref, gq_ref, w_uq_ref,
                 gkv_ref, w_ukv_ref, qm_ref, km_ref, vm_ref, rq_ref, rk_ref, rv_ref, rg_ref,
                 win_ref, wq_ref, wkv_ref):
    @pl.when(pl.program_id(0) == 0)
    def _():
        _stage_w_in(w_in_ref, win_ref)
        _stage_w_uq(w_uq_ref, wq_ref)
        _stage_w_ukv(w_ukv_ref, wkv_ref)

    n_bf = _rms(h_ref[...], gm_ref[...]).astype(BF16)

    def proj(off, width):
        return jnp.dot(n_bf, win_ref[:, off:off + width], preferred_element_type=F32)

    def ret_rope(x, out_ref):
        for u in range(D_RET // QUAD):
            x1 = x[:, u * QUAD:u * QUAD + LANES]
            x2 = x[:, u * QUAD + LANES:(u + 1) * QUAD]
            out_ref[:, u * QUAD:u * QUAD + LANES] = (x1 * cr - x2 * sr).astype(BF16)
            out_ref[:, u * QUAD + LANES:(u + 1) * QUAD] = (x1 * sr + x2 * cr).astype(BF16)

    cq_bf = _rms(proj(_OFF_CQ, Q_LORA), gq_ref[...]).astype(BF16)
    lat = proj(_OFF_CKV, 4 * LANES)
    ckv_bf = _rms(lat[:, :KV_LORA], gkv_ref[...]).astype(BF16)
    xq = proj(_OFF_RQ, D_RET)

    cos_t, sin_t = _rope_tables(pos_ref[...], inv_ref[...], pc_ref[...], ps_ref[...])
    cr, sr = cos_t[:, :LANES], sin_t[:, :LANES]
    nope_lane = lax.broadcasted_iota(jnp.int32, (1, LANES), 1) < NOPE
    cm, sm = cos_t[:, LANES:] + jnp.where(nope_lane, 1.0, 0.0), sin_t[:, LANES:]

    xk = proj(_OFF_RK, D_RET)
    ret_rope(xq, rq_ref)
    rv_ref[...] = proj(_OFF_RV, D_RET).astype(BF16)
    ret_rope(xk, rk_ref)
    g = proj(_OFF_RG, D_RET)
    rg_ref[...] = (g * jax.nn.sigmoid(g)).astype(BF16)

    qq = jnp.dot(cq_bf, wq_ref[...], preferred_element_type=F32)
    kv = jnp.dot(ckv_bf, wkv_ref[...], preferred_element_type=F32)
    nh = MLA_HEADS * LANES
    kpe = lat[:, LANES:2 * LANES] * cm + lat[:, 2 * LANES:3 * LANES] * sm
    cq_s, sq_s = cm * (MLA_SCALE * LOG2E), sm * (MLA_SCALE * LOG2E)
    for h in range(MLA_HEADS):
        sl = slice(h * LANES, (h + 1) * LANES)
        sw = slice(nh + h * LANES, nh + (h + 1) * LANES)
        qm_ref[sl, :] = (qq[:, sl] * cq_s + qq[:, sw] * sq_s).astype(BF16).T
        km_ref[:, sl] = (kv[:, sl] + kpe).astype(BF16)
    lane = lax.broadcasted_iota(jnp.int32, (1, nh), 1)
    ones_lane = jnp.where(lane % LANES == VDIM, 1.0, 0.0)
    vm_ref[...] = (kv[:, nh:] + ones_lane).astype(BF16).T


def _proj(h, pos_row, gm, w_in, gq, w_uq, gkv, w_ukv, tm=1024):
    n = h.shape[0]
    row = lambda w: pl.BlockSpec((tm, w), lambda i: (i, 0))
    sds = lambda w: jax.ShapeDtypeStruct((n, w), BF16)
    col = pl.BlockSpec((MLA_HEADS * LANES, tm), lambda i: (0, i))
    sds_t = jax.ShapeDtypeStruct((MLA_HEADS * LANES, n), BF16)
    inv_col, pc, ps = _rope_constants()
    return pl.pallas_call(
        _proj_kernel,
        grid=(n // tm,),
        in_specs=[row(D_MODEL), pl.BlockSpec((1, tm), lambda i: (0, i)),
                  _const_spec((N_FREQ, 1)), _const_spec((N_FREQ, 2 * LANES)),
                  _const_spec((N_FREQ, 2 * LANES)), _const_spec((1, D_MODEL)),
                  _const_spec(w_in.shape), _const_spec((1, Q_LORA)), _const_spec(w_uq.shape),
                  _const_spec((1, KV_LORA)), _const_spec(w_ukv.shape)],
        out_specs=[col, row(1024), col, row(512), row(512), row(512), row(512)],
        out_shape=[sds_t, sds(1024), sds_t, sds(512), sds(512), sds(512), sds(512)],
        scratch_shapes=[pltpu.VMEM((D_MODEL, D_IN_AUG), BF16),
                        pltpu.VMEM((Q_LORA, 2 * MLA_HEADS * LANES), BF16),
                        pltpu.VMEM((KV_LORA, 2 * MLA_HEADS * LANES), BF16)],
        compiler_params=_params(("arbitrary",)),
        name="mixer_proj",
    )(h, pos_row, inv_col, pc, ps, gm, w_in, gq, w_uq, gkv, w_ukv)


V_ROWS = 80


def _mla_kernel(qt_ref, k0_ref, k1_ref, vt_ref, o_ref, m_ref, acc_ref, s0_ref, s1_ref, mb0_ref,
                mb1_ref, al0_ref, al1_ref, *, tq, tk, tqs, seq, unroll):
    nk = seq // tk
    slots = ((s0_ref, mb0_ref, al0_ref), (s1_ref, mb1_ref, al1_ref))
    m_ref[...] = jnp.full(m_ref.shape, -jnp.inf, F32)
    acc_ref[...] = jnp.zeros(acc_ref.shape, F32)

    def scores(c, slot, qs):
        s_ref, mb_ref, al_ref = slots[slot]
        off = c * tk if isinstance(c, int) else pl.multiple_of(c * tk, tk)
        for h in range(2):
            k = (k0_ref, k1_ref)[h][0, pl.ds(off, tk), :]
            s = jnp.dot(k, qt_ref[h * LANES:(h + 1) * LANES, qs], preferred_element_type=F32)
            s_ref[h, :, qs] = s
            m_old = m_ref[h, :, qs]
            m_new = jnp.maximum(m_old, jnp.max(s, axis=0, keepdims=True))
            m_ref[h, :, qs] = m_new
            mb_ref[h, :, qs] = m_new
            al_ref[h, :, qs] = jnp.exp2(m_old - m_new)

    def weighted_values(c, slot, qs):
        s_ref, mb_ref, al_ref = slots[slot]
        off = c * tk if isinstance(c, int) else pl.multiple_of(c * tk, tk)
        for h in range(2):
            p = jnp.exp2(s_ref[h, :, qs] - mb_ref[h, 0:1, qs])
            vt = vt_ref[h * LANES:h * LANES + V_ROWS, pl.ds(off, tk)]
            acc_ref[h, :, qs] = (al_ref[h, 0:1, qs] * acc_ref[h, :, qs]
                                 + jnp.dot(vt, p.astype(BF16), preferred_element_type=F32))

    q_slices = [slice(i * tqs, (i + 1) * tqs) for i in range(tq // tqs)]

    def chunk_group(c0, last):
        for j in range(unroll):
            for qs in q_slices:
                if not (last and j == unroll - 1):
                    scores(c0 + j + 1, (j + 1) % 2, qs)
                weighted_values(c0 + j, j % 2, qs)

    for qs in q_slices:
        scores(0, 0, qs)

    def body(i, carry):
        chunk_group(i * unroll, False)
        return carry

    lax.fori_loop(0, nk // unroll - 1, body, 0)
    chunk_group(nk - unroll, True)

    outs = []
    for h in range(2):
        acc = acc_ref[h]
        outs.append(acc[:VDIM, :] / acc[VDIM:VDIM + 1, :])
    o_ref[0] = jnp.concatenate(outs, axis=0).T.astype(BF16)


def _mla_attention(qt, k, vt, tq=2048, tk=512, tqs=256, unroll=2):
    b, s, _ = k.shape
    npair = MLA_HEADS // 2
    assert unroll % 2 == 0 and (s // tk) % unroll == 0
    nq = s // tq
    rowstate = pltpu.VMEM((2, 8, tq), F32)
    sbuf = pltpu.VMEM((2, tk, tq), F32)
    return pl.pallas_call(
        functools.partial(_mla_kernel, tq=tq, tk=tk, tqs=tqs, seq=s, unroll=unroll),
        grid=(b, npair, nq),
        in_specs=[pl.BlockSpec((2 * LANES, tq), lambda bi, p, i: (p, bi * nq + i)),
                  pl.BlockSpec((1, s, LANES), lambda bi, p, i: (bi, 0, 2 * p)),
                  pl.BlockSpec((1, s, LANES), lambda bi, p, i: (bi, 0, 2 * p + 1)),
                  pl.BlockSpec((2 * LANES, s), lambda bi, p, i: (p, bi))],
        out_specs=pl.BlockSpec((1, tq, LANES), lambda bi, p, i: (bi, i, p)),
        out_shape=jax.ShapeDtypeStruct((b, s, MLA_HEADS * VDIM), BF16),
        scratch_shapes=[rowstate, pltpu.VMEM((2, V_ROWS, tq), F32), sbuf, sbuf,
                        rowstate, rowstate, rowstate, rowstate],
        compiler_params=_params(("parallel", "parallel", "parallel")),
        name="mla_attention",
    )(qt, k, k, vt)


def _log_sigmoid(x):
    return jnp.minimum(x, 0.0) - jnp.log1p(jnp.exp(-jnp.abs(x)))


def _retention_kernel(q_ref, k_ref, v_ref, g_ref, dec_ref, o_ref, acc_lo_ref, acc_hi_ref,
                      stf_ref, stb_ref, dmf_ref, dmb_ref, tab_ref, ones_ref, *, ch, seq, unroll):
    nc = seq // ch
    half = nc // 2
    nh = QUAD // RET_DIM
    lane = lax.broadcasted_iota(jnp.int32, (ch, QUAD), 1)
    row = lax.broadcasted_iota(jnp.int32, (ch, QUAD), 0).astype(F32)
    khead = (lane >> 5) & (nh - 1)
    vhead = lane >> 6
    ri = lax.broadcasted_iota(jnp.int32, (ch, ch), 0)
    ci = lax.broadcasted_iota(jnp.int32, (ch, ch), 1)
    sd = lax.broadcasted_iota(jnp.int32, (QUAD, QUAD), 0)
    se = lax.broadcasted_iota(jnp.int32, (QUAD, QUAD), 1)
    state_mask = ((sd >> 5) & (nh - 1)) == (se >> 6)
    ones_ref[...] = jnp.where((sd >> 6) == (se >> 6), 1.0, 0.0).astype(BF16)

    lg = _log_sigmoid(dec_ref[...])
    tab_ref[0] = jnp.exp(lg[0:1] * (row + 1.0))
    tab_ref[1] = jnp.exp(lg[1:2] * (ch - 1.0 - row))
    tab_ref[2] = jnp.exp(lg[2:3] * (ch - row))
    tab_ref[3] = jnp.exp(lg[3:4] * row)
    c_dec = (jnp.exp(lg[0:1] * float(ch)), jnp.exp(lg[2:3] * float(ch)))
    rel_f = jnp.maximum(ri - ci, 0).astype(F32)
    rel_b = jnp.maximum(ci - ri, 0).astype(F32)
    for h in range(nh):
        dmf_ref[h] = jnp.where(ri >= ci, jnp.exp(lg[0:1, h * RET_DIM:h * RET_DIM + 1] * rel_f), 0.0)
        dmb_ref[h] = jnp.where(ci > ri, jnp.exp(lg[2:3, h * RET_DIM:h * RET_DIM + 1] * rel_b), 0.0)
    stf_ref[...] = jnp.zeros_like(stf_ref)
    stb_ref[...] = jnp.zeros_like(stb_ref)

    def chunks(items):
        loaded = []
        for c, _ in items:
            off = pl.multiple_of(c * ch, ch)
            loaded.append((q_ref[0, pl.ds(off, ch), :], k_ref[0, pl.ds(off, ch), :],
                           v_ref[0, pl.ds(off, ch), :]))
        scores = []
        for qc, kc, _ in loaded:
            zero = jnp.zeros_like(qc)
            scores.append([lax.dot_general(jnp.where(khead == h, qc, zero), kc,
                                           (((1,), (1,)), ((), ())), preferred_element_type=F32)
                           for h in range(nh)])
        outs = []
        for (_, fwd), (_, _, vc), ss in zip(items, loaded, scores):
            dm_ref = dmf_ref if fwd else dmb_ref
            out = None
            for h in range(nh):
                ih = jnp.dot((ss[h] * dm_ref[h]).astype(BF16), vc, preferred_element_type=F32)
                out = ih if out is None else jnp.where(vhead == h, ih, out)
            outs.append(out)
        upds = []
        for (_, fwd), (_, kc, vc) in zip(items, loaded):
            kd = (kc.astype(F32) * tab_ref[1 if fwd else 3]).astype(BF16)
            upds.append(lax.dot_general(kd, vc, (((0,), (0,)), ((), ())),
                                        preferred_element_type=F32))
        for i, ((_, fwd), (qc, _, _)) in enumerate(zip(items, loaded)):
            st_ref = stf_ref if fwd else stb_ref
            state = st_ref[...]
            cross = jnp.dot(qc, state.astype(BF16), preferred_element_type=F32)
            outs[i] = outs[i] + cross * tab_ref[0 if fwd else 2]
            st_ref[...] = state * c_dec[0 if fwd else 1] + jnp.where(state_mask, upds[i], 0.0)
        return outs

    def head_sum(x):
        hi = x.astype(BF16)
        lo = (x - hi.astype(F32)).astype(BF16)
        return (jnp.dot(hi, ones_ref[...], preferred_element_type=F32)
                + jnp.dot(lo, ones_ref[...], preferred_element_type=F32))

    def finish(c, tot):
        off = pl.multiple_of(c * ch, ch)
        d = tot - head_sum(tot) * (1.0 / RET_DIM)
        var = head_sum(d * d) * (1.0 / RET_DIM)
        gate = g_ref[0, pl.ds(off, ch), :].astype(F32)
        o_ref[0, pl.ds(off, ch), :] = (d * lax.rsqrt(var + EPS) * gate).astype(BF16)

    def rows(ref, c):
        return ref.at[pl.ds(pl.multiple_of(c * ch, ch), ch), :]

    def pairs(t):
        items = []
        for j in range(unroll):
            cf = t * unroll + j
            items += [(cf, True), (nc - 1 - cf, False)]
        return items

    def first(t, carry):
        items = pairs(t)
        for (c, fwd), out in zip(items, chunks(items)):
            if fwd:
                rows(acc_lo_ref, c)[...] = out
            else:
                rows(acc_hi_ref, c - half)[...] = out
        return carry

    lax.fori_loop(0, half // unroll, first, 0)

    def second(t, carry):
        items = pairs(t)
        for (c, fwd), out in zip(items, chunks(items)):
            other = rows(acc_hi_ref, c - half) if fwd else rows(acc_lo_ref, c)
            finish(c, out + other[...])
        return carry

    lax.fori_loop(half // unroll, nc // unroll, second, 0)


def _retention(q, k, v, g, dec, ch=256, unroll=2):
    b, s, _ = q.shape
    nquad = D_RET // QUAD
    assert (s // ch // 2) % unroll == 0
    blk = pl.BlockSpec((1, s, QUAD), lambda bi, u: (bi, 0, u))
    return pl.pallas_call(
        functools.partial(_retention_kernel, ch=ch, seq=s, unroll=unroll),
        grid=(b, nquad),
        in_specs=[blk, blk, blk, blk, pl.BlockSpec((4, QUAD), lambda bi, u: (0, u))],
        out_specs=blk,
        out_shape=jax.ShapeDtypeStruct((b, s, D_RET), BF16),
        scratch_shapes=[pltpu.VMEM((s // 2, QUAD), F32), pltpu.VMEM((s // 2, QUAD), F32),
                        pltpu.VMEM((QUAD, QUAD), F32), pltpu.VMEM((QUAD, QUAD), F32),
                        pltpu.VMEM((QUAD // RET_DIM, ch, ch), F32),
                        pltpu.VMEM((QUAD // RET_DIM, ch, ch), F32),
                        pltpu.VMEM((4, ch, QUAD), F32), pltpu.VMEM((QUAD, QUAD), BF16)],
        compiler_params=_params(("parallel", "parallel")),
        name="retention",
    )(q, k, v, g, dec)


def _prep_decay(dec_f, dec_b):
    nquad, per_quad = D_RET // QUAD, QUAD // RET_DIM
    rows = []
    for d in (dec_f.astype(F32), dec_b.astype(F32)):
        rows.append(jnp.broadcast_to(d[:, None], (RET_HEADS, RET_DIM)).reshape(D_RET))
        rows.append(jnp.broadcast_to(d.reshape(nquad, 1, per_quad, 1),
                                     (nquad, 2, per_quad, 32)).reshape(D_RET))
    return jnp.stack(rows)


def kernel(x, positions, ffn1_norm, ffn1_w_gate, ffn1_w_up, ffn1_w_down, mix_norm, w_in, q_norm, w_uq, kv_norm, w_ukv, ret_decay_fwd, ret_decay_bwd, w_o, ffn2_norm, ffn2_w_gate, ffn2_w_up, ffn2_w_down, final_norm):
    b, s, d = x.shape
    n = b * s
    assert ffn1_norm.shape[0] == 1, "specialised to DEPTH == 1 (the final norm is fused into the layer)"
    pos_row = positions.astype(F32).reshape(1, n)

    h = _ffn1(x.reshape(n, d), ffn1_norm[0][None, :], ffn1_w_gate[0], ffn1_w_up[0],
              ffn1_w_down[0])
    qm, km, vm, rq, rk, rv, rg = _proj(
        h, pos_row, mix_norm[0][None, :], w_in[0], q_norm[0][None, :],
        w_uq[0], kv_norm[0][None, :], w_ukv[0])
    a = _mla_attention(qm, km.reshape(b, s, -1), vm)
    r = _retention(rq.reshape(b, s, -1), rk.reshape(b, s, -1), rv.reshape(b, s, -1),
                   rg.reshape(b, s, -1), _prep_decay(ret_decay_fwd[0], ret_decay_bwd[0]))
    wo = w_o[0].astype(BF16)
    out = _out_ffn2(h, a.reshape(n, -1), r.reshape(n, -1), wo, ffn2_norm[0][None, :],
                    ffn2_w_gate[0], ffn2_w_up[0], ffn2_w_down[0], final_norm[None, :])
    return out.reshape(b, s, d)
```

```python
import functools
import math

import numpy as np
import jax
import jax.numpy as jnp
from jax import lax
from jax.experimental import pallas as pl
from jax.experimental.pallas import tpu as pltpu

F32 = jnp.float32
BF16 = jnp.bfloat16

D_MODEL = 1024
D_FF = 2816
MLA_HEADS = 8
Q_LORA = 256
KV_LORA = 128
NOPE = 64
ROPE = 32
VDIM = 64
RET_HEADS = 8
RET_DIM = 64
D_RET = RET_HEADS * RET_DIM
ROPE_THETA = 10000.0
EPS = 1e-6
MLA_SCALE = (NOPE + ROPE) ** -0.5
LOG2E = math.log2(math.e)

LANES = 128
QUAD = 256
VMEM_LIMIT = 56 * 1024 * 1024

_OFF_CQ = 0
_OFF_CKV = 256
_OFF_RQ = 768
_OFF_RK = 1280
_OFF_RV = 1792
_OFF_RG = 2304
D_IN_AUG = 2816


def _params(sem):
    return pltpu.CompilerParams(dimension_semantics=sem, vmem_limit_bytes=VMEM_LIMIT)


def _const_spec(shape):
    nd = len(shape)
    return pl.BlockSpec(shape, lambda *_: (0,) * nd, pipeline_mode=pl.Buffered(1))


N_FREQ = 48


def _rope_constants():
    row = np.arange(N_FREQ)
    inv = np.where(row < 32, ROPE_THETA ** (-(row % 32) / 32.0),
                   np.where(row < 48, ROPE_THETA ** (-((row - 32) % 16) / 16.0), 0.0))
    pc = np.zeros((N_FREQ, 2 * LANES))
    ps = np.zeros((N_FREQ, 2 * LANES))
    for l in range(LANES):
        pc[l % 32, l] = 1.0
        ps[l % 32, l] = 1.0
    for l in range(NOPE, NOPE + ROPE):
        r = 32 + (l - NOPE) % 16
        pc[r, LANES + l] = 1.0
        ps[r, LANES + l] = -1.0 if l < NOPE + 16 else 1.0
    return jnp.asarray(inv[:, None], F32), jnp.asarray(pc, BF16), jnp.asarray(ps, BF16)


def _rope_tables(pos_row, inv_col, pc, ps):
    ang = inv_col * pos_row
    dn = (((0,), (0,)), ((), ()))

    def place(x, p):
        hi = x.astype(BF16)
        lo = (x - hi.astype(F32)).astype(BF16)
        return (lax.dot_general(hi, p, dn, preferred_element_type=F32)
                + lax.dot_general(lo, p, dn, preferred_element_type=F32))

    return place(jnp.cos(ang), pc), place(jnp.sin(ang), ps)


def _rms(x, g):
    return x * lax.rsqrt(jnp.mean(x * x, axis=-1, keepdims=True) + EPS) * g


def _swiglu_acc(n_bf, wg_ref, wu_ref, wd_ref, ck):
    acc = None
    for j in range(D_FF // ck):
        sl = slice(j * ck, (j + 1) * ck)
        g = jnp.dot(n_bf, wg_ref[:, sl].astype(BF16), preferred_element_type=F32)
        u = jnp.dot(n_bf, wu_ref[:, sl].astype(BF16), preferred_element_type=F32)
        hm = (g * jax.nn.sigmoid(g) * u).astype(BF16)
        d = jnp.dot(hm, wd_ref[sl, :].astype(BF16), preferred_element_type=F32)
        acc = d if acc is None else acc + d
    return acc


def _ffn1_kernel(x_ref, g_ref, wg_ref, wu_ref, wd_ref, o_ref, *, ck):
    x = x_ref[...]
    n_bf = _rms(x, g_ref[...]).astype(BF16)
    o_ref[...] = x + 0.5 * _swiglu_acc(n_bf, wg_ref, wu_ref, wd_ref, ck)


def _ffn1(x, g, wg, wu, wd, tm=512, ck=256):
    n = x.shape[0]
    row = pl.BlockSpec((tm, D_MODEL), lambda i: (i, 0))
    return pl.pallas_call(
        functools.partial(_ffn1_kernel, ck=ck),
        grid=(n // tm,),
        in_specs=[row, _const_spec((1, D_MODEL)), _const_spec((D_MODEL, D_FF)),
                  _const_spec((D_MODEL, D_FF)), _const_spec((D_FF, D_MODEL))],
        out_specs=row,
        out_shape=jax.ShapeDtypeStruct((n, D_MODEL), F32),
        compiler_params=_params(("parallel",)),
        name="ffn1",
    )(x, g, wg, wu, wd)


def _out_ffn2_kernel(h_ref, a_ref, r_ref, wo_ref, g_ref, wg_ref, wu_ref, wd_ref, gf_ref, o_ref,
                     *, ck):
    h = (h_ref[...]
         + jnp.dot(a_ref[...], wo_ref[:MLA_HEADS * VDIM, :], preferred_element_type=F32)
         + jnp.dot(r_ref[...], wo_ref[MLA_HEADS * VDIM:, :], preferred_element_type=F32))
    n_bf = _rms(h, g_ref[...]).astype(BF16)
    h = h + 0.5 * _swiglu_acc(n_bf, wg_ref, wu_ref, wd_ref, ck)
    o_ref[...] = _rms(h, gf_ref[...])


def _out_ffn2(h, a, r, wo, g, wg, wu, wd, gf, tm=512, ck=256):
    n = h.shape[0]
    row = pl.BlockSpec((tm, D_MODEL), lambda i: (i, 0))
    half = pl.BlockSpec((tm, D_RET), lambda i: (i, 0))
    return pl.pallas_call(
        functools.partial(_out_ffn2_kernel, ck=ck),
        grid=(n // tm,),
        in_specs=[row, half, half, _const_spec((MLA_HEADS * VDIM + D_RET, D_MODEL)),
                  _const_spec((1, D_MODEL)), _const_spec((D_MODEL, D_FF)),
                  _const_spec((D_MODEL, D_FF)), _const_spec((D_FF, D_MODEL)),
                  _const_spec((1, D_MODEL))],
        out_specs=row,
        out_shape=jax.ShapeDtypeStruct((n, D_MODEL), F32),
        compiler_params=_params(("parallel",)),
        name="out_ffn2",
    )(h, a, r, wo, g, wg, wu, wd, gf)


def _stage_w_in(w_ref, out_ref, rows=128):
    o_kpe = Q_LORA + KV_LORA
    o_rq = o_kpe + ROPE
    for r0 in range(0, D_MODEL, rows):
        x = w_ref[r0:r0 + rows, :]

        def put(off, val):
            out_ref[r0:r0 + rows, off:off + val.shape[1]] = val.astype(BF16)

        zeros = lambda w: jnp.zeros((rows, w), F32)
        put(_OFF_CQ, x[:, :o_kpe])
        kpe = x[:, o_kpe:o_rq]
        put(_OFF_CKV + LANES, jnp.concatenate([zeros(NOPE), kpe, zeros(32)], axis=1))
        put(_OFF_CKV + 2 * LANES,
            jnp.concatenate([zeros(NOPE), kpe[:, 16:], kpe[:, :16], zeros(32)], axis=1))
        put(_OFF_CKV + 3 * LANES, zeros(LANES))
        for off_out, off_in, scale in ((_OFF_RQ, o_rq, 1.0), (_OFF_RK, o_rq + D_RET, RET_DIM ** -0.5)):
            for u in range(D_RET // QUAD):
                for part in range(2):
                    src = [off_in + (4 * u + j) * RET_DIM + part * 32 for j in range(4)]
                    blk = jnp.concatenate([x[:, s:s + 32] for s in src], axis=1)
                    put(off_out + u * QUAD + part * LANES, blk * scale if scale != 1.0 else blk)
        put(_OFF_RV, x[:, o_rq + 2 * D_RET:o_rq + 4 * D_RET])


def _stage_w_uq(w_ref, out_ref):
    x = w_ref[...]
    rows = x.shape[0]
    zeros = lambda w: jnp.zeros((rows, w), F32)
    nh = MLA_HEADS * LANES
    for h in range(MLA_HEADS):
        head = x[:, h * (NOPE + ROPE):(h + 1) * (NOPE + ROPE)]
        out_ref[:, h * LANES:(h + 1) * LANES] = jnp.concatenate(
            [head, zeros(32)], axis=1).astype(BF16)
        out_ref[:, nh + h * LANES:nh + (h + 1) * LANES] = jnp.concatenate(
            [zeros(NOPE), head[:, NOPE + 16:], head[:, NOPE:NOPE + 16], zeros(32)],
            axis=1).astype(BF16)


def _stage_w_ukv(w_ref, out_ref):
    x = w_ref[...]
    zeros = jnp.zeros((x.shape[0], LANES - NOPE), F32)
    nh = MLA_HEADS * LANES
    for h in range(MLA_HEADS):
        src = h * (NOPE + VDIM)
        out_ref[:, h * LANES:(h + 1) * LANES] = jnp.concatenate(
            [x[:, src:src + NOPE], zeros], axis=1).astype(BF16)
        out_ref[:, nh + h * LANES:nh + (h + 1) * LANES] = jnp.concatenate(
            [x[:, src + NOPE:src + NOPE + VDIM], zeros], axis=1).astype(BF16)


def _proj_kernel(h_ref, pos_ref, inv_ref, pc_ref, ps_ref, gm_ref, w_in_ref, gq_ref, w_uq_ref,
                 gkv_ref, w_ukv_ref, qm_ref, km_ref, vm_ref, rq_ref, rk_ref, rv_ref, rg_ref,
                 win_ref, wq_ref, wkv_ref):
    @pl.when(pl.program_id(0) == 0)
    def _():
        _stage_w_in(w_in_ref, win_ref)
        _stage_w_uq(w_uq_ref, wq_ref)
        _stage_w_ukv(w_ukv_ref, wkv_ref)

    n_bf = _rms(h_ref[...], gm_ref[...]).astype(BF16)

    def proj(off, width):
        return jnp.dot(n_bf, win_ref[:, off:off + width], preferred_element_type=F32)

    def ret_rope(x, out_ref):
        for u in range(D_RET // QUAD):
            x1 = x[:, u * QUAD:u * QUAD + LANES]
            x2 = x[:, u * QUAD + LANES:(u + 1) * QUAD]
            out_ref[:, u * QUAD:u * QUAD + LANES] = (x1 * cr - x2 * sr).astype(BF16)
            out_ref[:, u * QUAD + LANES:(u + 1) * QUAD] = (x1 * sr + x2 * cr).astype(BF16)

    cq_bf = _rms(proj(_OFF_CQ, Q_LORA), gq_ref[...]).astype(BF16)
    lat = proj(_OFF_CKV, 4 * LANES)
    ckv_bf = _rms(lat[:, :KV_LORA], gkv_ref[...]).astype(BF16)
    xq = proj(_OFF_RQ, D_RET)

    cos_t, sin_t = _rope_tables(pos_ref[...], inv_ref[...], pc_ref[...], ps_ref[...])
    cr, sr = cos_t[:, :LANES], sin_t[:, :LANES]
    nope_lane = lax.broadcasted_iota(jnp.int32, (1, LANES), 1) < NOPE
    cm, sm = cos_t[:, LANES:] + jnp.where(nope_lane, 1.0, 0.0), sin_t[:, LANES:]

    xk = proj(_OFF_RK, D_RET)
    ret_rope(xq, rq_ref)
    rv_ref[...] = proj(_OFF_RV, D_RET).astype(BF16)
    ret_rope(xk, rk_ref)
    g = proj(_OFF_RG, D_RET)
    rg_ref[...] = (g * jax.nn.sigmoid(g)).astype(BF16)

    qq = jnp.dot(cq_bf, wq_ref[...], preferred_element_type=F32)
    kv = jnp.dot(ckv_bf, wkv_ref[...], preferred_element_type=F32)
    nh = MLA_HEADS * LANES
    kpe = lat[:, LANES:2 * LANES] * cm + lat[:, 2 * LANES:3 * LANES] * sm
    cq_s, sq_s = cm * (MLA_SCALE * LOG2E), sm * (MLA_SCALE * LOG2E)
    for h in range(MLA_HEADS):
        sl = slice(h * LANES, (h + 1) * LANES)
        sw = slice(nh + h * LANES, nh + (h + 1) * LANES)
        qm_ref[sl, :] = (qq[:, sl] * cq_s + qq[:, sw] * sq_s).astype(BF16).T
        km_ref[:, sl] = (kv[:, sl] + kpe).astype(BF16)
    lane = lax.broadcasted_iota(jnp.int32, (1, nh), 1)
    ones_lane = jnp.where(lane % LANES == VDIM, 1.0, 0.0)
    vm_ref[...] = (kv[:, nh:] + ones_lane).astype(BF16).T


def _proj(h, pos_row, gm, w_in, gq, w_uq, gkv, w_ukv, tm=1024):
    n = h.shape[0]
    row = lambda w: pl.BlockSpec((tm, w), lambda i: (i, 0))
    sds = lambda w: jax.ShapeDtypeStruct((n, w), BF16)
    col = pl.BlockSpec((MLA_HEADS * LANES, tm), lambda i: (0, i))
    sds_t = jax.ShapeDtypeStruct((MLA_HEADS * LANES, n), BF16)
    inv_col, pc, ps = _rope_constants()
    return pl.pallas_call(
        _proj_kernel,
        grid=(n // tm,),
        in_specs=[row(D_MODEL), pl.BlockSpec((1, tm), lambda i: (0, i)),
                  _const_spec((N_FREQ, 1)), _const_spec((N_FREQ, 2 * LANES)),
                  _const_spec((N_FREQ, 2 * LANES)), _const_spec((1, D_MODEL)),
                  _const_spec(w_in.shape), _const_spec((1, Q_LORA)), _const_spec(w_uq.shape),
                  _const_spec((1, KV_LORA)), _const_spec(w_ukv.shape)],
        out_specs=[col, row(1024), col, row(512), row(512), row(512), row(512)],
        out_shape=[sds_t, sds(1024), sds_t, sds(512), sds(512), sds(512), sds(512)],
        scratch_shapes=[pltpu.VMEM((D_MODEL, D_IN_AUG), BF16),
                        pltpu.VMEM((Q_LORA, 2 * MLA_HEADS * LANES), BF16),
                        pltpu.VMEM((KV_LORA, 2 * MLA_HEADS * LANES), BF16)],
        compiler_params=_params(("arbitrary",)),
        name="mixer_proj",
    )(h, pos_row, inv_col, pc, ps, gm, w_in, gq, w_uq, gkv, w_ukv)


V_ROWS = 80


def _mla_kernel(qt_ref, k0_ref, k1_ref, vt_ref, o_ref, m_ref, acc_ref, s0_ref, s1_ref, mb0_ref,
                mb1_ref, al0_ref, al1_ref, *, tq, tk, tqs, seq, unroll):
    nk = seq // tk
    slots = ((s0_ref, mb0_ref, al0_ref), (s1_ref, mb1_ref, al1_ref))
    m_ref[...] = jnp.full(m_ref.shape, -jnp.inf, F32)
    acc_ref[...] = jnp.zeros(acc_ref.shape, F32)

    def scores(c, slot, qs):
        s_ref, mb_ref, al_ref = slots[slot]
        off = c * tk if isinstance(c, int) else pl.multiple_of(c * tk, tk)
        for h in range(2):
            k = (k0_ref, k1_ref)[h][0, pl.ds(off, tk), :]
            s = jnp.dot(k, qt_ref[h * LANES:(h + 1) * LANES, qs], preferred_element_type=F32)
            s_ref[h, :, qs] = s
            m_old = m_ref[h, :, qs]
            m_new = jnp.maximum(m_old, jnp.max(s, axis=0, keepdims=True))
            m_ref[h, :, qs] = m_new
            mb_ref[h, :, qs] = m_new
            al_ref[h, :, qs] = jnp.exp2(m_old - m_new)

    def weighted_values(c, slot, qs):
        s_ref, mb_ref, al_ref = slots[slot]
        off = c * tk if isinstance(c, int) else pl.multiple_of(c * tk, tk)
        for h in range(2):
            p = jnp.exp2(s_ref[h, :, qs] - mb_ref[h, 0:1, qs])
            vt = vt_ref[h * LANES:h * LANES + V_ROWS, pl.ds(off, tk)]
            acc_ref[h, :, qs] = (al_ref[h, 0:1, qs] * acc_ref[h, :, qs]
                                 + jnp.dot(vt, p.astype(BF16), preferred_element_type=F32))

    q_slices = [slice(i * tqs, (i + 1) * tqs) for i in range(tq // tqs)]

    def chunk_group(c0, last):
        for j in range(unroll):
            for qs in q_slices:
                if not (last and j == unroll - 1):
                    scores(c0 + j + 1, (j + 1) % 2, qs)
                weighted_values(c0 + j, j % 2, qs)

    for qs in q_slices:
        scores(0, 0, qs)

    def body(i, carry):
        chunk_group(i * unroll, False)
        return carry

    lax.fori_loop(0, nk // unroll - 1, body, 0)
    chunk_group(nk - unroll, True)

    outs = []
    for h in range(2):
        acc = acc_ref[h]
        outs.append(acc[:VDIM, :] / acc[VDIM:VDIM + 1, :])
    o_ref[0] = jnp.concatenate(outs, axis=0).T.astype(BF16)


def _mla_attention(qt, k, vt, tq=2048, tk=512, tqs=256, unroll=2):
    b, s, _ = k.shape
    npair = MLA_HEADS // 2
    assert unroll % 2 == 0 and (s // tk) % unroll == 0
    nq = s // tq
    rowstate = pltpu.VMEM((2, 8, tq), F32)
    sbuf = pltpu.VMEM((2, tk, tq + LANES), F32)
    return pl.pallas_call(
        functools.partial(_mla_kernel, tq=tq, tk=tk, tqs=tqs, seq=s, unroll=unroll),
        grid=(b, npair, nq),
        in_specs=[pl.BlockSpec((2 * LANES, tq), lambda bi, p, i: (p, bi * nq + i)),
                  pl.BlockSpec((1, s, LANES), lambda bi, p, i: (bi, 0, 2 * p)),
                  pl.BlockSpec((1, s, LANES), lambda bi, p, i: (bi, 0, 2 * p + 1)),
                  pl.BlockSpec((2 * LANES, s), lambda bi, p, i: (p, bi))],
        out_specs=pl.BlockSpec((1, tq, LANES), lambda bi, p, i: (bi, i, p)),
        out_shape=jax.ShapeDtypeStruct((b, s, MLA_HEADS * VDIM), BF16),
        scratch_shapes=[rowstate, pltpu.VMEM((2, V_ROWS, tq), F32), sbuf, sbuf,
                        rowstate, rowstate, rowstate, rowstate],
        compiler_params=_params(("parallel", "parallel", "parallel")),
        name="mla_attention",
    )(qt, k, k, vt)


def _log_sigmoid(x):
    return jnp.minimum(x, 0.0) - jnp.log1p(jnp.exp(-jnp.abs(x)))


def _retention_kernel(q_ref, k_ref, v_ref, g_ref, dec_ref, o_ref, acc_lo_ref, acc_hi_ref,
                      stf_ref, stb_ref, dmf_ref, dmb_ref, tab_ref, ones_ref, *, ch, seq, unroll):
    nc = seq // ch
    half = nc // 2
    nh = QUAD // RET_DIM
    lane = lax.broadcasted_iota(jnp.int32, (ch, QUAD), 1)
    row = lax.broadcasted_iota(jnp.int32, (ch, QUAD), 0).astype(F32)
    khead = (lane >> 5) & (nh - 1)
    vhead = lane >> 6
    ri = lax.broadcasted_iota(jnp.int32, (ch, ch), 0)
    ci = lax.broadcasted_iota(jnp.int32, (ch, ch), 1)
    sd = lax.broadcasted_iota(jnp.int32, (QUAD, QUAD), 0)
    se = lax.broadcasted_iota(jnp.int32, (QUAD, QUAD), 1)
    state_mask = ((sd >> 5) & (nh - 1)) == (se >> 6)
    ones_ref[...] = jnp.where((sd >> 6) == (se >> 6), 1.0, 0.0).astype(BF16)

    lg = _log_sigmoid(dec_ref[...])
    tab_ref[0] = jnp.exp(lg[0:1] * (row + 1.0))
    tab_ref[1] = jnp.exp(lg[1:2] * (ch - 1.0 - row))
    tab_ref[2] = jnp.exp(lg[2:3] * (ch - row))
    tab_ref[3] = jnp.exp(lg[3:4] * row)
    c_dec = (jnp.exp(lg[0:1] * float(ch)), jnp.exp(lg[2:3] * float(ch)))
    rel_f = jnp.maximum(ri - ci, 0).astype(F32)
    rel_b = jnp.maximum(ci - ri, 0).astype(F32)
    for h in range(nh):
        dmf_ref[h] = jnp.where(ri >= ci, jnp.exp(lg[0:1, h * RET_DIM:h * RET_DIM + 1] * rel_f), 0.0)
        dmb_ref[h] = jnp.where(ci > ri, jnp.exp(lg[2:3, h * RET_DIM:h * RET_DIM + 1] * rel_b), 0.0)
    stf_ref[...] = jnp.zeros_like(stf_ref)
    stb_ref[...] = jnp.zeros_like(stb_ref)

    def chunks(items):
        loaded = []
        for c, _ in items:
            off = pl.multiple_of(c * ch, ch)
            loaded.append((q_ref[0, pl.ds(off, ch), :], k_ref[0, pl.ds(off, ch), :],
                           v_ref[0, pl.ds(off, ch), :]))
        scores = []
        for qc, kc, _ in loaded:
            zero = jnp.zeros_like(qc)
            scores.append([lax.dot_general(jnp.where(khead == h, qc, zero), kc,
                                           (((1,), (1,)), ((), ())), preferred_element_type=F32)
                           for h in range(nh)])
        outs = []
        for (_, fwd), (_, _, vc), ss in zip(items, loaded, scores):
            dm_ref = dmf_ref if fwd else dmb_ref
            out = None
            for h in range(nh):
                ih = jnp.dot((ss[h] * dm_ref[h]).astype(BF16), vc, preferred_element_type=F32)
                out = ih if out is None else jnp.where(vhead == h, ih, out)
            outs.append(out)
        upds = []
        for (_, fwd), (_, kc, vc) in zip(items, loaded):
            kd = (kc.astype(F32) * tab_ref[1 if fwd else 3]).astype(BF16)
            upds.append(lax.dot_general(kd, vc, (((0,), (0,)), ((), ())),
                                        preferred_element_type=F32))
        for i, ((_, fwd), (qc, _, _)) in enumerate(zip(items, loaded)):
            st_ref = stf_ref if fwd else stb_ref
            state = st_ref[...]
            cross = jnp.dot(qc, state.astype(BF16), preferred_element_type=F32)
            outs[i] = outs[i] + cross * tab_ref[0 if fwd else 2]
            st_ref[...] = state * c_dec[0 if fwd else 1] + jnp.where(state_mask, upds[i], 0.0)
        return outs

    def head_sum(x):
        hi = x.astype(BF16)
        lo = (x - hi.astype(F32)).astype(BF16)
        return (jnp.dot(hi, ones_ref[...], preferred_element_type=F32)
                + jnp.dot(lo, ones_ref[...], preferred_element_type=F32))

    def finish(c, tot):
        off = pl.multiple_of(c * ch, ch)
        d = tot - head_sum(tot) * (1.0 / RET_DIM)
        var = head_sum(d * d) * (1.0 / RET_DIM)
        gate = g_ref[0, pl.ds(off, ch), :].astype(F32)
        o_ref[0, pl.ds(off, ch), :] = (d * lax.rsqrt(var + EPS) * gate).astype(BF16)

    def rows(ref, c):
        return ref.at[pl.ds(pl.multiple_of(c * ch, ch), ch), :]

    def pairs(t):
        items = []
        for j in range(unroll):
            cf = t * unroll + j
            items += [(cf, True), (nc - 1 - cf, False)]
        return items

    def first(t, carry):
        items = pairs(t)
        for (c, fwd), out in zip(items, chunks(items)):
            if fwd:
                rows(acc_lo_ref, c)[...] = out
            else:
                rows(acc_hi_ref, c - half)[...] = out
        return carry

    lax.fori_loop(0, half // unroll, first, 0)

    def second(t, carry):
        items = pairs(t)
        for (c, fwd), out in zip(items, chunks(items)):
            other = rows(acc_hi_ref, c - half) if fwd else rows(acc_lo_ref, c)
            finish(c, out + other[...])
        return carry

    lax.fori_loop(half // unroll, nc // unroll, second, 0)


def _retention(q, k, v, g, dec, ch=256, unroll=2):
    b, s, _ = q.shape
    nquad = D_RET // QUAD
    assert (s // ch // 2) % unroll == 0
    blk = pl.BlockSpec((1, s, QUAD), lambda bi, u: (bi, 0, u))
    return pl.pallas_call(
        functools.partial(_retention_kernel, ch=ch, seq=s, unroll=unroll),
        grid=(b, nquad),
        in_specs=[blk, blk, blk, blk, pl.BlockSpec((4, QUAD), lambda bi, u: (0, u))],
        out_specs=blk,
        out_shape=jax.ShapeDtypeStruct((b, s, D_RET), BF16),
        scratch_shapes=[pltpu.VMEM((s // 2, QUAD), F32), pltpu.VMEM((s // 2, QUAD), F32),
                        pltpu.VMEM((QUAD, QUAD), F32), pltpu.VMEM((QUAD, QUAD), F32),
                        pltpu.VMEM((QUAD // RET_DIM, ch, ch), F32),
                        pltpu.VMEM((QUAD // RET_DIM, ch, ch), F32),
                        pltpu.VMEM((4, ch, QUAD), F32), pltpu.VMEM((QUAD, QUAD), BF16)],
        compiler_params=_params(("parallel", "parallel")),
        name="retention",
    )(q, k, v, g, dec)


def _prep_decay(dec_f, dec_b):
    nquad, per_quad = D_RET // QUAD, QUAD // RET_DIM
    rows = []
    for d in (dec_f.astype(F32), dec_b.astype(F32)):
        rows.append(jnp.broadcast_to(d[:, None], (RET_HEADS, RET_DIM)).reshape(D_RET))
        rows.append(jnp.broadcast_to(d.reshape(nquad, 1, per_quad, 1),
                                     (nquad, 2, per_quad, 32)).reshape(D_RET))
    return jnp.stack(rows)


def kernel(x, positions, ffn1_norm, ffn1_w_gate, ffn1_w_up, ffn1_w_down, mix_norm, w_in, q_norm, w_uq, kv_norm, w_ukv, ret_decay_fwd, ret_decay_bwd, w_o, ffn2_norm, ffn2_w_gate, ffn2_w_up, ffn2_w_down, final_norm):
    b, s, d = x.shape
    n = b * s
    assert ffn1_norm.shape[0] == 1, "specialised to DEPTH == 1 (the final norm is fused into the layer)"
    pos_row = positions.astype(F32).reshape(1, n)

    h = _ffn1(x.reshape(n, d), ffn1_norm[0][None, :], ffn1_w_gate[0], ffn1_w_up[0],
              ffn1_w_down[0])
    qm, km, vm, rq, rk, rv, rg = _proj(
        h, pos_row, mix_norm[0][None, :], w_in[0], q_norm[0][None, :],
        w_uq[0], kv_norm[0][None, :], w_ukv[0])
    a = _mla_attention(qm, km.reshape(b, s, -1), vm)
    r = _retention(rq.reshape(b, s, -1), rk.reshape(b, s, -1), rv.reshape(b, s, -1),
                   rg.reshape(b, s, -1), _prep_decay(ret_decay_fwd[0], ret_decay_bwd[0]))
    wo = w_o[0].astype(BF16)
    out = _out_ffn2(h, a.reshape(n, -1), r.reshape(n, -1), wo, ffn2_norm[0][None, :],
                    ffn2_w_gate[0], ffn2_w_up[0], ffn2_w_down[0], final_norm[None, :])
    return out.reshape(b, s, d)
```

```python
import functools
import math

import numpy as np
import jax
import jax.numpy as jnp
from jax import lax
from jax.experimental import pallas as pl
from jax.experimental.pallas import tpu as pltpu

F32 = jnp.float32
BF16 = jnp.bfloat16

D_MODEL = 1024
D_FF = 2816
MLA_HEADS = 8
Q_LORA = 256
KV_LORA = 128
NOPE = 64
ROPE = 32
VDIM = 64
RET_HEADS = 8
RET_DIM = 64
D_RET = RET_HEADS * RET_DIM
ROPE_THETA = 10000.0
EPS = 1e-6
MLA_SCALE = (NOPE + ROPE) ** -0.5
LOG2E = math.log2(math.e)

LANES = 128
QUAD = 256
VMEM_LIMIT = 56 * 1024 * 1024

_OFF_CQ = 0
_OFF_CKV = 256
_OFF_RQ = 768
_OFF_RK = 1280
_OFF_RV = 1792
_OFF_RG = 2304
D_IN_AUG = 2816


def _params(sem):
    return pltpu.CompilerParams(dimension_semantics=sem, vmem_limit_bytes=VMEM_LIMIT)


def _const_spec(shape):
    nd = len(shape)
    return pl.BlockSpec(shape, lambda *_: (0,) * nd, pipeline_mode=pl.Buffered(1))


N_FREQ = 48


def _rope_constants():
    row = np.arange(N_FREQ)
    inv = np.where(row < 32, ROPE_THETA ** (-(row % 32) / 32.0),
                   np.where(row < 48, ROPE_THETA ** (-((row - 32) % 16) / 16.0), 0.0))
    pc = np.zeros((N_FREQ, 2 * LANES))
    ps = np.zeros((N_FREQ, 2 * LANES))
    for l in range(LANES):
        pc[l % 32, l] = 1.0
        ps[l % 32, l] = 1.0
    for l in range(NOPE, NOPE + ROPE):
        r = 32 + (l - NOPE) % 16
        pc[r, LANES + l] = 1.0
        ps[r, LANES + l] = -1.0 if l < NOPE + 16 else 1.0
    return jnp.asarray(inv[:, None], F32), jnp.asarray(pc, BF16), jnp.asarray(ps, BF16)


def _rope_tables(pos_row, inv_col, pc, ps):
    ang = inv_col * pos_row
    dn = (((0,), (0,)), ((), ()))

    def place(x, p):
        hi = x.astype(BF16)
        lo = (x - hi.astype(F32)).astype(BF16)
        return (lax.dot_general(hi, p, dn, preferred_element_type=F32)
                + lax.dot_general(lo, p, dn, preferred_element_type=F32))

    return place(jnp.cos(ang), pc), place(jnp.sin(ang), ps)


def _rms(x, g):
    return x * lax.rsqrt(jnp.mean(x * x, axis=-1, keepdims=True) + EPS) * g


def _swiglu_acc(n_bf, wg_ref, wu_ref, wd_ref, ck):
    acc = None
    for j in range(D_FF // ck):
        sl = slice(j * ck, (j + 1) * ck)
        g = jnp.dot(n_bf, wg_ref[:, sl].astype(BF16), preferred_element_type=F32)
        u = jnp.dot(n_bf, wu_ref[:, sl].astype(BF16), preferred_element_type=F32)
        hm = (g * jax.nn.sigmoid(g) * u).astype(BF16)
        d = jnp.dot(hm, wd_ref[sl, :].astype(BF16), preferred_element_type=F32)
        acc = d if acc is None else acc + d
    return acc


def _ffn1_kernel(x_ref, g_ref, wg_ref, wu_ref, wd_ref, o_ref, *, ck):
    x = x_ref[...]
    n_bf = _rms(x, g_ref[...]).astype(BF16)
    o_ref[...] = x + 0.5 * _swiglu_acc(n_bf, wg_ref, wu_ref, wd_ref, ck)


def _ffn1(x, g, wg, wu, wd, tm=512, ck=256):
    n = x.shape[0]
    row = pl.BlockSpec((tm, D_MODEL), lambda i: (i, 0))
    return pl.pallas_call(
        functools.partial(_ffn1_kernel, ck=ck),
        grid=(n // tm,),
        in_specs=[row, _const_spec((1, D_MODEL)), _const_spec((D_MODEL, D_FF)),
                  _const_spec((D_MODEL, D_FF)), _const_spec((D_FF, D_MODEL))],
        out_specs=row,
        out_shape=jax.ShapeDtypeStruct((n, D_MODEL), F32),
        compiler_params=_params(("parallel",)),
        name="ffn1",
    )(x, g, wg, wu, wd)


def _out_ffn2_kernel(h_ref, a_ref, r_ref, wo_ref, g_ref, wg_ref, wu_ref, wd_ref, gf_ref, o_ref,
                     *, ck):
    h = (h_ref[...]
         + jnp.dot(a_ref[...], wo_ref[:MLA_HEADS * VDIM, :], preferred_element_type=F32)
         + jnp.dot(r_ref[...], wo_ref[MLA_HEADS * VDIM:, :], preferred_element_type=F32))
    n_bf = _rms(h, g_ref[...]).astype(BF16)
    h = h + 0.5 * _swiglu_acc(n_bf, wg_ref, wu_ref, wd_ref, ck)
    o_ref[...] = _rms(h, gf_ref[...])


def _out_ffn2(h, a, r, wo, g, wg, wu, wd, gf, tm=512, ck=256):
    n = h.shape[0]
    row = pl.BlockSpec((tm, D_MODEL), lambda i: (i, 0))
    half = pl.BlockSpec((tm, D_RET), lambda i: (i, 0))
    return pl.pallas_call(
        functools.partial(_out_ffn2_kernel, ck=ck),
        grid=(n // tm,),
        in_specs=[row, half, half, _const_spec((MLA_HEADS * VDIM + D_RET, D_MODEL)),
                  _const_spec((1, D_MODEL)), _const_spec((D_MODEL, D_FF)),
                  _const_spec((D_MODEL, D_FF)), _const_spec((D_FF, D_MODEL)),
                  _const_spec((1, D_MODEL))],
        out_specs=row,
        out_shape=jax.ShapeDtypeStruct((n, D_MODEL), F32),
        compiler_params=_params(("parallel",)),
        name="out_ffn2",
    )(h, a, r, wo, g, wg, wu, wd, gf)


def _stage_w_in(w_ref, out_ref, rows=128):
    o_kpe = Q_LORA + KV_LORA
    o_rq = o_kpe + ROPE
    for r0 in range(0, D_MODEL, rows):
        x = w_ref[r0:r0 + rows, :]

        def put(off, val):
            out_ref[r0:r0 + rows, off:off + val.shape[1]] = val.astype(BF16)

        zeros = lambda w: jnp.zeros((rows, w), F32)
        put(_OFF_CQ, x[:, :o_kpe])
        kpe = x[:, o_kpe:o_rq]
        put(_OFF_CKV + LANES, jnp.concatenate([zeros(NOPE), kpe, zeros(32)], axis=1))
        put(_OFF_CKV + 2 * LANES,
            jnp.concatenate([zeros(NOPE), kpe[:, 16:], kpe[:, :16], zeros(32)], axis=1))
        put(_OFF_CKV + 3 * LANES, zeros(LANES))
        for off_out, off_in, scale in ((_OFF_RQ, o_rq, 1.0), (_OFF_RK, o_rq + D_RET, RET_DIM ** -0.5)):
            for u in range(D_RET // QUAD):
                for part in range(2):
                    src = [off_in + (4 * u + j) * RET_DIM + part * 32 for j in range(4)]
                    blk = jnp.concatenate([x[:, s:s + 32] for s in src], axis=1)
                    put(off_out + u * QUAD + part * LANES, blk * scale if scale != 1.0 else blk)
        put(_OFF_RV, x[:, o_rq + 2 * D_RET:o_rq + 4 * D_RET])


def _stage_w_uq(w_ref, out_ref):
    x = w_ref[...]
    rows = x.shape[0]
    zeros = lambda w: jnp.zeros((rows, w), F32)
    nh = MLA_HEADS * LANES
    for h in range(MLA_HEADS):
        head = x[:, h * (NOPE + ROPE):(h + 1) * (NOPE + ROPE)]
        out_ref[:, h * LANES:(h + 1) * LANES] = jnp.concatenate(
            [head, zeros(32)], axis=1).astype(BF16)
        out_ref[:, nh + h * LANES:nh + (h + 1) * LANES] = jnp.concatenate(
            [zeros(NOPE), head[:, NOPE + 16:], head[:, NOPE:NOPE + 16], zeros(32)],
            axis=1).astype(BF16)


def _stage_w_ukv(w_ref, out_ref):
    x = w_ref[...]
    zeros = jnp.zeros((x.shape[0], LANES - NOPE), F32)
    nh = MLA_HEADS * LANES
    for h in range(MLA_HEADS):
        src = h * (NOPE + VDIM)
        out_ref[:, h * LANES:(h + 1) * LANES] = jnp.concatenate(
            [x[:, src:src + NOPE], zeros], axis=1).astype(BF16)
        out_ref[:, nh + h * LANES:nh + (h + 1) * LANES] = jnp.concatenate(
            [x[:, src + NOPE:src + NOPE + VDIM], zeros], axis=1).astype(BF16)


def _proj_kernel(h_ref, pos_ref, inv_ref, pc_ref, ps_ref, gm_ref, w_in_ref, gq_ref, w_uq_ref,
                 gkv_ref, w_ukv_ref, qm_ref, km_ref, vm_ref, rq_ref, rk_ref, rv_ref, rg_ref,
                 win_ref, wq_ref, wkv_ref):
    @pl.when(pl.program_id(0) == 0)
    def _():
        _stage_w_in(w_in_ref, win_ref)
        _stage_w_uq(w_uq_ref, wq_ref)
        _stage_w_ukv(w_ukv_ref, wkv_ref)

    n_bf = _rms(h_ref[...], gm_ref[...]).astype(BF16)

    def proj(off, width):
        return jnp.dot(n_bf, win_ref[:, off:off + width], preferred_element_type=F32)

    def ret_rope(x, out_ref):
        for u in range(D_RET // QUAD):
            x1 = x[:, u * QUAD:u * QUAD + LANES]
            x2 = x[:, u * QUAD + LANES:(u + 1) * QUAD]
            out_ref[:, u * QUAD:u * QUAD + LANES] = (x1 * cr - x2 * sr).astype(BF16)
            out_ref[:, u * QUAD + LANES:(u + 1) * QUAD] = (x1 * sr + x2 * cr).astype(BF16)

    cq_bf = _rms(proj(_OFF_CQ, Q_LORA), gq_ref[...]).astype(BF16)
    lat = proj(_OFF_CKV, 4 * LANES)
    ckv_bf = _rms(lat[:, :KV_LORA], gkv_ref[...]).astype(BF16)
    xq = proj(_OFF_RQ, D_RET)

    cos_t, sin_t = _rope_tables(pos_ref[...], inv_ref[...], pc_ref[...], ps_ref[...])
    cr, sr = cos_t[:, :LANES], sin_t[:, :LANES]
    nope_lane = lax.broadcasted_iota(jnp.int32, (1, LANES), 1) < NOPE
    cm, sm = cos_t[:, LANES:] + jnp.where(nope_lane, 1.0, 0.0), sin_t[:, LANES:]

    xk = proj(_OFF_RK, D_RET)
    ret_rope(xq, rq_ref)
    rv_ref[...] = proj(_OFF_RV, D_RET).astype(BF16)
    ret_rope(xk, rk_ref)
    g = proj(_OFF_RG, D_RET)
    rg_ref[...] = (g * jax.nn.sigmoid(g)).astype(BF16)

    qq = jnp.dot(cq_bf, wq_ref[...], preferred_element_type=F32)
    kv = jnp.dot(ckv_bf, wkv_ref[...], preferred_element_type=F32)
    nh = MLA_HEADS * LANES
    kpe = lat[:, LANES:2 * LANES] * cm + lat[:, 2 * LANES:3 * LANES] * sm
    cq_s, sq_s = cm * (MLA_SCALE * LOG2E), sm * (MLA_SCALE * LOG2E)
    for h in range(MLA_HEADS):
        sl = slice(h * LANES, (h + 1) * LANES)
        sw = slice(nh + h * LANES, nh + (h + 1) * LANES)
        qm_ref[sl, :] = (qq[:, sl] * cq_s + qq[:, sw] * sq_s).astype(BF16).T
        km_ref[:, sl] = (kv[:, sl] + kpe).astype(BF16)
    lane = lax.broadcasted_iota(jnp.int32, (1, nh), 1)
    ones_lane = jnp.where(lane % LANES == VDIM, 1.0, 0.0)
    vm_ref[...] = (kv[:, nh:] + ones_lane).astype(BF16).T


def _proj(h, pos_row, gm, w_in, gq, w_uq, gkv, w_ukv, tm=1024):
    n = h.shape[0]
    row = lambda w: pl.BlockSpec((tm, w), lambda i: (i, 0))
    sds = lambda w: jax.ShapeDtypeStruct((n, w), BF16)
    col = pl.BlockSpec((MLA_HEADS * LANES, tm), lambda i: (0, i))
    sds_t = jax.ShapeDtypeStruct((MLA_HEADS * LANES, n), BF16)
    inv_col, pc, ps = _rope_constants()
    return pl.pallas_call(
        _proj_kernel,
        grid=(n // tm,),
        in_specs=[row(D_MODEL), pl.BlockSpec((1, tm), lambda i: (0, i)),
                  _const_spec((N_FREQ, 1)), _const_spec((N_FREQ, 2 * LANES)),
                  _const_spec((N_FREQ, 2 * LANES)), _const_spec((1, D_MODEL)),
                  _const_spec(w_in.shape), _const_spec((1, Q_LORA)), _const_spec(w_uq.shape),
                  _const_spec((1, KV_LORA)), _const_spec(w_ukv.shape)],
        out_specs=[col, row(1024), col, row(512), row(512), row(512), row(512)],
        out_shape=[sds_t, sds(1024), sds_t, sds(512), sds(512), sds(512), sds(512)],
        scratch_shapes=[pltpu.VMEM((D_MODEL, D_IN_AUG), BF16),
                        pltpu.VMEM((Q_LORA, 2 * MLA_HEADS * LANES), BF16),
                        pltpu.VMEM((KV_LORA, 2 * MLA_HEADS * LANES), BF16)],
        compiler_params=_params(("arbitrary",)),
        name="mixer_proj",
    )(h, pos_row, inv_col, pc, ps, gm, w_in, gq, w_uq, gkv, w_ukv)


V_ROWS = 80


def _mla_kernel(qt_ref, k0_ref, k1_ref, vt_ref, o_ref, m_ref, acc_ref, s0_ref, s1_ref, mb0_ref,
                mb1_ref, al0_ref, al1_ref, *, tq, tk, tqs, seq, unroll):
    nk = seq // tk
    slots = ((s0_ref, mb0_ref, al0_ref), (s1_ref, mb1_ref, al1_ref))
    m_ref[...] = jnp.full(m_ref.shape, -jnp.inf, F32)
    acc_ref[...] = jnp.zeros(acc_ref.shape, F32)

    def scores(c, slot, qs):
        s_ref, mb_ref, al_ref = slots[slot]
        off = c * tk if isinstance(c, int) else pl.multiple_of(c * tk, tk)
        for h in range(2):
            k = (k0_ref, k1_ref)[h][0, pl.ds(off, tk), :]
            s = jnp.dot(k, qt_ref[h * LANES:(h + 1) * LANES, qs], preferred_element_type=F32)
            s_ref[h, :, qs] = s
            m_old = m_ref[h, :, qs]
            m_new = jnp.maximum(m_old, jnp.max(s, axis=0, keepdims=True))
            m_ref[h, :, qs] = m_new
            mb_ref[h, :, qs] = m_new
            al_ref[h, :, qs] = jnp.exp2(m_old - m_new)

    def weighted_values(c, slot, qs):
        s_ref, mb_ref, al_ref = slots[slot]
        off = c * tk if isinstance(c, int) else pl.multiple_of(c * tk, tk)
        for h in range(2):
            p = jnp.exp2(s_ref[h, :, qs] - mb_ref[h, 0:1, qs])
            vt = vt_ref[h * LANES:h * LANES + V_ROWS, pl.ds(off, tk)]
            acc_ref[h, :, qs] = (al_ref[h, 0:1, qs] * acc_ref[h, :, qs]
                                 + jnp.dot(vt, p.astype(BF16), preferred_element_type=F32))

    q_slices = [slice(i * tqs, (i + 1) * tqs) for i in range(tq // tqs)]

    def chunk_group(c0, last):
        for j in range(unroll):
            for qs in q_slices:
                if not (last and j == unroll - 1):
                    scores(c0 + j + 1, (j + 1) % 2, qs)
                weighted_values(c0 + j, j % 2, qs)

    for qs in q_slices:
        scores(0, 0, qs)

    def body(i, carry):
        chunk_group(i * unroll, False)
        return carry

    lax.fori_loop(0, nk // unroll - 1, body, 0)
    chunk_group(nk - unroll, True)

    outs = []
    for h in range(2):
        acc = acc_ref[h, :, :tq]
        outs.append(acc[:VDIM, :] / acc[VDIM:VDIM + 1, :])
    o_ref[0] = jnp.concatenate(outs, axis=0).T.astype(BF16)


def _mla_attention(qt, k, vt, tq=2048, tk=512, tqs=256, unroll=2):
    b, s, _ = k.shape
    npair = MLA_HEADS // 2
    assert unroll % 2 == 0 and (s // tk) % unroll == 0
    nq = s // tq
    rowstate = pltpu.VMEM((2, 8, tq + LANES), F32)
    sbuf = pltpu.VMEM((2, tk, tq + LANES), F32)
    return pl.pallas_call(
        functools.partial(_mla_kernel, tq=tq, tk=tk, tqs=tqs, seq=s, unroll=unroll),
        grid=(b, npair, nq),
        in_specs=[pl.BlockSpec((2 * LANES, tq), lambda bi, p, i: (p, bi * nq + i)),
                  pl.BlockSpec((1, s, LANES), lambda bi, p, i: (bi, 0, 2 * p)),
                  pl.BlockSpec((1, s, LANES), lambda bi, p, i: (bi, 0, 2 * p + 1)),
                  pl.BlockSpec((2 * LANES, s), lambda bi, p, i: (p, bi))],
        out_specs=pl.BlockSpec((1, tq, LANES), lambda bi, p, i: (bi, i, p)),
        out_shape=jax.ShapeDtypeStruct((b, s, MLA_HEADS * VDIM), BF16),
        scratch_shapes=[rowstate, pltpu.VMEM((2, V_ROWS, tq + LANES), F32), sbuf, sbuf,
                        rowstate, rowstate, rowstate, rowstate],
        compiler_params=_params(("parallel", "parallel", "parallel")),
        name="mla_attention",
    )(qt, k, k, vt)


def _log_sigmoid(x):
    return jnp.minimum(x, 0.0) - jnp.log1p(jnp.exp(-jnp.abs(x)))


def _retention_kernel(q_ref, k_ref, v_ref, g_ref, dec_ref, o_ref, acc_lo_ref, acc_hi_ref,
                      stf_ref, stb_ref, dmf_ref, dmb_ref, tab_ref, ones_ref, *, ch, seq, unroll):
    nc = seq // ch
    half = nc // 2
    nh = QUAD // RET_DIM
    lane = lax.broadcasted_iota(jnp.int32, (ch, QUAD), 1)
    row = lax.broadcasted_iota(jnp.int32, (ch, QUAD), 0).astype(F32)
    khead = (lane >> 5) & (nh - 1)
    vhead = lane >> 6
    ri = lax.broadcasted_iota(jnp.int32, (ch, ch), 0)
    ci = lax.broadcasted_iota(jnp.int32, (ch, ch), 1)
    sd = lax.broadcasted_iota(jnp.int32, (QUAD, QUAD), 0)
    se = lax.broadcasted_iota(jnp.int32, (QUAD, QUAD), 1)
    state_mask = ((sd >> 5) & (nh - 1)) == (se >> 6)
    ones_ref[...] = jnp.where((sd >> 6) == (se >> 6), 1.0, 0.0).astype(BF16)

    lg = _log_sigmoid(dec_ref[...])
    tab_ref[0] = jnp.exp(lg[0:1] * (row + 1.0))
    tab_ref[1] = jnp.exp(lg[1:2] * (ch - 1.0 - row))
    tab_ref[2] = jnp.exp(lg[2:3] * (ch - row))
    tab_ref[3] = jnp.exp(lg[3:4] * row)
    c_dec = (jnp.exp(lg[0:1] * float(ch)), jnp.exp(lg[2:3] * float(ch)))
    rel_f = jnp.maximum(ri - ci, 0).astype(F32)
    rel_b = jnp.maximum(ci - ri, 0).astype(F32)
    for h in range(nh):
        dmf_ref[h] = jnp.where(ri >= ci, jnp.exp(lg[0:1, h * RET_DIM:h * RET_DIM + 1] * rel_f), 0.0)
        dmb_ref[h] = jnp.where(ci > ri, jnp.exp(lg[2:3, h * RET_DIM:h * RET_DIM + 1] * rel_b), 0.0)
    stf_ref[...] = jnp.zeros_like(stf_ref)
    stb_ref[...] = jnp.zeros_like(stb_ref)

    def chunks(items):
        loaded = []
        for c, _ in items:
            off = pl.multiple_of(c * ch, ch)
            loaded.append((q_ref[0, pl.ds(off, ch), :], k_ref[0, pl.ds(off, ch), :],
                           v_ref[0, pl.ds(off, ch), :]))
        scores = []
        for qc, kc, _ in loaded:
            zero = jnp.zeros_like(qc)
            scores.append([lax.dot_general(jnp.where(khead == h, qc, zero), kc,
                                           (((1,), (1,)), ((), ())), preferred_element_type=F32)
                           for h in range(nh)])
        outs = []
        for (_, fwd), (_, _, vc), ss in zip(items, loaded, scores):
            dm_ref = dmf_ref if fwd else dmb_ref
            out = None
            for h in range(nh):
                ih = jnp.dot((ss[h] * dm_ref[h]).astype(BF16), vc, preferred_element_type=F32)
                out = ih if out is None else jnp.where(vhead == h, ih, out)
            outs.append(out)
        upds = []
        for (_, fwd), (_, kc, vc) in zip(items, loaded):
            kd = (kc.astype(F32) * tab_ref[1 if fwd else 3]).astype(BF16)
            upds.append(lax.dot_general(kd, vc, (((0,), (0,)), ((), ())),
                                        preferred_element_type=F32))
        for i, ((_, fwd), (qc, _, _)) in enumerate(zip(items, loaded)):
            st_ref = stf_ref if fwd else stb_ref
            state = st_ref[...]
            cross = jnp.dot(qc, state.astype(BF16), preferred_element_type=F32)
            outs[i] = outs[i] + cross * tab_ref[0 if fwd else 2]
            st_ref[...] = state * c_dec[0 if fwd else 1] + jnp.where(state_mask, upds[i], 0.0)
        return outs

    def head_sum(x):
        hi = x.astype(BF16)
        lo = (x - hi.astype(F32)).astype(BF16)
        return (jnp.dot(hi, ones_ref[...], preferred_element_type=F32)
                + jnp.dot(lo, ones_ref[...], preferred_element_type=F32))

    def finish(c, tot):
        off = pl.multiple_of(c * ch, ch)
        d = tot - head_sum(tot) * (1.0 / RET_DIM)
        var = head_sum(d * d) * (1.0 / RET_DIM)
        gate = g_ref[0, pl.ds(off, ch), :].astype(F32)
        o_ref[0, pl.ds(off, ch), :] = (d * lax.rsqrt(var + EPS) * gate).astype(BF16)

    def rows(ref, c):
        return ref.at[pl.ds(pl.multiple_of(c * ch, ch), ch), :]

    def pairs(t):
        items = []
        for j in range(unroll):
            cf = t * unroll + j
            items += [(cf, True), (nc - 1 - cf, False)]
        return items

    def first(t, carry):
        items = pairs(t)
        for (c, fwd), out in zip(items, chunks(items)):
            if fwd:
                rows(acc_lo_ref, c)[...] = out
            else:
                rows(acc_hi_ref, c - half)[...] = out
        return carry

    lax.fori_loop(0, half // unroll, first, 0)

    def second(t, carry):
        items = pairs(t)
        for (c, fwd), out in zip(items, chunks(items)):
            other = rows(acc_hi_ref, c - half) if fwd else rows(acc_lo_ref, c)
            finish(c, out + other[...])
        return carry

    lax.fori_loop(half // unroll, nc // unroll, second, 0)


def _retention(q, k, v, g, dec, ch=256, unroll=2):
    b, s, _ = q.shape
    nquad = D_RET // QUAD
    assert (s // ch // 2) % unroll == 0
    blk = pl.BlockSpec((1, s, QUAD), lambda bi, u: (bi, 0, u))
    return pl.pallas_call(
        functools.partial(_retention_kernel, ch=ch, seq=s, unroll=unroll),
        grid=(b, nquad),
        in_specs=[blk, blk, blk, blk, pl.BlockSpec((4, QUAD), lambda bi, u: (0, u))],
        out_specs=blk,
        out_shape=jax.ShapeDtypeStruct((b, s, D_RET), BF16),
        scratch_shapes=[pltpu.VMEM((s // 2, QUAD), F32), pltpu.VMEM((s // 2, QUAD), F32),
                        pltpu.VMEM((QUAD, QUAD), F32), pltpu.VMEM((QUAD, QUAD), F32),
                        pltpu.VMEM((QUAD // RET_DIM, ch, ch), F32),
                        pltpu.VMEM((QUAD // RET_DIM, ch, ch), F32),
                        pltpu.VMEM((4, ch, QUAD), F32), pltpu.VMEM((QUAD, QUAD), BF16)],
        compiler_params=_params(("parallel", "parallel")),
        name="retention",
    )(q, k, v, g, dec)


def _prep_decay(dec_f, dec_b):
    nquad, per_quad = D_RET // QUAD, QUAD // RET_DIM
    rows = []
    for d in (dec_f.astype(F32), dec_b.astype(F32)):
        rows.append(jnp.broadcast_to(d[:, None], (RET_HEADS, RET_DIM)).reshape(D_RET))
        rows.append(jnp.broadcast_to(d.reshape(nquad, 1, per_quad, 1),
                                     (nquad, 2, per_quad, 32)).reshape(D_RET))
    return jnp.stack(rows)


def kernel(x, positions, ffn1_norm, ffn1_w_gate, ffn1_w_up, ffn1_w_down, mix_norm, w_in, q_norm, w_uq, kv_norm, w_ukv, ret_decay_fwd, ret_decay_bwd, w_o, ffn2_norm, ffn2_w_gate, ffn2_w_up, ffn2_w_down, final_norm):
    b, s, d = x.shape
    n = b * s
    assert ffn1_norm.shape[0] == 1, "specialised to DEPTH == 1 (the final norm is fused into the layer)"
    pos_row = positions.astype(F32).reshape(1, n)

    h = _ffn1(x.reshape(n, d), ffn1_norm[0][None, :], ffn1_w_gate[0], ffn1_w_up[0],
              ffn1_w_down[0])
    qm, km, vm, rq, rk, rv, rg = _proj(
        h, pos_row, mix_norm[0][None, :], w_in[0], q_norm[0][None, :],
        w_uq[0], kv_norm[0][None, :], w_ukv[0])
    a = _mla_attention(qm, km.reshape(b, s, -1), vm)
    r = _retention(rq.reshape(b, s, -1), rk.reshape(b, s, -1), rv.reshape(b, s, -1),
                   rg.reshape(b, s, -1), _prep_decay(ret_decay_fwd[0], ret_decay_bwd[0]))
    wo = w_o[0].astype(BF16)
    out = _out_ffn2(h, a.reshape(n, -1), r.reshape(n, -1), wo, ffn2_norm[0][None, :],
                    ffn2_w_gate[0], ffn2_w_up[0], ffn2_w_down[0], final_norm[None, :])
    return out.reshape(b, s, d)
```
